```python
import math
import jax
import jax.numpy as jnp
from jax import lax
import numpy as np

D_MODEL = 4096
BATCH = 2
SEQ = 8192
DEPTH = 2

CTX_LEN = 256
GRID_W = 64

S5_WIDTH = D_MODEL // 4
S5_GROUP = 16
S5_GROUPS = S5_WIDTH // S5_GROUP
S5_STATE = 64

DN_WIDTH = D_MODEL // 2
DN_HEAD_DIM = 128
DN_HEADS = DN_WIDTH // DN_HEAD_DIM
DN_CHUNK = 64
SHORT_CONV = 3

HY_WIDTH = D_MODEL - S5_WIDTH - DN_WIDTH
HY_BANDS = 16
HY_EMB = 1 + 2 * HY_BANDS
HY_HIDDEN = 64
HY_DECAY_SHORT_PCT = 0.3
HY_DECAY_LONG_PCT = 1.5
HY_DECAY_TARGET = 1e-2

MIX_WIDTH = S5_WIDTH + DN_WIDTH + HY_WIDTH

COL_S5 = 0
COL_DN_QKV = COL_S5 + S5_WIDTH
COL_DN_A = COL_DN_QKV + 3 * DN_WIDTH
COL_DN_B = COL_DN_A + 2 * DN_HEADS
STATE_COLS = COL_DN_B + 2 * DN_HEADS
COL_DN_G = STATE_COLS
COL_HY = COL_DN_G + DN_WIDTH
IN_WIDTH = COL_HY + 3 * HY_WIDTH

N_GROUPS = 4
EXPERTS_PER_GROUP = 8
N_EXPERTS = N_GROUPS * EXPERTS_PER_GROUP
TOP_K = 2
D_EXPERT = 512

DEEPNORM_ALPHA = (2 * DEPTH) ** 0.25
DEEPNORM_BETA = (8 * DEPTH) ** -0.25
LN_EPS = 1e-5
RMS_EPS = 1e-6
F32 = jnp.float32

kernel_name = 'hybrid_s5_deltanet_hyena_hmoe_dit'


def _flip(t, rev):
    return jnp.flip(t, axis=1) if rev else t


def layer_norm(x, g, b):
    xf = x.astype(F32)
    mu = jnp.mean(xf, axis=-1, keepdims=True)
    var = jnp.mean(jnp.square(xf - mu), axis=-1, keepdims=True)
    return ((xf - mu) * lax.rsqrt(var + LN_EPS) * g.astype(F32) + b.astype(F32)).astype(x.dtype)


def rms_norm(x, g):
    xf = x.astype(F32)
    return (xf * lax.rsqrt(jnp.mean(jnp.square(xf), axis=-1, keepdims=True) + RMS_EPS) * g.astype(F32)).astype(x.dtype)


def l2_normalize(t):
    tf = t.astype(F32)
    return tf * lax.rsqrt(jnp.sum(tf * tf, axis=-1, keepdims=True) + 1e-6)


def depthwise_conv(u, w):
    k = w.shape[0]
    return lax.conv_general_dilated(u, w[:, None, :].astype(u.dtype), window_strides=(1,), padding=[(k // 2, k // 2)], dimension_numbers=('NWC', 'WIO', 'NWC'), feature_group_count=u.shape[-1])


def modulate(t, shift, scale):
    return t * (1 + scale) + shift


def s5_discretize(lam_re, lam_im, log_step, b_re, b_im):
    lam_re, lam_im = lam_re.astype(F32), lam_im.astype(F32)
    step = jnp.exp(log_step.astype(F32))[:, None]
    mag = jnp.exp(lam_re * step)
    a_re = mag * jnp.cos(lam_im * step)
    a_im = mag * jnp.sin(lam_im * step)
    den = lam_re * lam_re + lam_im * lam_im
    f_re = ((a_re - 1.0) * lam_re + a_im * lam_im) / den
    f_im = (a_im * lam_re - (a_re - 1.0) * lam_im) / den
    b_re, b_im = b_re.astype(F32), b_im.astype(F32)
    bb_re = f_re[..., None] * b_re - f_im[..., None] * b_im
    bb_im = f_re[..., None] * b_im + f_im[..., None] * b_re
    return a_re, a_im, bb_re, bb_im


def _complex_affine_combine(e1, e2):
    a1r, a1i, b1r, b1i = e1
    a2r, a2i, b2r, b2i = e2
    return (a2r * a1r - a2i * a1i, a2r * a1i + a2i * a1r, a2r * b1r - a2i * b1i + b2r, a2r * b1i + a2i * b1r + b2i)


def s5_states(u, a_re, a_im, bb_re, bb_im, h0_re, h0_im):
    seq_len = u.shape[1]
    bu_re = jnp.einsum('blgh,gph->blgp', u, bb_re)
    bu_im = jnp.einsum('blgh,gph->blgp', u, bb_im)
    a_re = jnp.broadcast_to(a_re, (1, seq_len) + a_re.shape)
    a_im = jnp.broadcast_to(a_im, (1, seq_len) + a_im.shape)
    p_re, p_im, s_re, s_im = lax.associative_scan(_complex_affine_combine, (a_re, a_im, bu_re, bu_im), axis=1)
    h_re = s_re + p_re * h0_re[:, None] - p_im * h0_im[:, None]
    h_im = s_im + p_re * h0_im[:, None] + p_im * h0_re[:, None]
    return h_re, h_im


def s5_readout(h_re, h_im, c_re, c_im):
    return jnp.einsum('blgp,ghp->blgh', h_re, c_re) - jnp.einsum('blgp,ghp->blgh', h_im, c_im)


def s5_glu(y, p):
    y = jax.nn.gelu(y)
    y = y * jax.nn.sigmoid(y @ p['s5_glu_w'] + p['s5_glu_b'])
    return rms_norm(y, p['s5_norm_g'])


def s5_mixer(u, uc, p, need_ctx_out):
    bsz, n, _ = u.shape
    lc = uc.shape[1]
    ug = u.astype(F32).reshape(bsz, n, S5_GROUPS, S5_GROUP)
    ucg = uc.astype(F32).reshape(bsz, lc, S5_GROUPS, S5_GROUP)
    d_skip = p['s5_d'].astype(F32).reshape(S5_GROUPS, S5_GROUP)
    zero = jnp.zeros((bsz, S5_GROUPS, S5_STATE), F32)
    y = ug * d_skip
    yc = ucg * d_skip if need_ctx_out else None
    for d in range(2):
        rev = d == 1
        disc = s5_discretize(p['s5_lam_re'][d], p['s5_lam_im'][d], p['s5_log_step'][d], p['s5_b_re'][d], p['s5_b_im'][d])
        c_re, c_im = p['s5_c_re'][d].astype(F32), p['s5_c_im'][d].astype(F32)
        hc_re, hc_im = s5_states(_flip(ucg, rev), *disc, zero, zero)
        h_re, h_im = s5_states(_flip(ug, rev), *disc, hc_re[:, -1], hc_im[:, -1])
        y = y + _flip(s5_readout(h_re, h_im, c_re, c_im), rev)
        if need_ctx_out:
            yc = yc + _flip(s5_readout(hc_re, hc_im, c_re, c_im), rev)
    out = s5_glu(y.reshape(bsz, n, S5_WIDTH).astype(u.dtype), p)
    out_c = s5_glu(yc.reshape(bsz, lc, S5_WIDTH).astype(u.dtype), p) if need_ctx_out else None
    return out, out_c


def dn_features(z_qkv, conv_w):
    bsz, seq_len, _ = z_qkv.shape
    q, k, v = jnp.split(jax.nn.silu(depthwise_conv(z_qkv, conv_w)), 3, axis=-1)
    shp = (bsz, seq_len, DN_HEADS, DN_HEAD_DIM)
    return l2_normalize(q.reshape(shp)) * DN_HEAD_DIM ** -0.5, l2_normalize(k.reshape(shp)), v.reshape(shp).astype(F32)


def dn_gates(z_a, z_b, a_log, dt_bias):
    bsz, seq_len, _ = z_a.shape
    shp = (bsz, seq_len, 2, DN_HEADS)
    g = -jnp.exp(a_log.astype(F32)) * jax.nn.softplus(z_a.astype(F32).reshape(shp) + dt_bias.astype(F32))
    beta = jax.nn.sigmoid(z_b.astype(F32).reshape(shp))
    return g, beta


def gated_delta_rule(q, k, v, g, beta, s0):
    bsz, seq_len, heads, _ = k.shape
    dv = v.shape[-1]
    n_chunks = seq_len // DN_CHUNK

    def chunks(t):
        return jnp.moveaxis(t.reshape((bsz, n_chunks, DN_CHUNK, heads) + t.shape[3:]), 3, 1)

    q, k, v, beta = chunks(q), chunks(k), chunks(v), chunks(beta)
    g = jnp.cumsum(chunks(g), axis=-1)
    idx = jnp.arange(DN_CHUNK)
    incl = idx[:, None] >= idx[None, :]
    strict = idx[:, None] > idx[None, :]
    decay = jnp.where(incl, jnp.exp(jnp.where(incl, g[..., :, None] - g[..., None, :], 0.0)), 0.0)
    k_beta = k * beta[..., None]
    lower = jnp.where(strict, jnp.einsum('bhnid,bhnjd->bhnij', k_beta, k) * decay, 0.0)
    rhs = jnp.concatenate([v * beta[..., None], k_beta * jnp.exp(g)[..., None]], axis=-1)
    sol = lax.linalg.triangular_solve(lower + jnp.eye(DN_CHUNK, dtype=F32), rhs, left_side=True, lower=True, unit_diagonal=True)
    u, w = sol[..., :dv], sol[..., dv:]
    qk = jnp.einsum('bhnid,bhnjd->bhnij', q, k) * decay
    q_dec = q * jnp.exp(g)[..., None]
    k_tail = k * jnp.exp(g[..., -1:] - g)[..., None]
    g_last = jnp.exp(g[..., -1])[..., None, None]

    def step(state, xs):
        qd_i, kt_i, u_i, w_i, qk_i, gl_i = xs
        v_new = u_i - jnp.einsum('bhck,bhkv->bhcv', w_i, state)
        o_i = jnp.einsum('bhck,bhkv->bhcv', qd_i, state) + jnp.einsum('bhij,bhjv->bhiv', qk_i, v_new)
        state = state * gl_i + jnp.einsum('bhck,bhcv->bhkv', kt_i, v_new)
        return state, o_i

    xs = tuple(jnp.moveaxis(t, 2, 0) for t in (q_dec, k_tail, u, w, qk, g_last))
    s_final, o = lax.scan(step, s0, xs)
    o = jnp.transpose(o, (1, 0, 3, 2, 4)).reshape(bsz, seq_len, heads, dv)
    return o, s_final


def dn_output(o, z_gate, p):
    bsz, seq_len = o.shape[:2]
    gate = jax.nn.silu(z_gate.astype(F32)).reshape(bsz, seq_len, DN_HEADS, DN_HEAD_DIM)
    return (rms_norm(o, p['dn_norm_g']) * gate).reshape(bsz, seq_len, DN_WIDTH).astype(z_gate.dtype)


def deltanet_mixer(z, zc, p, need_ctx_out):
    q, k, v = dn_features(z[..., COL_DN_QKV:COL_DN_A], p['dn_conv_w'])
    g, beta = dn_gates(z[..., COL_DN_A:COL_DN_B], z[..., COL_DN_B:STATE_COLS], p['dn_a_log'], p['dn_dt_bias'])
    qc, kc, vc = dn_features(zc[..., COL_DN_QKV:COL_DN_A], p['dn_conv_w'])
    gc, betac = dn_gates(zc[..., COL_DN_A:COL_DN_B], zc[..., COL_DN_B:STATE_COLS], p['dn_a_log'], p['dn_dt_bias'])
    s0 = jnp.zeros((z.shape[0], DN_HEADS, DN_HEAD_DIM, DN_HEAD_DIM), F32)
    o, oc = 0.0, 0.0
    for d in range(2):
        rev = d == 1
        oc_d, s_ctx = gated_delta_rule(*(_flip(t, rev) for t in (qc, kc, vc, gc[:, :, d], betac[:, :, d])), s0)
        o_d, _ = gated_delta_rule(*(_flip(t, rev) for t in (q, k, v, g[:, :, d], beta[:, :, d])), s_ctx)
        o = o + _flip(o_d, rev)
        if need_ctx_out:
            oc = oc + _flip(oc_d, rev)
    y = dn_output(o, z[..., COL_DN_G:COL_HY], p)
    yc = dn_output(oc, zc[..., COL_DN_G:COL_HY], p) if need_ctx_out else None
    return y, yc


def hyena_filters(seq_len, p):
    t = jnp.linspace(0.0, 1.0, seq_len, dtype=F32)[:, None]
    w = (2.0 * math.pi / seq_len) * jnp.arange(seq_len, dtype=F32)[:, None]
    f = jnp.linspace(1e-4, HY_BANDS - 1, HY_BANDS, dtype=F32)[None, :]
    feats = jnp.concatenate([t, jnp.cos(w * f), -jnp.sin(w * f)], axis=-1)
    freq = p['hy_f_freq'].astype(F32)
    h = jnp.sin(freq * (feats @ p['hy_f_w1'].astype(F32) + p['hy_f_b1'].astype(F32)))
    h = jnp.sin(freq * (h @ p['hy_f_w2'].astype(F32) + p['hy_f_b2'].astype(F32)))
    h = jnp.sin(freq * (h @ p['hy_f_w3'].astype(F32) + p['hy_f_b3'].astype(F32)))
    h = (h @ p['hy_f_w4'].astype(F32)).reshape(seq_len, 2, HY_WIDTH)
    rates = jnp.linspace(math.log(HY_DECAY_TARGET) / HY_DECAY_LONG_PCT, math.log(HY_DECAY_TARGET) / HY_DECAY_SHORT_PCT, HY_WIDTH, dtype=F32)
    h = h * jnp.exp(-t * jnp.abs(rates))[:, None, :]
    return h[:, 0], h[:, 1]


def two_sided_long_conv(u, h_fwd, h_bwd):
    seq_len = u.shape[1]
    taps = jnp.concatenate([h_fwd[:1] + h_bwd[:1], h_fwd[1:], jnp.zeros_like(h_fwd[:1]), h_bwd[:0:-1]], axis=0)
    u_f = jnp.fft.rfft(u.astype(F32), n=2 * seq_len, axis=1)
    t_f = jnp.fft.rfft(taps, axis=0)
    return jnp.fft.irfft(u_f * t_f[None], n=2 * seq_len, axis=1)[:, :seq_len]


def hyena_mixer(zh, p):
    zs = depthwise_conv(zh, p['hy_conv_w']) + p['hy_conv_b']
    x0, x1, v = jnp.split(zs, 3, axis=-1)
    z = (x1 * v).astype(F32)
    h_fwd, h_bwd = hyena_filters(zh.shape[1], p)
    y = x0.astype(F32) * (two_sided_long_conv(z, h_fwd, h_bwd) + p['hy_bias'].astype(F32) * z)
    return rms_norm(y, p['hy_norm_g']).astype(zh.dtype)


def token_mixer(h, hc, p, need_ctx_out):
    z = h @ p['w_in']
    zc = hc @ (p['w_in'] if need_ctx_out else p['w_in'][:, :STATE_COLS])
    s5_y, s5_yc = s5_mixer(z[..., COL_S5:COL_DN_QKV], zc[..., COL_S5:COL_DN_QKV], p, need_ctx_out)
    dn_y, dn_yc = deltanet_mixer(z, zc, p, need_ctx_out)
    y = jnp.concatenate([s5_y, dn_y, hyena_mixer(z[..., COL_HY:], p)], axis=-1) @ p['w_out']
    if not need_ctx_out:
        return y, None
    yc = jnp.concatenate([s5_yc, dn_yc, hyena_mixer(zc[..., COL_HY:], p)], axis=-1) @ p['w_out']
    return y, yc


def hier_moe(t, p):
    group_logits = (t @ p['moe_w_group'] + p['moe_b_group']).astype(F32)
    group = jnp.argmax(group_logits, axis=-1)
    group_w = jnp.max(jax.nn.softmax(group_logits, axis=-1), axis=-1, keepdims=True)
    exp_logits = (t @ p['moe_w_expert'] + p['moe_b_expert']).astype(F32).reshape(t.shape[0], N_GROUPS, EXPERTS_PER_GROUP)
    exp_logits = jnp.einsum('tg,tge->te', jax.nn.one_hot(group, N_GROUPS, dtype=F32), exp_logits)
    top_logits, top_idx = lax.top_k(exp_logits, TOP_K)
    top_w = jax.nn.softmax(top_logits, axis=-1) * group_w
    expert_id = group[:, None] * EXPERTS_PER_GROUP + top_idx
    combine = jnp.einsum('tk,tke->te', top_w, jax.nn.one_hot(expert_id, N_EXPERTS, dtype=F32)).astype(t.dtype)
    out = jnp.zeros_like(t)
    for e in range(N_EXPERTS):
        hidden = jax.nn.silu(t @ p['moe_w_gate'][e]) * (t @ p['moe_w_up'][e]) * combine[:, e:e + 1]
        out = out + hidden @ p['moe_w_down'][e]
    return out


def trunk_layer(x, xc, c, c_ctx, p, last):
    bsz, n, dm = x.shape
    lc = xc.shape[1]
    sh1, sc1, g1, sh2, sc2, g2 = jnp.split((jax.nn.silu(c) @ p['w_ada'] + p['b_ada'])[:, None, :], 6, axis=-1)
    ctx_cols = 2 * dm if last else 6 * dm
    mc = jnp.split(jax.nn.silu(c_ctx) @ p['w_ada'][:, :ctx_cols] + p['b_ada'][:ctx_cols], ctx_cols // dm, axis=-1)
    y, yc = token_mixer(modulate(x, sh1, sc1), modulate(xc, mc[0], mc[1]), p, not last)
    x = layer_norm(DEEPNORM_ALPHA * x + g1 * y, p['ln1_g'], p['ln1_b'])
    if last:
        f = hier_moe(modulate(x, sh2, sc2).reshape(bsz * n, dm), p).reshape(bsz, n, dm)
        return layer_norm(DEEPNORM_ALPHA * x + g2 * f, p['ln2_g'], p['ln2_b']), None
    xc = layer_norm(DEEPNORM_ALPHA * xc + mc[2] * yc, p['ln1_g'], p['ln1_b'])
    tokens = jnp.concatenate([modulate(x, sh2, sc2).reshape(bsz * n, dm), modulate(xc, mc[3], mc[4]).reshape(bsz * lc, dm)], axis=0)
    f = hier_moe(tokens, p)
    x = layer_norm(DEEPNORM_ALPHA * x + g2 * f[:bsz * n].reshape(bsz, n, dm), p['ln2_g'], p['ln2_b'])
    xc = layer_norm(DEEPNORM_ALPHA * xc + mc[5] * f[bsz * n:].reshape(bsz, lc, dm), p['ln2_g'], p['ln2_b'])
    return x, xc


def setup_inputs(seed: int = 0) -> dict:
    key = jax.random.key(seed)
    ks = iter(jax.random.split(key, 64))

    def nrm(shape, scale):
        return jax.random.normal(next(ks), shape, F32) * scale

    def gain(shape):
        return 1.0 + nrm(shape, 0.02)

    def unif(shape, lo, hi):
        return jax.random.uniform(next(ks), shape, F32, lo, hi)

    s5_shape = (DEPTH, 2, S5_GROUPS, S5_STATE)
    lam_im_base = math.pi * jnp.arange(S5_STATE, dtype=F32)
    dt = jnp.exp(unif((DEPTH, 2, DN_HEADS), math.log(1e-3), math.log(1e-1)))
    return {
        'x': nrm((BATCH, SEQ, D_MODEL), 1.0),
        'c': nrm((BATCH, D_MODEL), 1.0),
        'ctx': nrm((BATCH, CTX_LEN, D_MODEL), 1.0),
        'c_ctx': nrm((D_MODEL,), 1.0),
        'w_ada': nrm((DEPTH, D_MODEL, 6 * D_MODEL), 0.5 * D_MODEL ** -0.5),
        'b_ada': nrm((DEPTH, 6 * D_MODEL), 0.02),
        'w_in': nrm((DEPTH, D_MODEL, IN_WIDTH), D_MODEL ** -0.5),
        's5_lam_re': -0.5 + nrm(s5_shape, 0.01),
        's5_lam_im': lam_im_base + nrm(s5_shape, 0.01),
        's5_log_step': unif((DEPTH, 2, S5_GROUPS), math.log(1e-3), math.log(1e-1)),
        's5_b_re': nrm((DEPTH, 2, S5_GROUPS, S5_STATE, S5_GROUP), (2 * S5_GROUP) ** -0.5),
        's5_b_im': nrm((DEPTH, 2, S5_GROUPS, S5_STATE, S5_GROUP), (2 * S5_GROUP) ** -0.5),
        's5_c_re': nrm((DEPTH, 2, S5_GROUPS, S5_GROUP, S5_STATE), S5_STATE ** -0.5),
        's5_c_im': nrm((DEPTH, 2, S5_GROUPS, S5_GROUP, S5_STATE), S5_STATE ** -0.5),
        's5_d': nrm((DEPTH, S5_WIDTH), 1.0),
        's5_glu_w': nrm((DEPTH, S5_WIDTH, S5_WIDTH), S5_WIDTH ** -0.5),
        's5_glu_b': nrm((DEPTH, S5_WIDTH), 0.02),
        's5_norm_g': gain((DEPTH, S5_WIDTH)),
        'dn_conv_w': nrm((DEPTH, SHORT_CONV, 3 * DN_WIDTH), SHORT_CONV ** -0.5),
        'dn_a_log': jnp.log(unif((DEPTH, 2, DN_HEADS), 1.0, 16.0)),
        'dn_dt_bias': dt + jnp.log(-jnp.expm1(-dt)),
        'dn_norm_g': gain((DEPTH, DN_HEAD_DIM)),
        'hy_conv_w': nrm((DEPTH, SHORT_CONV, 3 * HY_WIDTH), SHORT_CONV ** -0.5),
        'hy_conv_b': nrm((DEPTH, 3 * HY_WIDTH), 0.02),
        'hy_f_w1': nrm((DEPTH, HY_EMB, HY_HIDDEN), HY_EMB ** -0.5),
        'hy_f_b1': nrm((DEPTH, HY_HIDDEN), 0.1),
        'hy_f_w2': nrm((DEPTH, HY_HIDDEN, HY_HIDDEN), HY_HIDDEN ** -0.5),
        'hy_f_b2': nrm((DEPTH, HY_HIDDEN), 0.1),
        'hy_f_w3': nrm((DEPTH, HY_HIDDEN, HY_HIDDEN), HY_HIDDEN ** -0.5),
        'hy_f_b3': nrm((DEPTH, HY_HIDDEN), 0.1),
        'hy_f_freq': gain((DEPTH, HY_HIDDEN)),
        'hy_f_w4': nrm((DEPTH, HY_HIDDEN, 2 * HY_WIDTH), HY_HIDDEN ** -0.5),
        'hy_bias': nrm((DEPTH, HY_WIDTH), 1.0),
        'hy_norm_g': gain((DEPTH, HY_WIDTH)),
        'w_out': nrm((DEPTH, MIX_WIDTH, D_MODEL), MIX_WIDTH ** -0.5 * DEEPNORM_BETA),
        'ln1_g': gain((DEPTH, D_MODEL)),
        'ln1_b': nrm((DEPTH, D_MODEL), 0.02),
        'ln2_g': gain((DEPTH, D_MODEL)),
        'ln2_b': nrm((DEPTH, D_MODEL), 0.02),
        'moe_w_group': nrm((DEPTH, D_MODEL, N_GROUPS), D_MODEL ** -0.5),
        'moe_b_group': nrm((DEPTH, N_GROUPS), 0.01),
        'moe_w_expert': nrm((DEPTH, D_MODEL, N_EXPERTS), D_MODEL ** -0.5),
        'moe_b_expert': nrm((DEPTH, N_EXPERTS), 0.01),
        'moe_w_gate': nrm((DEPTH, N_EXPERTS, D_MODEL, D_EXPERT), D_MODEL ** -0.5),
        'moe_w_up': nrm((DEPTH, N_EXPERTS, D_MODEL, D_EXPERT), D_MODEL ** -0.5),
        'moe_w_down': nrm((DEPTH, N_EXPERTS, D_EXPERT, D_MODEL), D_EXPERT ** -0.5 * DEEPNORM_BETA),
    }


def reference(x, c, ctx, c_ctx, w_ada, b_ada, w_in, s5_lam_re, s5_lam_im, s5_log_step, s5_b_re, s5_b_im, s5_c_re, s5_c_im, s5_d, s5_glu_w, s5_glu_b, s5_norm_g, dn_conv_w, dn_a_log, dn_dt_bias, dn_norm_g, hy_conv_w, hy_conv_b, hy_f_w1, hy_f_b1, hy_f_w2, hy_f_b2, hy_f_w3, hy_f_b3, hy_f_freq, hy_f_w4, hy_bias, hy_norm_g, w_out, ln1_g, ln1_b, ln2_g, ln2_b, moe_w_group, moe_b_group, moe_w_expert, moe_b_expert, moe_w_gate, moe_w_up, moe_w_down):
    xc = ctx
    for l in range(DEPTH):
        p = {
            'w_ada': w_ada[l], 'b_ada': b_ada[l], 'w_in': w_in[l],
            's5_lam_re': s5_lam_re[l], 's5_lam_im': s5_lam_im[l], 's5_log_step': s5_log_step[l],
            's5_b_re': s5_b_re[l], 's5_b_im': s5_b_im[l], 's5_c_re': s5_c_re[l], 's5_c_im': s5_c_im[l],
            's5_d': s5_d[l], 's5_glu_w': s5_glu_w[l], 's5_glu_b': s5_glu_b[l], 's5_norm_g': s5_norm_g[l],
            'dn_conv_w': dn_conv_w[l], 'dn_a_log': dn_a_log[l], 'dn_dt_bias': dn_dt_bias[l], 'dn_norm_g': dn_norm_g[l],
            'hy_conv_w': hy_conv_w[l], 'hy_conv_b': hy_conv_b[l],
            'hy_f_w1': hy_f_w1[l], 'hy_f_b1': hy_f_b1[l], 'hy_f_w2': hy_f_w2[l], 'hy_f_b2': hy_f_b2[l],
            'hy_f_w3': hy_f_w3[l], 'hy_f_b3': hy_f_b3[l], 'hy_f_freq': hy_f_freq[l], 'hy_f_w4': hy_f_w4[l],
            'hy_bias': hy_bias[l], 'hy_norm_g': hy_norm_g[l], 'w_out': w_out[l],
            'ln1_g': ln1_g[l], 'ln1_b': ln1_b[l], 'ln2_g': ln2_g[l], 'ln2_b': ln2_b[l],
            'moe_w_group': moe_w_group[l], 'moe_b_group': moe_b_group[l],
            'moe_w_expert': moe_w_expert[l], 'moe_b_expert': moe_b_expert[l],
            'moe_w_gate': moe_w_gate[l], 'moe_w_up': moe_w_up[l], 'moe_w_down': moe_w_down[l],
        }
        x, xc = trunk_layer(x, xc, c, c_ctx, p, l == DEPTH - 1)
    return x
```

```python
import functools
import math

import jax
import jax.numpy as jnp
from jax import lax
from jax.experimental import pallas as pl
from jax.experimental.pallas import tpu as pltpu

D_MODEL = 4096
DEPTH = 2
GRID_W = 64

S5_WIDTH = D_MODEL // 4
S5_GROUP = 16
S5_GROUPS = S5_WIDTH // S5_GROUP
S5_STATE = 64

DN_WIDTH = D_MODEL // 2
DN_HEAD_DIM = 128
DN_HEADS = DN_WIDTH // DN_HEAD_DIM
DN_CHUNK = 64
SHORT_CONV = 3

HY_WIDTH = D_MODEL - S5_WIDTH - DN_WIDTH
HY_BANDS = 16
HY_EMB = 1 + 2 * HY_BANDS
HY_HIDDEN = 64
HY_DECAY_SHORT_PCT = 0.3
HY_DECAY_LONG_PCT = 1.5
HY_DECAY_TARGET = 1e-2

MIX_WIDTH = S5_WIDTH + DN_WIDTH + HY_WIDTH

COL_S5 = 0
COL_DN_QKV = COL_S5 + S5_WIDTH
COL_DN_A = COL_DN_QKV + 3 * DN_WIDTH
COL_DN_B = COL_DN_A + 2 * DN_HEADS
STATE_COLS = COL_DN_B + 2 * DN_HEADS
COL_DN_G = STATE_COLS
COL_HY = COL_DN_G + DN_WIDTH
IN_WIDTH = COL_HY + 3 * HY_WIDTH

N_GROUPS = 4
EXPERTS_PER_GROUP = 8
N_EXPERTS = N_GROUPS * EXPERTS_PER_GROUP
TOP_K = 2
D_EXPERT = 512

DEEPNORM_ALPHA = (2 * DEPTH) ** 0.25
LN_EPS = 1e-5
RMS_EPS = 1e-6
F32 = jnp.float32
BF16 = jnp.bfloat16

V7X_VMEM_BYTES = 64 * 1024 * 1024
VMEM_LIMIT = 52 * 1024 * 1024
LANE = 128

MOE_TILE = 256


def _mm_kernel(a_ref, b_ref, o_ref):
    o_ref[...] = jnp.dot(a_ref[...].astype(BF16), b_ref[...].astype(BF16), preferred_element_type=F32).astype(o_ref.dtype)


def _pick(n, prefs):
    for p in prefs:
        if n % p == 0:
            return p
    return n


def matmul(a, b, out_dtype=F32, name="matmul"):
    m, k = a.shape
    n = b.shape[1]
    tm = _pick(m, (512, 256, 128, 64, 32, 16, 8))
    tn = _pick(n, (1024, 512, 256, 128))
    return pl.pallas_call(
        _mm_kernel,
        grid=(m // tm, n // tn),
        in_specs=[pl.BlockSpec((tm, k), lambda i, j: (i, 0)), pl.BlockSpec((k, tn), lambda i, j: (0, j))],
        out_specs=pl.BlockSpec((tm, tn), lambda i, j: (i, j)),
        out_shape=jax.ShapeDtypeStruct((m, n), out_dtype),
        compiler_params=pltpu.CompilerParams(dimension_semantics=("parallel", "parallel"), vmem_limit_bytes=VMEM_LIMIT),
        name=name,
    )(a, b)


def _mm_f32_kernel(a_ref, b_ref, o_ref):
    o_ref[...] = jnp.dot(a_ref[...], b_ref[...], preferred_element_type=F32, precision=lax.Precision.HIGHEST)


def matmul_f32(a, b, name="matmul_f32"):
    m, k = a.shape
    n = b.shape[1]
    tm = _pick(m, (512, 256, 128, 64, 32, 16, 8))
    return pl.pallas_call(
        _mm_f32_kernel,
        grid=(m // tm,),
        in_specs=[pl.BlockSpec((tm, k), lambda i: (i, 0)), pl.BlockSpec((k, n), lambda i: (0, 0))],
        out_specs=pl.BlockSpec((tm, n), lambda i: (i, 0)),
        out_shape=jax.ShapeDtypeStruct((m, n), F32),
        compiler_params=pltpu.CompilerParams(dimension_semantics=("parallel",), vmem_limit_bytes=VMEM_LIMIT),
        name=name,
    )(a, b)


def _moe_kernel(tile_expert_ref, n_used_ref, x_ref, w_ref, wg_ref, wu_ref, wd_ref, o_ref):
    i = pl.program_id(0)

    @pl.when(i < n_used_ref[0])
    def _():
        x = x_ref[...]
        g = jnp.dot(x, wg_ref[0], preferred_element_type=F32)
        u = jnp.dot(x, wu_ref[0], preferred_element_type=F32)
        h = (g * jax.nn.sigmoid(g)) * u * w_ref[...]
        o_ref[...] = jnp.dot(h.astype(BF16), wd_ref[0], preferred_element_type=F32)

    @pl.when(i >= n_used_ref[0])
    def _():
        o_ref[...] = jnp.zeros_like(o_ref)


def moe_experts(tile_expert, n_used, xs, row_w, wg, wu, wd):
    p_rows, dm = xs.shape
    n_tiles = p_rows // MOE_TILE
    grid_spec = pltpu.PrefetchScalarGridSpec(
        num_scalar_prefetch=2,
        grid=(n_tiles,),
        in_specs=[
            pl.BlockSpec((MOE_TILE, dm), lambda i, te, nu: (i, 0)),
            pl.BlockSpec((MOE_TILE, 1), lambda i, te, nu: (i, 0)),
            pl.BlockSpec((1, dm, D_EXPERT), lambda i, te, nu: (te[i], 0, 0)),
            pl.BlockSpec((1, dm, D_EXPERT), lambda i, te, nu: (te[i], 0, 0)),
            pl.BlockSpec((1, D_EXPERT, dm), lambda i, te, nu: (te[i], 0, 0)),
        ],
        out_specs=pl.BlockSpec((MOE_TILE, dm), lambda i, te, nu: (i, 0)),
    )
    return pl.pallas_call(
        _moe_kernel,
        grid_spec=grid_spec,
        out_shape=jax.ShapeDtypeStruct((p_rows, dm), F32),
        compiler_params=pltpu.CompilerParams(dimension_semantics=("arbitrary",), vmem_limit_bytes=VMEM_LIMIT),
        name="moe_experts",
    )(tile_expert, n_used, xs, row_w, wg, wu, wd)


def hier_moe(t, p):
    n_tok = t.shape[0]
    logits = matmul_f32(t, p['moe_w_router'], name="moe_router")
    group_logits = logits[:, :N_GROUPS] + p['moe_b_group']
    group = jnp.argmax(group_logits, axis=-1)
    group_w = jnp.max(jax.nn.softmax(group_logits, axis=-1), axis=-1, keepdims=True)
    exp_logits = (logits[:, N_GROUPS:N_GROUPS + N_EXPERTS] + p['moe_b_expert']).reshape(n_tok, N_GROUPS, EXPERTS_PER_GROUP)
    exp_logits = jnp.take_along_axis(exp_logits, group[:, None, None], axis=1)[:, 0]
    top_logits, top_idx = lax.top_k(exp_logits, TOP_K)
    top_w = jax.nn.softmax(top_logits, axis=-1) * group_w
    expert_id = (group[:, None] * EXPERTS_PER_GROUP + top_idx).astype(jnp.int32)

    n_rows = n_tok * TOP_K
    flat_e = expert_id.reshape(n_rows)
    flat_w = top_w.reshape(n_rows)
    flat_t = jnp.arange(n_rows, dtype=jnp.int32) // TOP_K
    order = jnp.argsort(flat_e, stable=True)
    sorted_e = flat_e[order]
    counts = jnp.zeros((N_EXPERTS,), jnp.int32).at[flat_e].add(1)
    padded = ((counts + MOE_TILE - 1) // MOE_TILE) * MOE_TILE
    pad_end = jnp.cumsum(padded)
    pad_start = pad_end - padded
    start = jnp.cumsum(counts) - counts
    dest = pad_start[sorted_e] + (jnp.arange(n_rows, dtype=jnp.int32) - start[sorted_e])
    p_rows = n_rows + N_EXPERTS * MOE_TILE
    n_tiles = p_rows // MOE_TILE
    row_token = jnp.zeros((p_rows,), jnp.int32).at[dest].set(flat_t[order])
    row_w = jnp.zeros((p_rows,), F32).at[dest].set(flat_w[order])
    pos = jnp.zeros((n_rows,), jnp.int32).at[order].set(dest).reshape(n_tok, TOP_K)
    n_used = (pad_end[-1] // MOE_TILE).astype(jnp.int32)
    tile_idx = jnp.minimum(jnp.arange(n_tiles, dtype=jnp.int32), n_used - 1)
    tile_expert = jnp.minimum(jnp.searchsorted(pad_end, tile_idx * MOE_TILE, side='right'), N_EXPERTS - 1).astype(jnp.int32)

    xs = t.astype(BF16)[row_token]
    ys = moe_experts(tile_expert, n_used.reshape(1), xs, row_w[:, None], p['moe_w_gate'], p['moe_w_up'], p['moe_w_down'])
    return ys[pos[:, 0]] + ys[pos[:, 1]]


def _flip(t, rev):
    return jnp.flip(t, axis=1) if rev else t


def layer_norm(x, g, b):
    mu = jnp.mean(x, axis=-1, keepdims=True)
    var = jnp.mean(jnp.square(x - mu), axis=-1, keepdims=True)
    return (x - mu) * lax.rsqrt(var + LN_EPS) * g + b


def rms_norm(x, g):
    return x * lax.rsqrt(jnp.mean(jnp.square(x), axis=-1, keepdims=True) + RMS_EPS) * g


def l2_normalize(t):
    return t * lax.rsqrt(jnp.sum(t * t, axis=-1, keepdims=True) + 1e-6)


def depthwise_conv(u, w):
    k = w.shape[0]
    return lax.conv_general_dilated(u, w[:, None, :], window_strides=(1,), padding=[(k // 2, k // 2)], dimension_numbers=('NWC', 'WIO', 'NWC'), feature_group_count=u.shape[-1])


def modulate(t, shift, scale):
    return t * (1 + scale) + shift


def mm3(t, w, name):
    bsz, n, k = t.shape
    return matmul(t.reshape(bsz * n, k), w, name=name).reshape(bsz, n, w.shape[1])


def s5_discretize(lam_re, lam_im, log_step, b_re, b_im):
    step = jnp.exp(log_step)[:, None]
    mag = jnp.exp(lam_re * step)
    a_re = mag * jnp.cos(lam_im * step)
    a_im = mag * jnp.sin(lam_im * step)
    den = lam_re * lam_re + lam_im * lam_im
    f_re = ((a_re - 1.0) * lam_re + a_im * lam_im) / den
    f_im = (a_im * lam_re - (a_re - 1.0) * lam_im) / den
    bb_re = f_re[..., None] * b_re - f_im[..., None] * b_im
    bb_im = f_re[..., None] * b_im + f_im[..., None] * b_re
    return a_re, a_im, bb_re, bb_im


def _complex_affine_combine(e1, e2):
    a1r, a1i, b1r, b1i = e1
    a2r, a2i, b2r, b2i = e2
    return (a2r * a1r - a2i * a1i, a2r * a1i + a2i * a1r, a2r * b1r - a2i * b1i + b2r, a2r * b1i + a2i * b1r + b2i)


def s5_states(u, a_re, a_im, bb_re, bb_im, h0_re, h0_im):
    seq_len = u.shape[1]
    bu_re = jnp.einsum('blgh,gph->blgp', u, bb_re)
    bu_im = jnp.einsum('blgh,gph->blgp', u, bb_im)
    a_re = jnp.broadcast_to(a_re, (1, seq_len) + a_re.shape)
    a_im = jnp.broadcast_to(a_im, (1, seq_len) + a_im.shape)
    p_re, p_im, s_re, s_im = lax.associative_scan(_complex_affine_combine, (a_re, a_im, bu_re, bu_im), axis=1)
    h_re = s_re + p_re * h0_re[:, None] - p_im * h0_im[:, None]
    h_im = s_im + p_re * h0_im[:, None] + p_im * h0_re[:, None]
    return h_re, h_im


def s5_readout(h_re, h_im, c_re, c_im):
    return jnp.einsum('blgp,ghp->blgh', h_re, c_re) - jnp.einsum('blgp,ghp->blgh', h_im, c_im)


def s5_glu(y, p):
    y = jax.nn.gelu(y)
    y = y * jax.nn.sigmoid(mm3(y, p['s5_glu_w'], "s5_glu") + p['s5_glu_b'])
    return rms_norm(y, p['s5_norm_g'])


def s5_mixer(u, uc, p, need_ctx_out):
    bsz, n, _ = u.shape
    lc = uc.shape[1]
    ug = u.reshape(bsz, n, S5_GROUPS, S5_GROUP)
    ucg = uc.reshape(bsz, lc, S5_GROUPS, S5_GROUP)
    d_skip = p['s5_d'].reshape(S5_GROUPS, S5_GROUP)
    zero = jnp.zeros((bsz, S5_GROUPS, S5_STATE), F32)
    y = ug * d_skip
    yc = ucg * d_skip if need_ctx_out else None
    for d in range(2):
        rev = d == 1
        disc = s5_discretize(p['s5_lam_re'][d], p['s5_lam_im'][d], p['s5_log_step'][d], p['s5_b_re'][d], p['s5_b_im'][d])
        c_re, c_im = p['s5_c_re'][d], p['s5_c_im'][d]
        hc_re, hc_im = s5_states(_flip(ucg, rev), *disc, zero, zero)
        h_re, h_im = s5_states(_flip(ug, rev), *disc, hc_re[:, -1], hc_im[:, -1])
        y = y + _flip(s5_readout(h_re, h_im, c_re, c_im), rev)
        if need_ctx_out:
            yc = yc + _flip(s5_readout(hc_re, hc_im, c_re, c_im), rev)
    out = s5_glu(y.reshape(bsz, n, S5_WIDTH), p)
    out_c = s5_glu(yc.reshape(bsz, lc, S5_WIDTH), p) if need_ctx_out else None
    return out, out_c


def dn_features(z_qkv, conv_w):
    bsz, seq_len, _ = z_qkv.shape
    q, k, v = jnp.split(jax.nn.silu(depthwise_conv(z_qkv, conv_w)), 3, axis=-1)
    shp = (bsz, seq_len, DN_HEADS, DN_HEAD_DIM)
    return l2_normalize(q.reshape(shp)) * DN_HEAD_DIM ** -0.5, l2_normalize(k.reshape(shp)), v.reshape(shp)


def dn_gates(z_a, z_b, a_log, dt_bias):
    bsz, seq_len, _ = z_a.shape
    shp = (bsz, seq_len, 2, DN_HEADS)
    g = -jnp.exp(a_log) * jax.nn.softplus(z_a.reshape(shp) + dt_bias)
    beta = jax.nn.sigmoid(z_b.reshape(shp))
    return g, beta


def gated_delta_rule(q, k, v, g, beta, s0):
    bsz, seq_len, heads, _ = k.shape
    dv = v.shape[-1]
    n_chunks = seq_len // DN_CHUNK

    def chunks(t):
        return jnp.moveaxis(t.reshape((bsz, n_chunks, DN_CHUNK, heads) + t.shape[3:]), 3, 1)

    q, k, v, beta = chunks(q), chunks(k), chunks(v), chunks(beta)
    g = jnp.cumsum(chunks(g), axis=-1)
    idx = jnp.arange(DN_CHUNK)
    incl = idx[:, None] >= idx[None, :]
    strict = idx[:, None] > idx[None, :]
    decay = jnp.where(incl, jnp.exp(jnp.where(incl, g[..., :, None] - g[..., None, :], 0.0)), 0.0)
    k_beta = k * beta[..., None]
    hp = lax.Precision.HIGHEST
    lower = jnp.where(strict, jnp.einsum('bhnid,bhnjd->bhnij', k_beta, k, precision=hp) * decay, 0.0)
    rhs = jnp.concatenate([v * beta[..., None], k_beta * jnp.exp(g)[..., None]], axis=-1)
    sol = lax.linalg.triangular_solve(lower + jnp.eye(DN_CHUNK, dtype=F32), rhs, left_side=True, lower=True, unit_diagonal=True)
    u, w = sol[..., :dv], sol[..., dv:]
    qk = jnp.einsum('bhnid,bhnjd->bhnij', q, k, precision=hp) * decay
    q_dec = q * jnp.exp(g)[..., None]
    k_tail = k * jnp.exp(g[..., -1:] - g)[..., None]
    g_last = jnp.exp(g[..., -1])[..., None, None]

    def step(state, xs):
        qd_i, kt_i, u_i, w_i, qk_i, gl_i = xs
        v_new = u_i - jnp.einsum('bhck,bhkv->bhcv', w_i, state, precision=hp)
        o_i = jnp.einsum('bhck,bhkv->bhcv', qd_i, state, precision=hp) + jnp.einsum('bhij,bhjv->bhiv', qk_i, v_new, precision=hp)
        state = state * gl_i + jnp.einsum('bhck,bhcv->bhkv', kt_i, v_new, precision=hp)
        return state, o_i

    xs = tuple(jnp.moveaxis(t, 2, 0) for t in (q_dec, k_tail, u, w, qk, g_last))
    s_final, o = lax.scan(step, s0, xs)
    o = jnp.transpose(o, (1, 0, 3, 2, 4)).reshape(bsz, seq_len, heads, dv)
    return o, s_final


def dn_output(o, z_gate, p):
    bsz, seq_len = o.shape[:2]
    gate = jax.nn.silu(z_gate).reshape(bsz, seq_len, DN_HEADS, DN_HEAD_DIM)
    return (rms_norm(o, p['dn_norm_g']) * gate).reshape(bsz, seq_len, DN_WIDTH)


def deltanet_mixer(z, zc, p, need_ctx_out):
    q, k, v = dn_features(z['qkv'], p['dn_conv_w'])
    g, beta = dn_gates(z['a'], z['b'], p['dn_a_log'], p['dn_dt_bias'])
    qc, kc, vc = dn_features(zc['qkv'], p['dn_conv_w'])
    gc, betac = dn_gates(zc['a'], zc['b'], p['dn_a_log'], p['dn_dt_bias'])
    s0 = jnp.zeros((q.shape[0], DN_HEADS, DN_HEAD_DIM, DN_HEAD_DIM), F32)
    o, oc = 0.0, 0.0
    for d in range(2):
        rev = d == 1
        oc_d, s_ctx = gated_delta_rule(*(_flip(t, rev) for t in (qc, kc, vc, gc[:, :, d], betac[:, :, d])), s0)
        o_d, _ = gated_delta_rule(*(_flip(t, rev) for t in (q, k, v, g[:, :, d], beta[:, :, d])), s_ctx)
        o = o + _flip(o_d, rev)
        if need_ctx_out:
            oc = oc + _flip(oc_d, rev)
    y = dn_output(o, z['g'], p)
    yc = dn_output(oc, zc['g'], p) if need_ctx_out else None
    return y, yc


def hyena_filters(seq_len, p):
    hp = lax.Precision.HIGHEST
    t = jnp.linspace(0.0, 1.0, seq_len, dtype=F32)[:, None]
    w = (2.0 * math.pi / seq_len) * jnp.arange(seq_len, dtype=F32)[:, None]
    f = jnp.linspace(1e-4, HY_BANDS - 1, HY_BANDS, dtype=F32)[None, :]
    feats = jnp.concatenate([t, jnp.cos(w * f), -jnp.sin(w * f)], axis=-1)
    freq = p['hy_f_freq']
    h = jnp.sin(freq * (jnp.dot(feats, p['hy_f_w1'], precision=hp) + p['hy_f_b1']))
    h = jnp.sin(freq * (jnp.dot(h, p['hy_f_w2'], precision=hp) + p['hy_f_b2']))
    h = jnp.sin(freq * (jnp.dot(h, p['hy_f_w3'], precision=hp) + p['hy_f_b3']))
    h = jnp.dot(h, p['hy_f_w4'], precision=hp).reshape(seq_len, 2, HY_WIDTH)
    rates = jnp.linspace(math.log(HY_DECAY_TARGET) / HY_DECAY_LONG_PCT, math.log(HY_DECAY_TARGET) / HY_DECAY_SHORT_PCT, HY_WIDTH, dtype=F32)
    h = h * jnp.exp(-t * jnp.abs(rates))[:, None, :]
    return h[:, 0], h[:, 1]


def two_sided_long_conv(u, h_fwd, h_bwd):
    seq_len = u.shape[1]
    taps = jnp.concatenate([h_fwd[:1] + h_bwd[:1], h_fwd[1:], jnp.zeros_like(h_fwd[:1]), h_bwd[:0:-1]], axis=0)
    u_f = jnp.fft.rfft(u, n=2 * seq_len, axis=1)
    t_f = jnp.fft.rfft(taps, axis=0)
    return jnp.fft.irfft(u_f * t_f[None], n=2 * seq_len, axis=1)[:, :seq_len]


def hyena_mixer(zh, p):
    zs = depthwise_conv(zh, p['hy_conv_w']) + p['hy_conv_b']
    x0, x1, v = jnp.split(zs, 3, axis=-1)
    z = x1 * v
    h_fwd, h_bwd = hyena_filters(zh.shape[1], p)
    y = x0 * (two_sided_long_conv(z, h_fwd, h_bwd) + p['hy_bias'] * z)
    return rms_norm(y, p['hy_norm_g'])


def in_projection(h, p, full):
    bsz, n, k = h.shape
    hb = h.reshape(bsz * n, k).astype(BF16)
    out = {}
    out['s5'] = matmul(hb, p['w_in_s5'], name="in_s5").reshape(bsz, n, -1)
    out['qkv'] = matmul(hb, p['w_in_qkv'], name="in_qkv").reshape(bsz, n, -1)
    ab = matmul(hb, p['w_in_ab'], name="in_ab").reshape(bsz, n, -1)
    out['a'] = ab[..., :2 * DN_HEADS]
    out['b'] = ab[..., 2 * DN_HEADS:4 * DN_HEADS]
    if full:
        out['g'] = matmul(hb, p['w_in_g'], name="in_g").reshape(bsz, n, -1)
        out['hy'] = matmul(hb, p['w_in_hy'], name="in_hy").reshape(bsz, n, -1)
    return out


def token_mixer(h, hc, p, need_ctx_out):
    z = in_projection(h, p, True)
    zc = in_projection(hc, p, need_ctx_out)
    s5_y, s5_yc = s5_mixer(z['s5'], zc['s5'], p, need_ctx_out)
    dn_y, dn_yc = deltanet_mixer(z, zc, p, need_ctx_out)
    y = mm3(jnp.concatenate([s5_y, dn_y, hyena_mixer(z['hy'], p)], axis=-1).astype(BF16), p['w_out'], "out_proj")
    if not need_ctx_out:
        return y, None
    yc = mm3(jnp.concatenate([s5_yc, dn_yc, hyena_mixer(zc['hy'], p)], axis=-1).astype(BF16), p['w_out'], "out_proj_ctx")
    return y, yc


def ada_modulation(c_rows, w_ada, b_ada):
    rows = c_rows.shape[0]
    a = jnp.zeros((8, c_rows.shape[1]), F32).at[:rows].set(jax.nn.silu(c_rows))
    return matmul(a, w_ada, name="ada")[:rows] + b_ada


def trunk_layer(x, xc, c, c_ctx, p, last):
    bsz, n, dm = x.shape
    lc = xc.shape[1]
    ada = ada_modulation(jnp.concatenate([c, c_ctx[None]], axis=0), p['w_ada'], p['b_ada'])
    sh1, sc1, g1, sh2, sc2, g2 = jnp.split(ada[:bsz, None, :], 6, axis=-1)
    mc = jnp.split(ada[bsz], 6, axis=-1)
    y, yc = token_mixer(modulate(x, sh1, sc1), modulate(xc, mc[0], mc[1]), p, not last)
    x = layer_norm(DEEPNORM_ALPHA * x + g1 * y, p['ln1_g'], p['ln1_b'])
    if last:
        f = hier_moe(modulate(x, sh2, sc2).reshape(bsz * n, dm), p).reshape(bsz, n, dm)
        return layer_norm(DEEPNORM_ALPHA * x + g2 * f, p['ln2_g'], p['ln2_b']), None
    xc = layer_norm(DEEPNORM_ALPHA * xc + mc[2] * yc, p['ln1_g'], p['ln1_b'])
    tokens = jnp.concatenate([modulate(x, sh2, sc2).reshape(bsz * n, dm), modulate(xc, mc[3], mc[4]).reshape(bsz * lc, dm)], axis=0)
    f = hier_moe(tokens, p)
    x = layer_norm(DEEPNORM_ALPHA * x + g2 * f[:bsz * n].reshape(bsz, n, dm), p['ln2_g'], p['ln2_b'])
    xc = layer_norm(DEEPNORM_ALPHA * xc + mc[5] * f[bsz * n:].reshape(bsz, lc, dm), p['ln2_g'], p['ln2_b'])
    return x, xc


def kernel(x, c, ctx, c_ctx, w_ada, b_ada, w_in, s5_lam_re, s5_lam_im, s5_log_step, s5_b_re, s5_b_im, s5_c_re, s5_c_im, s5_d, s5_glu_w, s5_glu_b, s5_norm_g, dn_conv_w, dn_a_log, dn_dt_bias, dn_norm_g, hy_conv_w, hy_conv_b, hy_f_w1, hy_f_b1, hy_f_w2, hy_f_b2, hy_f_w3, hy_f_b3, hy_f_freq, hy_f_w4, hy_bias, hy_norm_g, w_out, ln1_g, ln1_b, ln2_g, ln2_b, moe_w_group, moe_b_group, moe_w_expert, moe_b_expert, moe_w_gate, moe_w_up, moe_w_down):
    xc = ctx
    for l in range(DEPTH):
        w_in_l = w_in[l]
        w_ab = jnp.zeros((D_MODEL, LANE), F32).at[:, :4 * DN_HEADS].set(w_in_l[:, COL_DN_A:STATE_COLS])
        w_router = jnp.zeros((D_MODEL, LANE), F32).at[:, :N_GROUPS].set(moe_w_group[l]).at[:, N_GROUPS:N_GROUPS + N_EXPERTS].set(moe_w_expert[l])
        p = {
            'w_ada': w_ada[l], 'b_ada': b_ada[l],
            'w_in_s5': w_in_l[:, COL_S5:COL_DN_QKV].astype(BF16),
            'w_in_qkv': w_in_l[:, COL_DN_QKV:COL_DN_A].astype(BF16),
            'w_in_ab': w_ab.astype(BF16),
            'w_in_g': w_in_l[:, COL_DN_G:COL_HY].astype(BF16),
            'w_in_hy': w_in_l[:, COL_HY:].astype(BF16),
            's5_lam_re': s5_lam_re[l], 's5_lam_im': s5_lam_im[l], 's5_log_step': s5_log_step[l],
            's5_b_re': s5_b_re[l], 's5_b_im': s5_b_im[l], 's5_c_re': s5_c_re[l], 's5_c_im': s5_c_im[l],
            's5_d': s5_d[l], 's5_glu_w': s5_glu_w[l].astype(BF16), 's5_glu_b': s5_glu_b[l], 's5_norm_g': s5_norm_g[l],
            'dn_conv_w': dn_conv_w[l], 'dn_a_log': dn_a_log[l], 'dn_dt_bias': dn_dt_bias[l], 'dn_norm_g': dn_norm_g[l],
            'hy_conv_w': hy_conv_w[l], 'hy_conv_b': hy_conv_b[l],
            'hy_f_w1': hy_f_w1[l], 'hy_f_b1': hy_f_b1[l], 'hy_f_w2': hy_f_w2[l], 'hy_f_b2': hy_f_b2[l],
            'hy_f_w3': hy_f_w3[l], 'hy_f_b3': hy_f_b3[l], 'hy_f_freq': hy_f_freq[l], 'hy_f_w4': hy_f_w4[l],
            'hy_bias': hy_bias[l], 'hy_norm_g': hy_norm_g[l], 'w_out': w_out[l].astype(BF16),
            'ln1_g': ln1_g[l], 'ln1_b': ln1_b[l], 'ln2_g': ln2_g[l], 'ln2_b': ln2_b[l],
            'moe_w_router': w_router, 'moe_b_group': moe_b_group[l], 'moe_b_expert': moe_b_expert[l],
            'moe_w_gate': moe_w_gate[l].astype(BF16), 'moe_w_up': moe_w_up[l].astype(BF16), 'moe_w_down': moe_w_down[l].astype(BF16),
        }
        x, xc = trunk_layer(x, xc, c, c_ctx, p, l == DEPTH - 1)
    return x
```

```python
import functools
import math

import jax
import jax.numpy as jnp
from jax import lax
from jax.experimental import pallas as pl
from jax.experimental.pallas import tpu as pltpu

D_MODEL = 4096
DEPTH = 2
GRID_W = 64

S5_WIDTH = D_MODEL // 4
S5_GROUP = 16
S5_GROUPS = S5_WIDTH // S5_GROUP
S5_STATE = 64

DN_WIDTH = D_MODEL // 2
DN_HEAD_DIM = 128
DN_HEADS = DN_WIDTH // DN_HEAD_DIM
DN_CHUNK = 64
SHORT_CONV = 3

HY_WIDTH = D_MODEL - S5_WIDTH - DN_WIDTH
HY_BANDS = 16
HY_EMB = 1 + 2 * HY_BANDS
HY_HIDDEN = 64
HY_DECAY_SHORT_PCT = 0.3
HY_DECAY_LONG_PCT = 1.5
HY_DECAY_TARGET = 1e-2

MIX_WIDTH = S5_WIDTH + DN_WIDTH + HY_WIDTH

COL_S5 = 0
COL_DN_QKV = COL_S5 + S5_WIDTH
COL_DN_A = COL_DN_QKV + 3 * DN_WIDTH
COL_DN_B = COL_DN_A + 2 * DN_HEADS
STATE_COLS = COL_DN_B + 2 * DN_HEADS
COL_DN_G = STATE_COLS
COL_HY = COL_DN_G + DN_WIDTH
IN_WIDTH = COL_HY + 3 * HY_WIDTH

N_GROUPS = 4
EXPERTS_PER_GROUP = 8
N_EXPERTS = N_GROUPS * EXPERTS_PER_GROUP
TOP_K = 2
D_EXPERT = 512

DEEPNORM_ALPHA = (2 * DEPTH) ** 0.25
LN_EPS = 1e-5
RMS_EPS = 1e-6
F32 = jnp.float32
BF16 = jnp.bfloat16

V7X_VMEM_BYTES = 64 * 1024 * 1024
VMEM_LIMIT = 52 * 1024 * 1024
LANE = 128

MOE_TILE = 256
S5_T = 64


def _mm_kernel(a_ref, b_ref, o_ref):
    o_ref[...] = jnp.dot(a_ref[...].astype(BF16), b_ref[...].astype(BF16), preferred_element_type=F32).astype(o_ref.dtype)


def _pick(n, prefs):
    for p in prefs:
        if n % p == 0:
            return p
    return n


def matmul(a, b, out_dtype=F32, name="matmul"):
    m, k = a.shape
    n = b.shape[1]
    tm = _pick(m, (512, 256, 128, 64, 32, 16, 8))
    tn = _pick(n, (1024, 512, 256, 128))
    return pl.pallas_call(
        _mm_kernel,
        grid=(m // tm, n // tn),
        in_specs=[pl.BlockSpec((tm, k), lambda i, j: (i, 0)), pl.BlockSpec((k, tn), lambda i, j: (0, j))],
        out_specs=pl.BlockSpec((tm, tn), lambda i, j: (i, j)),
        out_shape=jax.ShapeDtypeStruct((m, n), out_dtype),
        compiler_params=pltpu.CompilerParams(dimension_semantics=("parallel", "parallel"), vmem_limit_bytes=VMEM_LIMIT),
        name=name,
    )(a, b)


def _mm_f32_kernel(a_ref, b_ref, o_ref):
    o_ref[...] = jnp.dot(a_ref[...], b_ref[...], preferred_element_type=F32, precision=lax.Precision.HIGHEST)


def matmul_f32(a, b, name="matmul_f32"):
    m, k = a.shape
    n = b.shape[1]
    tm = _pick(m, (512, 256, 128, 64, 32, 16, 8))
    return pl.pallas_call(
        _mm_f32_kernel,
        grid=(m // tm,),
        in_specs=[pl.BlockSpec((tm, k), lambda i: (i, 0)), pl.BlockSpec((k, n), lambda i: (0, 0))],
        out_specs=pl.BlockSpec((tm, n), lambda i: (i, 0)),
        out_shape=jax.ShapeDtypeStruct((m, n), F32),
        compiler_params=pltpu.CompilerParams(dimension_semantics=("parallel",), vmem_limit_bytes=VMEM_LIMIT),
        name=name,
    )(a, b)


def _moe_kernel(tile_expert_ref, n_used_ref, x_ref, w_ref, wg_ref, wu_ref, wd_ref, o_ref):
    i = pl.program_id(0)

    @pl.when(i < n_used_ref[0])
    def _():
        x = x_ref[...]
        g = jnp.dot(x, wg_ref[0], preferred_element_type=F32)
        u = jnp.dot(x, wu_ref[0], preferred_element_type=F32)
        h = (g * jax.nn.sigmoid(g)) * u * w_ref[...]
        o_ref[...] = jnp.dot(h.astype(BF16), wd_ref[0], preferred_element_type=F32)

    @pl.when(i >= n_used_ref[0])
    def _():
        o_ref[...] = jnp.zeros_like(o_ref)


def moe_experts(tile_expert, n_used, xs, row_w, wg, wu, wd):
    p_rows, dm = xs.shape
    n_tiles = p_rows // MOE_TILE
    grid_spec = pltpu.PrefetchScalarGridSpec(
        num_scalar_prefetch=2,
        grid=(n_tiles,),
        in_specs=[
            pl.BlockSpec((MOE_TILE, dm), lambda i, te, nu: (i, 0)),
            pl.BlockSpec((MOE_TILE, 1), lambda i, te, nu: (i, 0)),
            pl.BlockSpec((1, dm, D_EXPERT), lambda i, te, nu: (te[i], 0, 0)),
            pl.BlockSpec((1, dm, D_EXPERT), lambda i, te, nu: (te[i], 0, 0)),
            pl.BlockSpec((1, D_EXPERT, dm), lambda i, te, nu: (te[i], 0, 0)),
        ],
        out_specs=pl.BlockSpec((MOE_TILE, dm), lambda i, te, nu: (i, 0)),
    )
    return pl.pallas_call(
        _moe_kernel,
        grid_spec=grid_spec,
        out_shape=jax.ShapeDtypeStruct((p_rows, dm), F32),
        compiler_params=pltpu.CompilerParams(dimension_semantics=("arbitrary",), vmem_limit_bytes=VMEM_LIMIT),
        name="moe_experts",
    )(tile_expert, n_used, xs, row_w, wg, wu, wd)


def hier_moe(t, p):
    n_tok = t.shape[0]
    logits = matmul_f32(t, p['moe_w_router'], name="moe_router")
    group_logits = logits[:, :N_GROUPS] + p['moe_b_group']
    group = jnp.argmax(group_logits, axis=-1)
    group_w = jnp.max(jax.nn.softmax(group_logits, axis=-1), axis=-1, keepdims=True)
    exp_logits = (logits[:, N_GROUPS:N_GROUPS + N_EXPERTS] + p['moe_b_expert']).reshape(n_tok, N_GROUPS, EXPERTS_PER_GROUP)
    exp_logits = jnp.take_along_axis(exp_logits, group[:, None, None], axis=1)[:, 0]
    top_logits, top_idx = lax.top_k(exp_logits, TOP_K)
    top_w = jax.nn.softmax(top_logits, axis=-1) * group_w
    expert_id = (group[:, None] * EXPERTS_PER_GROUP + top_idx).astype(jnp.int32)

    n_rows = n_tok * TOP_K
    flat_e = expert_id.reshape(n_rows)
    flat_w = top_w.reshape(n_rows)
    flat_t = jnp.arange(n_rows, dtype=jnp.int32) // TOP_K
    order = jnp.argsort(flat_e, stable=True)
    sorted_e = flat_e[order]
    counts = jnp.zeros((N_EXPERTS,), jnp.int32).at[flat_e].add(1)
    padded = ((counts + MOE_TILE - 1) // MOE_TILE) * MOE_TILE
    pad_end = jnp.cumsum(padded)
    pad_start = pad_end - padded
    start = jnp.cumsum(counts) - counts
    dest = pad_start[sorted_e] + (jnp.arange(n_rows, dtype=jnp.int32) - start[sorted_e])
    p_rows = n_rows + N_EXPERTS * MOE_TILE
    n_tiles = p_rows // MOE_TILE
    row_token = jnp.zeros((p_rows,), jnp.int32).at[dest].set(flat_t[order])
    row_w = jnp.zeros((p_rows,), F32).at[dest].set(flat_w[order])
    pos = jnp.zeros((n_rows,), jnp.int32).at[order].set(dest).reshape(n_tok, TOP_K)
    n_used = (pad_end[-1] // MOE_TILE).astype(jnp.int32)
    tile_idx = jnp.minimum(jnp.arange(n_tiles, dtype=jnp.int32), n_used - 1)
    tile_expert = jnp.minimum(jnp.searchsorted(pad_end, tile_idx * MOE_TILE, side='right'), N_EXPERTS - 1).astype(jnp.int32)

    xs = t.astype(BF16)[row_token]
    ys = moe_experts(tile_expert, n_used.reshape(1), xs, row_w[:, None], p['moe_w_gate'], p['moe_w_up'], p['moe_w_down'])
    return ys[pos[:, 0]] + ys[pos[:, 1]]


def _flip(t, rev):
    return jnp.flip(t, axis=1) if rev else t


def layer_norm(x, g, b):
    mu = jnp.mean(x, axis=-1, keepdims=True)
    var = jnp.mean(jnp.square(x - mu), axis=-1, keepdims=True)
    return (x - mu) * lax.rsqrt(var + LN_EPS) * g + b


def rms_norm(x, g):
    return x * lax.rsqrt(jnp.mean(jnp.square(x), axis=-1, keepdims=True) + RMS_EPS) * g


def l2_normalize(t):
    return t * lax.rsqrt(jnp.sum(t * t, axis=-1, keepdims=True) + 1e-6)


def depthwise_conv(u, w):
    k = w.shape[0]
    return lax.conv_general_dilated(u, w[:, None, :], window_strides=(1,), padding=[(k // 2, k // 2)], dimension_numbers=('NWC', 'WIO', 'NWC'), feature_group_count=u.shape[-1])


def modulate(t, shift, scale):
    return t * (1 + scale) + shift


def mm3(t, w, name):
    bsz, n, k = t.shape
    return matmul(t.reshape(bsz * n, k), w, name=name).reshape(bsz, n, w.shape[1])


def s5_discretize(lam_re, lam_im, log_step, b_re, b_im):
    step = jnp.exp(log_step)[:, None]
    mag = jnp.exp(lam_re * step)
    a_re = mag * jnp.cos(lam_im * step)
    a_im = mag * jnp.sin(lam_im * step)
    den = lam_re * lam_re + lam_im * lam_im
    f_re = ((a_re - 1.0) * lam_re + a_im * lam_im) / den
    f_im = (a_im * lam_re - (a_re - 1.0) * lam_im) / den
    bb_re = f_re[..., None] * b_re - f_im[..., None] * b_im
    bb_im = f_re[..., None] * b_im + f_im[..., None] * b_re
    return a_re, a_im, bb_re, bb_im


def _cmul(x, y):
    return x[0] * y[0] - x[1] * y[1], x[0] * y[1] + x[1] * y[0]


def s5_tables(p):
    hp = lax.Precision.HIGHEST
    t_len, grp, st = S5_T, S5_GROUPS, S5_STATE
    taps, wst, rd, pq = [], [], [], []
    for d in range(2):
        a_re, a_im, bb_re, bb_im = s5_discretize(p['s5_lam_re'][d], p['s5_lam_im'][d], p['s5_log_step'][d], p['s5_b_re'][d], p['s5_b_im'][d])
        c_re, c_im = p['s5_c_re'][d], p['s5_c_im'][d]
        pr, pi = lax.associative_scan(_cmul, (jnp.broadcast_to(a_re, (t_len, grp, st)), jnp.broadcast_to(a_im, (t_len, grp, st))), axis=0)
        pw_re = jnp.concatenate([jnp.ones((1, grp, st), F32), pr], axis=0)
        pw_im = jnp.concatenate([jnp.zeros((1, grp, st), F32), pi], axis=0)
        ca_re = c_re[None] * pw_re[:, :, None, :] - c_im[None] * pw_im[:, :, None, :]
        ca_im = c_re[None] * pw_im[:, :, None, :] + c_im[None] * pw_re[:, :, None, :]
        taps.append(jnp.einsum('tgip,gpj->tgij', ca_re[:t_len], bb_re, precision=hp) - jnp.einsum('tgip,gpj->tgij', ca_im[:t_len], bb_im, precision=hp))
        e_st = jnp.arange(t_len - 1, -1, -1) if d == 0 else jnp.arange(t_len)
        w_re = pw_re[e_st][..., None] * bb_re[None] - pw_im[e_st][..., None] * bb_im[None]
        w_im = pw_re[e_st][..., None] * bb_im[None] + pw_im[e_st][..., None] * bb_re[None]
        to_rows = lambda w: jnp.transpose(w, (1, 0, 3, 2)).reshape(grp, t_len * S5_GROUP, st)
        wst += [to_rows(w_re), to_rows(w_im)]
        e_rd = jnp.arange(1, t_len + 1) if d == 0 else jnp.arange(t_len, 0, -1)
        to_cols = lambda r: jnp.transpose(r, (1, 3, 0, 2)).reshape(grp, st, t_len * S5_GROUP)
        rd += [to_cols(ca_re[e_rd]), to_cols(-ca_im[e_rd])]
        lv_re, lv_im = pw_re[t_len], pw_im[t_len]
        for _ in range(8):
            pq += [jnp.concatenate([lv_re, lv_re], axis=-1), jnp.concatenate([-lv_im, lv_im], axis=-1)]
            lv_re, lv_im = _cmul((lv_re, lv_im), (lv_re, lv_im))
    kf, kb = taps
    blocks = jnp.concatenate([kb[1:][::-1], (kf[0] + kb[0])[None], kf[1:], jnp.zeros_like(kf[:1])], axis=0)
    ext = jnp.transpose(blocks, (1, 3, 0, 2)).reshape(grp, S5_GROUP, 2 * t_len * S5_GROUP)
    return ext, jnp.concatenate(wst, axis=-1).astype(BF16), jnp.concatenate(rd, axis=1).astype(BF16), jnp.stack(pq, axis=1)


def _s5_kernel(u_ref, ext_ref, wst_ref, rd_ref, pq_ref, h0_ref, y_ref, hfin_ref, m_ref, *, n_chunks, bsz):
    cw = S5_T * S5_GROUP
    st2 = 2 * S5_STATE
    ext = ext_ref[0]
    for s in range(S5_T):
        off = S5_GROUP * (S5_T - 1 - s)
        shifted = ext if off == 0 else pltpu.roll(ext, shift=2 * cw - off, axis=1)
        m_ref[S5_GROUP * s:S5_GROUP * (s + 1), :] = shifted[:, :cw].astype(BF16)
    u = u_ref[0]
    rows = u.shape[0]
    y = jnp.dot(u, m_ref[...], preferred_element_type=F32)
    s_all = jnp.dot(u, wst_ref[0], preferred_element_type=F32)
    sf, sb = s_all[:, :st2], s_all[:, st2:]
    row = lax.broadcasted_iota(jnp.int32, (rows, st2), 0)
    c_idx = row % n_chunks

    def cmul_rows(idx, x):
        return pq_ref[0, idx:idx + 1, :] * x + pq_ref[0, idx + 1:idx + 2, :] * pltpu.roll(x, shift=S5_STATE, axis=1)

    h0 = h0_ref[0]
    h0f = jnp.zeros((rows, st2), F32)
    h0b = jnp.zeros((rows, st2), F32)
    for b in range(bsz):
        in_b = (row >= b * n_chunks) & (row < (b + 1) * n_chunks)
        h0f = jnp.where(in_b, h0[b:b + 1, :st2], h0f)
        h0b = jnp.where(in_b, h0[b:b + 1, st2:], h0b)
    hf = jnp.where(c_idx == 0, h0f, pltpu.roll(sf, shift=1, axis=0))
    hb = jnp.where(c_idx == n_chunks - 1, h0b, pltpu.roll(sb, shift=rows - 1, axis=0))
    level, sh = 0, 1
    while sh < n_chunks:
        dn = jnp.where(c_idx >= sh, pltpu.roll(hf, shift=sh, axis=0), 0.0)
        up = jnp.where(c_idx < n_chunks - sh, pltpu.roll(hb, shift=rows - sh, axis=0), 0.0)
        hf = hf + cmul_rows(2 * level, dn)
        hb = hb + cmul_rows(16 + 2 * level, up)
        level, sh = level + 1, sh * 2
    h_in = jnp.concatenate([hf, hb], axis=1)
    y_ref[0] = y + jnp.dot(h_in.astype(BF16), rd_ref[0], preferred_element_type=F32)
    hfin_ref[0] = jnp.concatenate([cmul_rows(0, hf) + sf, cmul_rows(16, hb) + sb], axis=1)


def s5_scan(u, tables, h0, n_chunks, bsz):
    ext, wst, rd, pq = tables
    grp, rows, cw = u.shape
    st4 = 4 * S5_STATE
    blk = lambda *shape: pl.BlockSpec((1,) + shape, lambda i: (i, 0, 0))
    return pl.pallas_call(
        functools.partial(_s5_kernel, n_chunks=n_chunks, bsz=bsz),
        grid=(grp,),
        in_specs=[blk(rows, cw), blk(S5_GROUP, 2 * cw), blk(cw, st4), blk(st4, cw), blk(32, 2 * S5_STATE), blk(bsz, st4)],
        out_specs=[blk(rows, cw), blk(rows, st4)],
        out_shape=[jax.ShapeDtypeStruct((grp, rows, cw), F32), jax.ShapeDtypeStruct((grp, rows, st4), F32)],
        scratch_shapes=[pltpu.VMEM((cw, cw), BF16)],
        compiler_params=pltpu.CompilerParams(dimension_semantics=("parallel",), vmem_limit_bytes=VMEM_LIMIT),
        name="s5_scan",
    )(u, ext, wst, rd, pq, h0)


def s5_sequence(u, tables, h0):
    bsz, n, _ = u.shape
    n_chunks = n // S5_T
    rows = bsz * n_chunks
    rows_pad = -(-rows // 16) * 16
    ug = u.astype(BF16).reshape(bsz, n_chunks, S5_T, S5_GROUPS, S5_GROUP)
    ug = jnp.transpose(ug, (3, 0, 1, 2, 4)).reshape(S5_GROUPS, rows, S5_T * S5_GROUP)
    if rows_pad != rows:
        ug = jnp.pad(ug, ((0, 0), (0, rows_pad - rows), (0, 0)))
    y, hfin = s5_scan(ug, tables, h0, n_chunks, bsz)
    y = y[:, :rows].reshape(S5_GROUPS, bsz, n_chunks, S5_T, S5_GROUP)
    y = jnp.transpose(y, (1, 2, 3, 0, 4)).reshape(bsz, n, S5_WIDTH)
    hfin = hfin[:, :rows].reshape(S5_GROUPS, bsz, n_chunks, 4 * S5_STATE)
    st2 = 2 * S5_STATE
    h_end = jnp.concatenate([hfin[:, :, n_chunks - 1, :st2], hfin[:, :, 0, st2:]], axis=-1)
    return y, h_end


def s5_glu(y, p):
    y = jax.nn.gelu(y)
    y = y * jax.nn.sigmoid(mm3(y, p['s5_glu_w'], "s5_glu") + p['s5_glu_b'])
    return rms_norm(y, p['s5_norm_g'])


def s5_mixer(u, uc, p, need_ctx_out):
    bsz = u.shape[0]
    tables = s5_tables(p)
    yc, h_ctx = s5_sequence(uc, tables, jnp.zeros((S5_GROUPS, bsz, 4 * S5_STATE), F32))
    y, _ = s5_sequence(u, tables, h_ctx)
    out = s5_glu(u * p['s5_d'] + y, p)
    out_c = s5_glu(uc * p['s5_d'] + yc, p) if need_ctx_out else None
    return out, out_c


def dn_features(z_qkv, conv_w):
    bsz, seq_len, _ = z_qkv.shape
    q, k, v = jnp.split(jax.nn.silu(depthwise_conv(z_qkv, conv_w)), 3, axis=-1)
    shp = (bsz, seq_len, DN_HEADS, DN_HEAD_DIM)
    return l2_normalize(q.reshape(shp)) * DN_HEAD_DIM ** -0.5, l2_normalize(k.reshape(shp)), v.reshape(shp)


def dn_gates(z_a, z_b, a_log, dt_bias):
    bsz, seq_len, _ = z_a.shape
    shp = (bsz, seq_len, 2, DN_HEADS)
    g = -jnp.exp(a_log) * jax.nn.softplus(z_a.reshape(shp) + dt_bias)
    beta = jax.nn.sigmoid(z_b.reshape(shp))
    return g, beta


def gated_delta_rule(q, k, v, g, beta, s0):
    bsz, seq_len, heads, _ = k.shape
    dv = v.shape[-1]
    n_chunks = seq_len // DN_CHUNK

    def chunks(t):
        return jnp.moveaxis(t.reshape((bsz, n_chunks, DN_CHUNK, heads) + t.shape[3:]), 3, 1)

    q, k, v, beta = chunks(q), chunks(k), chunks(v), chunks(beta)
    g = jnp.cumsum(chunks(g), axis=-1)
    idx = jnp.arange(DN_CHUNK)
    incl = idx[:, None] >= idx[None, :]
    strict = idx[:, None] > idx[None, :]
    decay = jnp.where(incl, jnp.exp(jnp.where(incl, g[..., :, None] - g[..., None, :], 0.0)), 0.0)
    k_beta = k * beta[..., None]
    hp = lax.Precision.HIGHEST
    lower = jnp.where(strict, jnp.einsum('bhnid,bhnjd->bhnij', k_beta, k, precision=hp) * decay, 0.0)
    rhs = jnp.concatenate([v * beta[..., None], k_beta * jnp.exp(g)[..., None]], axis=-1)
    sol = lax.linalg.triangular_solve(lower + jnp.eye(DN_CHUNK, dtype=F32), rhs, left_side=True, lower=True, unit_diagonal=True)
    u, w = sol[..., :dv], sol[..., dv:]
    qk = jnp.einsum('bhnid,bhnjd->bhnij', q, k, precision=hp) * decay
    q_dec = q * jnp.exp(g)[..., None]
    k_tail = k * jnp.exp(g[..., -1:] - g)[..., None]
    g_last = jnp.exp(g[..., -1])[..., None, None]

    def step(state, xs):
        qd_i, kt_i, u_i, w_i, qk_i, gl_i = xs
        v_new = u_i - jnp.einsum('bhck,bhkv->bhcv', w_i, state, precision=hp)
        o_i = jnp.einsum('bhck,bhkv->bhcv', qd_i, state, precision=hp) + jnp.einsum('bhij,bhjv->bhiv', qk_i, v_new, precision=hp)
        state = state * gl_i + jnp.einsum('bhck,bhcv->bhkv', kt_i, v_new, precision=hp)
        return state, o_i

    xs = tuple(jnp.moveaxis(t, 2, 0) for t in (q_dec, k_tail, u, w, qk, g_last))
    s_final, o = lax.scan(step, s0, xs)
    o = jnp.transpose(o, (1, 0, 3, 2, 4)).reshape(bsz, seq_len, heads, dv)
    return o, s_final


def dn_output(o, z_gate, p):
    bsz, seq_len = o.shape[:2]
    gate = jax.nn.silu(z_gate).reshape(bsz, seq_len, DN_HEADS, DN_HEAD_DIM)
    return (rms_norm(o, p['dn_norm_g']) * gate).reshape(bsz, seq_len, DN_WIDTH)


def deltanet_mixer(z, zc, p, need_ctx_out):
    q, k, v = dn_features(z['qkv'], p['dn_conv_w'])
    g, beta = dn_gates(z['a'], z['b'], p['dn_a_log'], p['dn_dt_bias'])
    qc, kc, vc = dn_features(zc['qkv'], p['dn_conv_w'])
    gc, betac = dn_gates(zc['a'], zc['b'], p['dn_a_log'], p['dn_dt_bias'])
    s0 = jnp.zeros((q.shape[0], DN_HEADS, DN_HEAD_DIM, DN_HEAD_DIM), F32)
    o, oc = 0.0, 0.0
    for d in range(2):
        rev = d == 1
        oc_d, s_ctx = gated_delta_rule(*(_flip(t, rev) for t in (qc, kc, vc, gc[:, :, d], betac[:, :, d])), s0)
        o_d, _ = gated_delta_rule(*(_flip(t, rev) for t in (q, k, v, g[:, :, d], beta[:, :, d])), s_ctx)
        o = o + _flip(o_d, rev)
        if need_ctx_out:
            oc = oc + _flip(oc_d, rev)
    y = dn_output(o, z['g'], p)
    yc = dn_output(oc, zc['g'], p) if need_ctx_out else None
    return y, yc


def hyena_filters(seq_len, p):
    hp = lax.Precision.HIGHEST
    t = jnp.linspace(0.0, 1.0, seq_len, dtype=F32)[:, None]
    w = (2.0 * math.pi / seq_len) * jnp.arange(seq_len, dtype=F32)[:, None]
    f = jnp.linspace(1e-4, HY_BANDS - 1, HY_BANDS, dtype=F32)[None, :]
    feats = jnp.concatenate([t, jnp.cos(w * f), -jnp.sin(w * f)], axis=-1)
    freq = p['hy_f_freq']
    h = jnp.sin(freq * (jnp.dot(feats, p['hy_f_w1'], precision=hp) + p['hy_f_b1']))
    h = jnp.sin(freq * (jnp.dot(h, p['hy_f_w2'], precision=hp) + p['hy_f_b2']))
    h = jnp.sin(freq * (jnp.dot(h, p['hy_f_w3'], precision=hp) + p['hy_f_b3']))
    h = jnp.dot(h, p['hy_f_w4'], precision=hp).reshape(seq_len, 2, HY_WIDTH)
    rates = jnp.linspace(math.log(HY_DECAY_TARGET) / HY_DECAY_LONG_PCT, math.log(HY_DECAY_TARGET) / HY_DECAY_SHORT_PCT, HY_WIDTH, dtype=F32)
    h = h * jnp.exp(-t * jnp.abs(rates))[:, None, :]
    return h[:, 0], h[:, 1]


def two_sided_long_conv(u, h_fwd, h_bwd):
    seq_len = u.shape[1]
    taps = jnp.concatenate([h_fwd[:1] + h_bwd[:1], h_fwd[1:], jnp.zeros_like(h_fwd[:1]), h_bwd[:0:-1]], axis=0)
    u_f = jnp.fft.rfft(u, n=2 * seq_len, axis=1)
    t_f = jnp.fft.rfft(taps, axis=0)
    return jnp.fft.irfft(u_f * t_f[None], n=2 * seq_len, axis=1)[:, :seq_len]


def hyena_mixer(zh, p):
    zs = depthwise_conv(zh, p['hy_conv_w']) + p['hy_conv_b']
    x0, x1, v = jnp.split(zs, 3, axis=-1)
    z = x1 * v
    h_fwd, h_bwd = hyena_filters(zh.shape[1], p)
    y = x0 * (two_sided_long_conv(z, h_fwd, h_bwd) + p['hy_bias'] * z)
    return rms_norm(y, p['hy_norm_g'])


def in_projection(h, p, full):
    bsz, n, k = h.shape
    hb = h.reshape(bsz * n, k).astype(BF16)
    out = {}
    out['s5'] = matmul(hb, p['w_in_s5'], name="in_s5").reshape(bsz, n, -1)
    out['qkv'] = matmul(hb, p['w_in_qkv'], name="in_qkv").reshape(bsz, n, -1)
    ab = matmul(hb, p['w_in_ab'], name="in_ab").reshape(bsz, n, -1)
    out['a'] = ab[..., :2 * DN_HEADS]
    out['b'] = ab[..., 2 * DN_HEADS:4 * DN_HEADS]
    if full:
        out['g'] = matmul(hb, p['w_in_g'], name="in_g").reshape(bsz, n, -1)
        out['hy'] = matmul(hb, p['w_in_hy'], name="in_hy").reshape(bsz, n, -1)
    return out


def token_mixer(h, hc, p, need_ctx_out):
    z = in_projection(h, p, True)
    zc = in_projection(hc, p, need_ctx_out)
    s5_y, s5_yc = s5_mixer(z['s5'], zc['s5'], p, need_ctx_out)
    dn_y, dn_yc = deltanet_mixer(z, zc, p, need_ctx_out)
    y = mm3(jnp.concatenate([s5_y, dn_y, hyena_mixer(z['hy'], p)], axis=-1).astype(BF16), p['w_out'], "out_proj")
    if not need_ctx_out:
        return y, None
    yc = mm3(jnp.concatenate([s5_yc, dn_yc, hyena_mixer(zc['hy'], p)], axis=-1).astype(BF16), p['w_out'], "out_proj_ctx")
    return y, yc


def ada_modulation(c_rows, w_ada, b_ada):
    rows = c_rows.shape[0]
    a = jnp.zeros((8, c_rows.shape[1]), F32).at[:rows].set(jax.nn.silu(c_rows))
    return matmul(a, w_ada, name="ada")[:rows] + b_ada


def trunk_layer(x, xc, c, c_ctx, p, last):
    bsz, n, dm = x.shape
    lc = xc.shape[1]
    ada = ada_modulation(jnp.concatenate([c, c_ctx[None]], axis=0), p['w_ada'], p['b_ada'])
    sh1, sc1, g1, sh2, sc2, g2 = jnp.split(ada[:bsz, None, :], 6, axis=-1)
    mc = jnp.split(ada[bsz], 6, axis=-1)
    y, yc = token_mixer(modulate(x, sh1, sc1), modulate(xc, mc[0], mc[1]), p, not last)
    x = layer_norm(DEEPNORM_ALPHA * x + g1 * y, p['ln1_g'], p['ln1_b'])
    if last:
        f = hier_moe(modulate(x, sh2, sc2).reshape(bsz * n, dm), p).reshape(bsz, n, dm)
        return layer_norm(DEEPNORM_ALPHA * x + g2 * f, p['ln2_g'], p['ln2_b']), None
    xc = layer_norm(DEEPNORM_ALPHA * xc + mc[2] * yc, p['ln1_g'], p['ln1_b'])
    tokens = jnp.concatenate([modulate(x, sh2, sc2).reshape(bsz * n, dm), modulate(xc, mc[3], mc[4]).reshape(bsz * lc, dm)], axis=0)
    f = hier_moe(tokens, p)
    x = layer_norm(DEEPNORM_ALPHA * x + g2 * f[:bsz * n].reshape(bsz, n, dm), p['ln2_g'], p['ln2_b'])
    xc = layer_norm(DEEPNORM_ALPHA * xc + mc[5] * f[bsz * n:].reshape(bsz, lc, dm), p['ln2_g'], p['ln2_b'])
    return x, xc


def kernel(x, c, ctx, c_ctx, w_ada, b_ada, w_in, s5_lam_re, s5_lam_im, s5_log_step, s5_b_re, s5_b_im, s5_c_re, s5_c_im, s5_d, s5_glu_w, s5_glu_b, s5_norm_g, dn_conv_w, dn_a_log, dn_dt_bias, dn_norm_g, hy_conv_w, hy_conv_b, hy_f_w1, hy_f_b1, hy_f_w2, hy_f_b2, hy_f_w3, hy_f_b3, hy_f_freq, hy_f_w4, hy_bias, hy_norm_g, w_out, ln1_g, ln1_b, ln2_g, ln2_b, moe_w_group, moe_b_group, moe_w_expert, moe_b_expert, moe_w_gate, moe_w_up, moe_w_down):
    xc = ctx
    for l in range(DEPTH):
        w_in_l = w_in[l]
        w_ab = jnp.zeros((D_MODEL, LANE), F32).at[:, :4 * DN_HEADS].set(w_in_l[:, COL_DN_A:STATE_COLS])
        w_router = jnp.zeros((D_MODEL, LANE), F32).at[:, :N_GROUPS].set(moe_w_group[l]).at[:, N_GROUPS:N_GROUPS + N_EXPERTS].set(moe_w_expert[l])
        p = {
            'w_ada': w_ada[l], 'b_ada': b_ada[l],
            'w_in_s5': w_in_l[:, COL_S5:COL_DN_QKV].astype(BF16),
            'w_in_qkv': w_in_l[:, COL_DN_QKV:COL_DN_A].astype(BF16),
            'w_in_ab': w_ab.astype(BF16),
            'w_in_g': w_in_l[:, COL_DN_G:COL_HY].astype(BF16),
            'w_in_hy': w_in_l[:, COL_HY:].astype(BF16),
            's5_lam_re': s5_lam_re[l], 's5_lam_im': s5_lam_im[l], 's5_log_step': s5_log_step[l],
            's5_b_re': s5_b_re[l], 's5_b_im': s5_b_im[l], 's5_c_re': s5_c_re[l], 's5_c_im': s5_c_im[l],
            's5_d': s5_d[l], 's5_glu_w': s5_glu_w[l].astype(BF16), 's5_glu_b': s5_glu_b[l], 's5_norm_g': s5_norm_g[l],
            'dn_conv_w': dn_conv_w[l], 'dn_a_log': dn_a_log[l], 'dn_dt_bias': dn_dt_bias[l], 'dn_norm_g': dn_norm_g[l],
            'hy_conv_w': hy_conv_w[l], 'hy_conv_b': hy_conv_b[l],
            'hy_f_w1': hy_f_w1[l], 'hy_f_b1': hy_f_b1[l], 'hy_f_w2': hy_f_w2[l], 'hy_f_b2': hy_f_b2[l],
            'hy_f_w3': hy_f_w3[l], 'hy_f_b3': hy_f_b3[l], 'hy_f_freq': hy_f_freq[l], 'hy_f_w4': hy_f_w4[l],
            'hy_bias': hy_bias[l], 'hy_norm_g': hy_norm_g[l], 'w_out': w_out[l].astype(BF16),
            'ln1_g': ln1_g[l], 'ln1_b': ln1_b[l], 'ln2_g': ln2_g[l], 'ln2_b': ln2_b[l],
            'moe_w_router': w_router, 'moe_b_group': moe_b_group[l], 'moe_b_expert': moe_b_expert[l],
            'moe_w_gate': moe_w_gate[l].astype(BF16), 'moe_w_up': moe_w_up[l].astype(BF16), 'moe_w_down': moe_w_down[l].astype(BF16),
        }
        x, xc = trunk_layer(x, xc, c, c_ctx, p, l == DEPTH - 1)
    return x
```

```python
import functools
import math

import jax
import jax.numpy as jnp
from jax import lax
from jax.experimental import pallas as pl
from jax.experimental.pallas import tpu as pltpu

D_MODEL = 4096
DEPTH = 2
GRID_W = 64

S5_WIDTH = D_MODEL // 4
S5_GROUP = 16
S5_GROUPS = S5_WIDTH // S5_GROUP
S5_STATE = 64

DN_WIDTH = D_MODEL // 2
DN_HEAD_DIM = 128
DN_HEADS = DN_WIDTH // DN_HEAD_DIM
DN_CHUNK = 64
SHORT_CONV = 3

HY_WIDTH = D_MODEL - S5_WIDTH - DN_WIDTH
HY_BANDS = 16
HY_EMB = 1 + 2 * HY_BANDS
HY_HIDDEN = 64
HY_DECAY_SHORT_PCT = 0.3
HY_DECAY_LONG_PCT = 1.5
HY_DECAY_TARGET = 1e-2

MIX_WIDTH = S5_WIDTH + DN_WIDTH + HY_WIDTH

COL_S5 = 0
COL_DN_QKV = COL_S5 + S5_WIDTH
COL_DN_A = COL_DN_QKV + 3 * DN_WIDTH
COL_DN_B = COL_DN_A + 2 * DN_HEADS
STATE_COLS = COL_DN_B + 2 * DN_HEADS
COL_DN_G = STATE_COLS
COL_HY = COL_DN_G + DN_WIDTH
IN_WIDTH = COL_HY + 3 * HY_WIDTH

N_GROUPS = 4
EXPERTS_PER_GROUP = 8
N_EXPERTS = N_GROUPS * EXPERTS_PER_GROUP
TOP_K = 2
D_EXPERT = 512

DEEPNORM_ALPHA = (2 * DEPTH) ** 0.25
LN_EPS = 1e-5
RMS_EPS = 1e-6
F32 = jnp.float32
BF16 = jnp.bfloat16

V7X_VMEM_BYTES = 64 * 1024 * 1024
VMEM_LIMIT = 52 * 1024 * 1024
LANE = 128

MOE_TILE = 256
S5_T = 64


def _mm_kernel(a_ref, b_ref, o_ref):
    o_ref[...] = jnp.dot(a_ref[...].astype(BF16), b_ref[...].astype(BF16), preferred_element_type=F32).astype(o_ref.dtype)


def _pick(n, prefs):
    for p in prefs:
        if n % p == 0:
            return p
    return n


def matmul(a, b, out_dtype=F32, name="matmul"):
    m, k = a.shape
    n = b.shape[1]
    tm = _pick(m, (512, 256, 128, 64, 32, 16, 8))
    tn = _pick(n, (1024, 512, 256, 128))
    return pl.pallas_call(
        _mm_kernel,
        grid=(m // tm, n // tn),
        in_specs=[pl.BlockSpec((tm, k), lambda i, j: (i, 0)), pl.BlockSpec((k, tn), lambda i, j: (0, j))],
        out_specs=pl.BlockSpec((tm, tn), lambda i, j: (i, j)),
        out_shape=jax.ShapeDtypeStruct((m, n), out_dtype),
        compiler_params=pltpu.CompilerParams(dimension_semantics=("parallel", "parallel"), vmem_limit_bytes=VMEM_LIMIT),
        name=name,
    )(a, b)


def _mm_f32_kernel(a_ref, b_ref, o_ref):
    o_ref[...] = jnp.dot(a_ref[...], b_ref[...], preferred_element_type=F32, precision=lax.Precision.HIGHEST)


def matmul_f32(a, b, name="matmul_f32"):
    m, k = a.shape
    n = b.shape[1]
    tm = _pick(m, (512, 256, 128, 64, 32, 16, 8))
    return pl.pallas_call(
        _mm_f32_kernel,
        grid=(m // tm,),
        in_specs=[pl.BlockSpec((tm, k), lambda i: (i, 0)), pl.BlockSpec((k, n), lambda i: (0, 0))],
        out_specs=pl.BlockSpec((tm, n), lambda i: (i, 0)),
        out_shape=jax.ShapeDtypeStruct((m, n), F32),
        compiler_params=pltpu.CompilerParams(dimension_semantics=("parallel",), vmem_limit_bytes=VMEM_LIMIT),
        name=name,
    )(a, b)


def _moe_kernel(tile_expert_ref, n_used_ref, x_ref, w_ref, wg_ref, wu_ref, wd_ref, o_ref):
    i = pl.program_id(0)

    @pl.when(i < n_used_ref[0])
    def _():
        x = x_ref[...]
        g = jnp.dot(x, wg_ref[0], preferred_element_type=F32)
        u = jnp.dot(x, wu_ref[0], preferred_element_type=F32)
        h = (g * jax.nn.sigmoid(g)) * u * w_ref[...]
        o_ref[...] = jnp.dot(h.astype(BF16), wd_ref[0], preferred_element_type=F32)

    @pl.when(i >= n_used_ref[0])
    def _():
        o_ref[...] = jnp.zeros_like(o_ref)


def moe_experts(tile_expert, n_used, xs, row_w, wg, wu, wd):
    p_rows, dm = xs.shape
    n_tiles = p_rows // MOE_TILE
    grid_spec = pltpu.PrefetchScalarGridSpec(
        num_scalar_prefetch=2,
        grid=(n_tiles,),
        in_specs=[
            pl.BlockSpec((MOE_TILE, dm), lambda i, te, nu: (i, 0)),
            pl.BlockSpec((MOE_TILE, 1), lambda i, te, nu: (i, 0)),
            pl.BlockSpec((1, dm, D_EXPERT), lambda i, te, nu: (te[i], 0, 0)),
            pl.BlockSpec((1, dm, D_EXPERT), lambda i, te, nu: (te[i], 0, 0)),
            pl.BlockSpec((1, D_EXPERT, dm), lambda i, te, nu: (te[i], 0, 0)),
        ],
        out_specs=pl.BlockSpec((MOE_TILE, dm), lambda i, te, nu: (i, 0)),
    )
    return pl.pallas_call(
        _moe_kernel,
        grid_spec=grid_spec,
        out_shape=jax.ShapeDtypeStruct((p_rows, dm), F32),
        compiler_params=pltpu.CompilerParams(dimension_semantics=("arbitrary",), vmem_limit_bytes=VMEM_LIMIT),
        name="moe_experts",
    )(tile_expert, n_used, xs, row_w, wg, wu, wd)


def hier_moe(t, p):
    n_tok = t.shape[0]
    logits = matmul_f32(t, p['moe_w_router'], name="moe_router")
    group_logits = logits[:, :N_GROUPS] + p['moe_b_group']
    group = jnp.argmax(group_logits, axis=-1)
    group_w = jnp.max(jax.nn.softmax(group_logits, axis=-1), axis=-1, keepdims=True)
    exp_logits = (logits[:, N_GROUPS:N_GROUPS + N_EXPERTS] + p['moe_b_expert']).reshape(n_tok, N_GROUPS, EXPERTS_PER_GROUP)
    exp_logits = jnp.take_along_axis(exp_logits, group[:, None, None], axis=1)[:, 0]
    top_logits, top_idx = lax.top_k(exp_logits, TOP_K)
    top_w = jax.nn.softmax(top_logits, axis=-1) * group_w
    expert_id = (group[:, None] * EXPERTS_PER_GROUP + top_idx).astype(jnp.int32)

    n_rows = n_tok * TOP_K
    flat_e = expert_id.reshape(n_rows)
    flat_w = top_w.reshape(n_rows)
    flat_t = jnp.arange(n_rows, dtype=jnp.int32) // TOP_K
    order = jnp.argsort(flat_e, stable=True)
    sorted_e = flat_e[order]
    counts = jnp.zeros((N_EXPERTS,), jnp.int32).at[flat_e].add(1)
    padded = ((counts + MOE_TILE - 1) // MOE_TILE) * MOE_TILE
    pad_end = jnp.cumsum(padded)
    pad_start = pad_end - padded
    start = jnp.cumsum(counts) - counts
    dest = pad_start[sorted_e] + (jnp.arange(n_rows, dtype=jnp.int32) - start[sorted_e])
    p_rows = n_rows + N_EXPERTS * MOE_TILE
    n_tiles = p_rows // MOE_TILE
    row_token = jnp.zeros((p_rows,), jnp.int32).at[dest].set(flat_t[order])
    row_w = jnp.zeros((p_rows,), F32).at[dest].set(flat_w[order])
    pos = jnp.zeros((n_rows,), jnp.int32).at[order].set(dest).reshape(n_tok, TOP_K)
    n_used = (pad_end[-1] // MOE_TILE).astype(jnp.int32)
    tile_idx = jnp.minimum(jnp.arange(n_tiles, dtype=jnp.int32), n_used - 1)
    tile_expert = jnp.minimum(jnp.searchsorted(pad_end, tile_idx * MOE_TILE, side='right'), N_EXPERTS - 1).astype(jnp.int32)

    xs = t.astype(BF16)[row_token]
    ys = moe_experts(tile_expert, n_used.reshape(1), xs, row_w[:, None], p['moe_w_gate'], p['moe_w_up'], p['moe_w_down'])
    return ys[pos[:, 0]] + ys[pos[:, 1]]


def _flip(t, rev):
    return jnp.flip(t, axis=1) if rev else t


def layer_norm(x, g, b):
    mu = jnp.mean(x, axis=-1, keepdims=True)
    var = jnp.mean(jnp.square(x - mu), axis=-1, keepdims=True)
    return (x - mu) * lax.rsqrt(var + LN_EPS) * g + b


def rms_norm(x, g):
    return x * lax.rsqrt(jnp.mean(jnp.square(x), axis=-1, keepdims=True) + RMS_EPS) * g


def l2_normalize(t):
    return t * lax.rsqrt(jnp.sum(t * t, axis=-1, keepdims=True) + 1e-6)


def depthwise_conv(u, w):
    k = w.shape[0]
    return lax.conv_general_dilated(u, w[:, None, :], window_strides=(1,), padding=[(k // 2, k // 2)], dimension_numbers=('NWC', 'WIO', 'NWC'), feature_group_count=u.shape[-1])


def modulate(t, shift, scale):
    return t * (1 + scale) + shift


def mm3(t, w, name):
    bsz, n, k = t.shape
    return matmul(t.reshape(bsz * n, k), w, name=name).reshape(bsz, n, w.shape[1])


def s5_discretize(lam_re, lam_im, log_step, b_re, b_im):
    step = jnp.exp(log_step)[:, None]
    mag = jnp.exp(lam_re * step)
    a_re = mag * jnp.cos(lam_im * step)
    a_im = mag * jnp.sin(lam_im * step)
    den = lam_re * lam_re + lam_im * lam_im
    f_re = ((a_re - 1.0) * lam_re + a_im * lam_im) / den
    f_im = (a_im * lam_re - (a_re - 1.0) * lam_im) / den
    bb_re = f_re[..., None] * b_re - f_im[..., None] * b_im
    bb_im = f_re[..., None] * b_im + f_im[..., None] * b_re
    return a_re, a_im, bb_re, bb_im


def _cmul(x, y):
    return x[0] * y[0] - x[1] * y[1], x[0] * y[1] + x[1] * y[0]


def s5_tables(p):
    hp = lax.Precision.HIGHEST
    t_len, grp, st = S5_T, S5_GROUPS, S5_STATE
    taps, wst, rd, pq = [], [], [], []
    for d in range(2):
        a_re, a_im, bb_re, bb_im = s5_discretize(p['s5_lam_re'][d], p['s5_lam_im'][d], p['s5_log_step'][d], p['s5_b_re'][d], p['s5_b_im'][d])
        c_re, c_im = p['s5_c_re'][d], p['s5_c_im'][d]
        pr, pi = lax.associative_scan(_cmul, (jnp.broadcast_to(a_re, (t_len, grp, st)), jnp.broadcast_to(a_im, (t_len, grp, st))), axis=0)
        pw_re = jnp.concatenate([jnp.ones((1, grp, st), F32), pr], axis=0)
        pw_im = jnp.concatenate([jnp.zeros((1, grp, st), F32), pi], axis=0)
        ca_re = c_re[None] * pw_re[:, :, None, :] - c_im[None] * pw_im[:, :, None, :]
        ca_im = c_re[None] * pw_im[:, :, None, :] + c_im[None] * pw_re[:, :, None, :]
        taps.append(jnp.einsum('tgip,gpj->tgij', ca_re[:t_len], bb_re, precision=hp) - jnp.einsum('tgip,gpj->tgij', ca_im[:t_len], bb_im, precision=hp))
        e_st = jnp.arange(t_len - 1, -1, -1) if d == 0 else jnp.arange(t_len)
        w_re = pw_re[e_st][..., None] * bb_re[None] - pw_im[e_st][..., None] * bb_im[None]
        w_im = pw_re[e_st][..., None] * bb_im[None] + pw_im[e_st][..., None] * bb_re[None]
        to_rows = lambda w: jnp.transpose(w, (1, 0, 3, 2)).reshape(grp, t_len * S5_GROUP, st)
        wst += [to_rows(w_re), to_rows(w_im)]
        e_rd = jnp.arange(1, t_len + 1) if d == 0 else jnp.arange(t_len, 0, -1)
        to_cols = lambda r: jnp.transpose(r, (1, 3, 0, 2)).reshape(grp, st, t_len * S5_GROUP)
        rd += [to_cols(ca_re[e_rd]), to_cols(-ca_im[e_rd])]
        lv_re, lv_im = pw_re[t_len], pw_im[t_len]
        for _ in range(8):
            pq += [jnp.concatenate([lv_re, lv_re], axis=-1), jnp.concatenate([-lv_im, lv_im], axis=-1)]
            lv_re, lv_im = _cmul((lv_re, lv_im), (lv_re, lv_im))
    kf, kb = taps
    blocks = jnp.concatenate([kb[1:][::-1], (kf[0] + kb[0])[None], kf[1:], jnp.zeros_like(kf[:1])], axis=0)
    ext = jnp.transpose(blocks, (1, 3, 0, 2)).reshape(grp, S5_GROUP, 2 * t_len * S5_GROUP)
    return ext, jnp.concatenate(wst, axis=-1).astype(BF16), jnp.concatenate(rd, axis=1).astype(BF16), jnp.stack(pq, axis=1)


def _s5_kernel(u_ref, ext_ref, wst_ref, rd_ref, pq_ref, h0_ref, y_ref, hfin_ref, m_ref, *, n_chunks, bsz):
    cw = S5_T * S5_GROUP
    st2 = 2 * S5_STATE
    ext = ext_ref[0]
    for s in range(S5_T):
        off = S5_GROUP * (S5_T - 1 - s)
        shifted = ext if off == 0 else pltpu.roll(ext, shift=2 * cw - off, axis=1)
        m_ref[S5_GROUP * s:S5_GROUP * (s + 1), :] = shifted[:, :cw].astype(BF16)
    u = u_ref[0]
    rows = u.shape[0]
    y = jnp.dot(u, m_ref[...], preferred_element_type=F32)
    s_all = jnp.dot(u, wst_ref[0], preferred_element_type=F32)
    sf, sb = s_all[:, :st2], s_all[:, st2:]
    row = lax.broadcasted_iota(jnp.int32, (rows, st2), 0)
    c_idx = row % n_chunks

    def cmul_rows(idx, x):
        return pq_ref[0, idx:idx + 1, :] * x + pq_ref[0, idx + 1:idx + 2, :] * pltpu.roll(x, shift=S5_STATE, axis=1)

    h0 = h0_ref[0]
    h0f = jnp.zeros((rows, st2), F32)
    h0b = jnp.zeros((rows, st2), F32)
    for b in range(bsz):
        in_b = (row >= b * n_chunks) & (row < (b + 1) * n_chunks)
        h0f = jnp.where(in_b, h0[b:b + 1, :st2], h0f)
        h0b = jnp.where(in_b, h0[b:b + 1, st2:], h0b)
    hf = jnp.where(c_idx == 0, h0f, pltpu.roll(sf, shift=1, axis=0))
    hb = jnp.where(c_idx == n_chunks - 1, h0b, pltpu.roll(sb, shift=rows - 1, axis=0))
    level, sh = 0, 1
    while sh < n_chunks:
        dn = jnp.where(c_idx >= sh, pltpu.roll(hf, shift=sh, axis=0), 0.0)
        up = jnp.where(c_idx < n_chunks - sh, pltpu.roll(hb, shift=rows - sh, axis=0), 0.0)
        hf = hf + cmul_rows(2 * level, dn)
        hb = hb + cmul_rows(16 + 2 * level, up)
        level, sh = level + 1, sh * 2
    h_in = jnp.concatenate([hf, hb], axis=1)
    y_ref[0] = y + jnp.dot(h_in.astype(BF16), rd_ref[0], preferred_element_type=F32)
    hfin_ref[0] = jnp.concatenate([cmul_rows(0, hf) + sf, cmul_rows(16, hb) + sb], axis=1)


def s5_scan(u, tables, h0, n_chunks, bsz):
    ext, wst, rd, pq = tables
    grp, rows, cw = u.shape
    st4 = 4 * S5_STATE
    blk = lambda *shape: pl.BlockSpec((1,) + shape, lambda i: (i, 0, 0))
    return pl.pallas_call(
        functools.partial(_s5_kernel, n_chunks=n_chunks, bsz=bsz),
        grid=(grp,),
        in_specs=[blk(rows, cw), blk(S5_GROUP, 2 * cw), blk(cw, st4), blk(st4, cw), blk(32, 2 * S5_STATE), blk(bsz, st4)],
        out_specs=[blk(rows, cw), blk(rows, st4)],
        out_shape=[jax.ShapeDtypeStruct((grp, rows, cw), F32), jax.ShapeDtypeStruct((grp, rows, st4), F32)],
        scratch_shapes=[pltpu.VMEM((cw, cw), BF16)],
        compiler_params=pltpu.CompilerParams(dimension_semantics=("parallel",), vmem_limit_bytes=VMEM_LIMIT),
        name="s5_scan",
    )(u, ext, wst, rd, pq, h0)


def s5_sequence(u, tables, h0):
    bsz, n, _ = u.shape
    n_chunks = n // S5_T
    rows = bsz * n_chunks
    rows_pad = -(-rows // 16) * 16
    ug = u.astype(BF16).reshape(bsz, n_chunks, S5_T, S5_GROUPS, S5_GROUP)
    ug = jnp.transpose(ug, (3, 0, 1, 2, 4)).reshape(S5_GROUPS, rows, S5_T * S5_GROUP)
    if rows_pad != rows:
        ug = jnp.pad(ug, ((0, 0), (0, rows_pad - rows), (0, 0)))
    y, hfin = s5_scan(ug, tables, h0, n_chunks, bsz)
    y = y[:, :rows].reshape(S5_GROUPS, bsz, n_chunks, S5_T, S5_GROUP)
    y = jnp.transpose(y, (1, 2, 3, 0, 4)).reshape(bsz, n, S5_WIDTH)
    hfin = hfin[:, :rows].reshape(S5_GROUPS, bsz, n_chunks, 4 * S5_STATE)
    st2 = 2 * S5_STATE
    h_end = jnp.concatenate([hfin[:, :, n_chunks - 1, :st2], hfin[:, :, 0, st2:]], axis=-1)
    return y, h_end


def s5_glu(y, p):
    y = jax.nn.gelu(y)
    y = y * jax.nn.sigmoid(mm3(y, p['s5_glu_w'], "s5_glu") + p['s5_glu_b'])
    return rms_norm(y, p['s5_norm_g'])


def s5_mixer(u, uc, p, need_ctx_out):
    bsz = u.shape[0]
    tables = s5_tables(p)
    yc, h_ctx = s5_sequence(uc, tables, jnp.zeros((S5_GROUPS, bsz, 4 * S5_STATE), F32))
    y, _ = s5_sequence(u, tables, h_ctx)
    out = s5_glu(u * p['s5_d'] + y, p)
    out_c = s5_glu(uc * p['s5_d'] + yc, p) if need_ctx_out else None
    return out, out_c


def dn_features(z_qkv, conv_w):
    bsz, seq_len, _ = z_qkv.shape
    q, k, v = jnp.split(jax.nn.silu(depthwise_conv(z_qkv, conv_w)), 3, axis=-1)
    shp = (bsz, seq_len, DN_HEADS, DN_HEAD_DIM)
    return l2_normalize(q.reshape(shp)) * DN_HEAD_DIM ** -0.5, l2_normalize(k.reshape(shp)), v.reshape(shp)


def dn_gates(z_a, z_b, a_log, dt_bias):
    bsz, seq_len, _ = z_a.shape
    shp = (bsz, seq_len, 2, DN_HEADS)
    g = -jnp.exp(a_log) * jax.nn.softplus(z_a.reshape(shp) + dt_bias)
    beta = jax.nn.sigmoid(z_b.reshape(shp))
    return g, beta


def _dn_kernel(q_ref, k_ref, v_ref, g_ref, gt_ref, b_ref, s0_ref, o_ref, sfin_ref, s_ref, *, n_chunks):
    cs, hd = DN_CHUNK, DN_HEAD_DIM
    step = pl.program_id(1)

    @pl.when(step == 0)
    def _():
        s_ref[...] = s0_ref[0]

    r_i = lax.broadcasted_iota(jnp.int32, (cs, cs), 0)
    c_i = lax.broadcasted_iota(jnp.int32, (cs, cs), 1)
    incl = r_i >= c_i
    strict = r_i > c_i
    hp = lax.Precision.HIGHEST
    g = g_ref[0, 0]
    beta = b_ref[0, 0]
    gc = jnp.dot(incl.astype(F32), g, preferred_element_type=F32, precision=hp)
    gc_t = jnp.dot(gt_ref[0, 0], (c_i >= r_i).astype(F32), preferred_element_type=F32, precision=hp)
    g_tot = gc[cs - 1:cs, :]
    e_g = jnp.exp(gc)
    e_tail = jnp.exp(g_tot - gc)
    e_tot = jnp.exp(g_tot)

    def mm(a, b):
        return jnp.dot(a.astype(BF16), b.astype(BF16), preferred_element_type=F32)

    def split(a):
        hi = a.astype(BF16)
        return hi, (a - hi.astype(F32)).astype(BF16)

    def mm_3pass(a, b):
        a_hi, a_lo = split(a)
        b_hi, b_lo = split(b)
        m = a.shape[0]
        t = jnp.dot(jnp.concatenate([a_hi, a_lo], axis=0), b_hi, preferred_element_type=F32)
        return t[:m] + t[m:] + jnp.dot(a_hi, b_lo, preferred_element_type=F32)

    for h in range(DN_HEADS):
        sl = slice(h * hd, (h + 1) * hd)
        q, k, v = q_ref[0, :, sl], k_ref[0, :, sl], v_ref[0, :, sl]
        b_col = beta[:, h:h + 1]
        kb = k * b_col
        decay = jnp.where(incl, jnp.exp(jnp.where(incl, gc[:, h:h + 1] - gc_t[h:h + 1, :], 0.0)), 0.0)
        kq = lax.dot_general(jnp.concatenate([kb, q], axis=0).astype(BF16), k.astype(BF16), (((1,), (1,)), ((), ())), preferred_element_type=F32)
        n = jnp.where(strict, kq[:cs] * decay, 0.0)
        qk = kq[cs:] * decay
        x = jnp.concatenate([v * b_col, kb * e_g[:, h:h + 1]], axis=1)
        x = x - mm_3pass(n, x)
        pw = n
        for _ in range(5):
            pw = mm_3pass(pw, pw)
            x = x + mm_3pass(pw, x)
        u, w = x[:, :hd], x[:, hd:]
        s = s_ref[h]
        ws = mm(jnp.concatenate([w, q * e_g[:, h:h + 1]], axis=0), s)
        v_new = u - ws[:cs]
        o_ref[0, :, sl] = ws[cs:] + mm(qk, v_new)
        kt = (k * e_tail[:, h:h + 1]).astype(BF16)
        s_ref[h] = s * e_tot[:, h:h + 1] + lax.dot_general(kt, v_new.astype(BF16), (((0,), (0,)), ((), ())), preferred_element_type=F32)

    @pl.when(step == n_chunks - 1)
    def _():
        sfin_ref[0] = s_ref[...]


def gated_delta_rule(q, k, v, g, beta, s0):
    bsz, seq_len, heads, hd = q.shape
    width = heads * hd
    n_chunks = seq_len // DN_CHUNK
    q, k, v = (t.reshape(bsz, seq_len, width) for t in (q, k, v))
    g4 = g.reshape(bsz, n_chunks, DN_CHUNK, heads)
    gt4 = jnp.swapaxes(g4, 2, 3)
    b4 = beta.reshape(bsz, n_chunks, DN_CHUNK, heads)
    tok_spec = pl.BlockSpec((1, DN_CHUNK, width), lambda b, i: (b, i, 0))
    gate_spec = pl.BlockSpec((1, 1, DN_CHUNK, heads), lambda b, i: (b, i, 0, 0))
    gate_t_spec = pl.BlockSpec((1, 1, heads, DN_CHUNK), lambda b, i: (b, i, 0, 0))
    state_spec = pl.BlockSpec((1, heads, hd, hd), lambda b, i: (b, 0, 0, 0))
    o, s_fin = pl.pallas_call(
        functools.partial(_dn_kernel, n_chunks=n_chunks),
        grid=(bsz, n_chunks),
        in_specs=[tok_spec, tok_spec, tok_spec, gate_spec, gate_t_spec, gate_spec, state_spec],
        out_specs=[tok_spec, state_spec],
        out_shape=[jax.ShapeDtypeStruct((bsz, seq_len, width), F32), jax.ShapeDtypeStruct(s0.shape, F32)],
        scratch_shapes=[pltpu.VMEM((heads, hd, hd), F32)],
        compiler_params=pltpu.CompilerParams(dimension_semantics=("parallel", "arbitrary"), vmem_limit_bytes=VMEM_LIMIT),
        name="delta_rule",
    )(q, k, v, g4, gt4, b4, s0)
    return o.reshape(bsz, seq_len, heads, hd), s_fin


def dn_output(o, z_gate, p):
    bsz, seq_len = o.shape[:2]
    gate = jax.nn.silu(z_gate).reshape(bsz, seq_len, DN_HEADS, DN_HEAD_DIM)
    return (rms_norm(o, p['dn_norm_g']) * gate).reshape(bsz, seq_len, DN_WIDTH)


def deltanet_mixer(z, zc, p, need_ctx_out):
    q, k, v = dn_features(z['qkv'], p['dn_conv_w'])
    g, beta = dn_gates(z['a'], z['b'], p['dn_a_log'], p['dn_dt_bias'])
    qc, kc, vc = dn_features(zc['qkv'], p['dn_conv_w'])
    gc, betac = dn_gates(zc['a'], zc['b'], p['dn_a_log'], p['dn_dt_bias'])
    s0 = jnp.zeros((q.shape[0], DN_HEADS, DN_HEAD_DIM, DN_HEAD_DIM), F32)
    o, oc = 0.0, 0.0
    for d in range(2):
        rev = d == 1
        oc_d, s_ctx = gated_delta_rule(*(_flip(t, rev) for t in (qc, kc, vc, gc[:, :, d], betac[:, :, d])), s0)
        o_d, _ = gated_delta_rule(*(_flip(t, rev) for t in (q, k, v, g[:, :, d], beta[:, :, d])), s_ctx)
        o = o + _flip(o_d, rev)
        if need_ctx_out:
            oc = oc + _flip(oc_d, rev)
    y = dn_output(o, z['g'], p)
    yc = dn_output(oc, zc['g'], p) if need_ctx_out else None
    return y, yc


def hyena_filters(seq_len, p):
    hp = lax.Precision.HIGHEST
    t = jnp.linspace(0.0, 1.0, seq_len, dtype=F32)[:, None]
    w = (2.0 * math.pi / seq_len) * jnp.arange(seq_len, dtype=F32)[:, None]
    f = jnp.linspace(1e-4, HY_BANDS - 1, HY_BANDS, dtype=F32)[None, :]
    feats = jnp.concatenate([t, jnp.cos(w * f), -jnp.sin(w * f)], axis=-1)
    freq = p['hy_f_freq']
    h = jnp.sin(freq * (jnp.dot(feats, p['hy_f_w1'], precision=hp) + p['hy_f_b1']))
    h = jnp.sin(freq * (jnp.dot(h, p['hy_f_w2'], precision=hp) + p['hy_f_b2']))
    h = jnp.sin(freq * (jnp.dot(h, p['hy_f_w3'], precision=hp) + p['hy_f_b3']))
    h = jnp.dot(h, p['hy_f_w4'], precision=hp).reshape(seq_len, 2, HY_WIDTH)
    rates = jnp.linspace(math.log(HY_DECAY_TARGET) / HY_DECAY_LONG_PCT, math.log(HY_DECAY_TARGET) / HY_DECAY_SHORT_PCT, HY_WIDTH, dtype=F32)
    h = h * jnp.exp(-t * jnp.abs(rates))[:, None, :]
    return h[:, 0], h[:, 1]


def two_sided_long_conv(u, h_fwd, h_bwd):
    seq_len = u.shape[1]
    taps = jnp.concatenate([h_fwd[:1] + h_bwd[:1], h_fwd[1:], jnp.zeros_like(h_fwd[:1]), h_bwd[:0:-1]], axis=0)
    u_f = jnp.fft.rfft(u, n=2 * seq_len, axis=1)
    t_f = jnp.fft.rfft(taps, axis=0)
    return jnp.fft.irfft(u_f * t_f[None], n=2 * seq_len, axis=1)[:, :seq_len]


def hyena_mixer(zh, p):
    zs = depthwise_conv(zh, p['hy_conv_w']) + p['hy_conv_b']
    x0, x1, v = jnp.split(zs, 3, axis=-1)
    z = x1 * v
    h_fwd, h_bwd = hyena_filters(zh.shape[1], p)
    y = x0 * (two_sided_long_conv(z, h_fwd, h_bwd) + p['hy_bias'] * z)
    return rms_norm(y, p['hy_norm_g'])


def in_projection(h, p, full):
    bsz, n, k = h.shape
    hb = h.reshape(bsz * n, k).astype(BF16)
    out = {}
    out['s5'] = matmul(hb, p['w_in_s5'], name="in_s5").reshape(bsz, n, -1)
    out['qkv'] = matmul(hb, p['w_in_qkv'], name="in_qkv").reshape(bsz, n, -1)
    ab = matmul(hb, p['w_in_ab'], name="in_ab").reshape(bsz, n, -1)
    out['a'] = ab[..., :2 * DN_HEADS]
    out['b'] = ab[..., 2 * DN_HEADS:4 * DN_HEADS]
    if full:
        out['g'] = matmul(hb, p['w_in_g'], name="in_g").reshape(bsz, n, -1)
        out['hy'] = matmul(hb, p['w_in_hy'], name="in_hy").reshape(bsz, n, -1)
    return out


def token_mixer(h, hc, p, need_ctx_out):
    z = in_projection(h, p, True)
    zc = in_projection(hc, p, need_ctx_out)
    s5_y, s5_yc = s5_mixer(z['s5'], zc['s5'], p, need_ctx_out)
    dn_y, dn_yc = deltanet_mixer(z, zc, p, need_ctx_out)
    y = mm3(jnp.concatenate([s5_y, dn_y, hyena_mixer(z['hy'], p)], axis=-1).astype(BF16), p['w_out'], "out_proj")
    if not need_ctx_out:
        return y, None
    yc = mm3(jnp.concatenate([s5_yc, dn_yc, hyena_mixer(zc['hy'], p)], axis=-1).astype(BF16), p['w_out'], "out_proj_ctx")
    return y, yc


def ada_modulation(c_rows, w_ada, b_ada):
    rows = c_rows.shape[0]
    a = jnp.zeros((8, c_rows.shape[1]), F32).at[:rows].set(jax.nn.silu(c_rows))
    return matmul(a, w_ada, name="ada")[:rows] + b_ada


def trunk_layer(x, xc, c, c_ctx, p, last):
    bsz, n, dm = x.shape
    lc = xc.shape[1]
    ada = ada_modulation(jnp.concatenate([c, c_ctx[None]], axis=0), p['w_ada'], p['b_ada'])
    sh1, sc1, g1, sh2, sc2, g2 = jnp.split(ada[:bsz, None, :], 6, axis=-1)
    mc = jnp.split(ada[bsz], 6, axis=-1)
    y, yc = token_mixer(modulate(x, sh1, sc1), modulate(xc, mc[0], mc[1]), p, not last)
    x = layer_norm(DEEPNORM_ALPHA * x + g1 * y, p['ln1_g'], p['ln1_b'])
    if last:
        f = hier_moe(modulate(x, sh2, sc2).reshape(bsz * n, dm), p).reshape(bsz, n, dm)
        return layer_norm(DEEPNORM_ALPHA * x + g2 * f, p['ln2_g'], p['ln2_b']), None
    xc = layer_norm(DEEPNORM_ALPHA * xc + mc[2] * yc, p['ln1_g'], p['ln1_b'])
    tokens = jnp.concatenate([modulate(x, sh2, sc2).reshape(bsz * n, dm), modulate(xc, mc[3], mc[4]).reshape(bsz * lc, dm)], axis=0)
    f = hier_moe(tokens, p)
    x = layer_norm(DEEPNORM_ALPHA * x + g2 * f[:bsz * n].reshape(bsz, n, dm), p['ln2_g'], p['ln2_b'])
    xc = layer_norm(DEEPNORM_ALPHA * xc + mc[5] * f[bsz * n:].reshape(bsz, lc, dm), p['ln2_g'], p['ln2_b'])
    return x, xc


def kernel(x, c, ctx, c_ctx, w_ada, b_ada, w_in, s5_lam_re, s5_lam_im, s5_log_step, s5_b_re, s5_b_im, s5_c_re, s5_c_im, s5_d, s5_glu_w, s5_glu_b, s5_norm_g, dn_conv_w, dn_a_log, dn_dt_bias, dn_norm_g, hy_conv_w, hy_conv_b, hy_f_w1, hy_f_b1, hy_f_w2, hy_f_b2, hy_f_w3, hy_f_b3, hy_f_freq, hy_f_w4, hy_bias, hy_norm_g, w_out, ln1_g, ln1_b, ln2_g, ln2_b, moe_w_group, moe_b_group, moe_w_expert, moe_b_expert, moe_w_gate, moe_w_up, moe_w_down):
    xc = ctx
    for l in range(DEPTH):
        w_in_l = w_in[l]
        w_ab = jnp.zeros((D_MODEL, LANE), F32).at[:, :4 * DN_HEADS].set(w_in_l[:, COL_DN_A:STATE_COLS])
        w_router = jnp.zeros((D_MODEL, LANE), F32).at[:, :N_GROUPS].set(moe_w_group[l]).at[:, N_GROUPS:N_GROUPS + N_EXPERTS].set(moe_w_expert[l])
        p = {
            'w_ada': w_ada[l], 'b_ada': b_ada[l],
            'w_in_s5': w_in_l[:, COL_S5:COL_DN_QKV].astype(BF16),
            'w_in_qkv': w_in_l[:, COL_DN_QKV:COL_DN_A].astype(BF16),
            'w_in_ab': w_ab.astype(BF16),
            'w_in_g': w_in_l[:, COL_DN_G:COL_HY].astype(BF16),
            'w_in_hy': w_in_l[:, COL_HY:].astype(BF16),
            's5_lam_re': s5_lam_re[l], 's5_lam_im': s5_lam_im[l], 's5_log_step': s5_log_step[l],
            's5_b_re': s5_b_re[l], 's5_b_im': s5_b_im[l], 's5_c_re': s5_c_re[l], 's5_c_im': s5_c_im[l],
            's5_d': s5_d[l], 's5_glu_w': s5_glu_w[l].astype(BF16), 's5_glu_b': s5_glu_b[l], 's5_norm_g': s5_norm_g[l],
            'dn_conv_w': dn_conv_w[l], 'dn_a_log': dn_a_log[l], 'dn_dt_bias': dn_dt_bias[l], 'dn_norm_g': dn_norm_g[l],
            'hy_conv_w': hy_conv_w[l], 'hy_conv_b': hy_conv_b[l],
            'hy_f_w1': hy_f_w1[l], 'hy_f_b1': hy_f_b1[l], 'hy_f_w2': hy_f_w2[l], 'hy_f_b2': hy_f_b2[l],
            'hy_f_w3': hy_f_w3[l], 'hy_f_b3': hy_f_b3[l], 'hy_f_freq': hy_f_freq[l], 'hy_f_w4': hy_f_w4[l],
            'hy_bias': hy_bias[l], 'hy_norm_g': hy_norm_g[l], 'w_out': w_out[l].astype(BF16),
            'ln1_g': ln1_g[l], 'ln1_b': ln1_b[l], 'ln2_g': ln2_g[l], 'ln2_b': ln2_b[l],
            'moe_w_router': w_router, 'moe_b_group': moe_b_group[l], 'moe_b_expert': moe_b_expert[l],
            'moe_w_gate': moe_w_gate[l].astype(BF16), 'moe_w_up': moe_w_up[l].astype(BF16), 'moe_w_down': moe_w_down[l].astype(BF16),
        }
        x, xc = trunk_layer(x, xc, c, c_ctx, p, l == DEPTH - 1)
    return x
```

```python
import functools
import math

import jax
import jax.numpy as jnp
from jax import lax
from jax.experimental import pallas as pl
from jax.experimental.pallas import tpu as pltpu

D_MODEL = 4096
DEPTH = 2
GRID_W = 64

S5_WIDTH = D_MODEL // 4
S5_GROUP = 16
S5_GROUPS = S5_WIDTH // S5_GROUP
S5_STATE = 64

DN_WIDTH = D_MODEL // 2
DN_HEAD_DIM = 128
DN_HEADS = DN_WIDTH // DN_HEAD_DIM
DN_CHUNK = 64
SHORT_CONV = 3

HY_WIDTH = D_MODEL - S5_WIDTH - DN_WIDTH
HY_BANDS = 16
HY_EMB = 1 + 2 * HY_BANDS
HY_HIDDEN = 64
HY_DECAY_SHORT_PCT = 0.3
HY_DECAY_LONG_PCT = 1.5
HY_DECAY_TARGET = 1e-2

MIX_WIDTH = S5_WIDTH + DN_WIDTH + HY_WIDTH

COL_S5 = 0
COL_DN_QKV = COL_S5 + S5_WIDTH
COL_DN_A = COL_DN_QKV + 3 * DN_WIDTH
COL_DN_B = COL_DN_A + 2 * DN_HEADS
STATE_COLS = COL_DN_B + 2 * DN_HEADS
COL_DN_G = STATE_COLS
COL_HY = COL_DN_G + DN_WIDTH
IN_WIDTH = COL_HY + 3 * HY_WIDTH

N_GROUPS = 4
EXPERTS_PER_GROUP = 8
N_EXPERTS = N_GROUPS * EXPERTS_PER_GROUP
TOP_K = 2
D_EXPERT = 512

DEEPNORM_ALPHA = (2 * DEPTH) ** 0.25
LN_EPS = 1e-5
RMS_EPS = 1e-6
F32 = jnp.float32
BF16 = jnp.bfloat16

V7X_VMEM_BYTES = 64 * 1024 * 1024
VMEM_LIMIT = 52 * 1024 * 1024
LANE = 128

MOE_TILE = 256
DN_HEAD_GROUP = 4
S5_T = 64


def _mm_kernel(a_ref, b_ref, o_ref):
    o_ref[...] = jnp.dot(a_ref[...].astype(BF16), b_ref[...].astype(BF16), preferred_element_type=F32).astype(o_ref.dtype)


def _pick(n, prefs):
    for p in prefs:
        if n % p == 0:
            return p
    return n


def matmul(a, b, out_dtype=F32, name="matmul"):
    m, k = a.shape
    n = b.shape[1]
    tm = _pick(m, (512, 256, 128, 64, 32, 16, 8))
    tn = _pick(n, (1024, 512, 256, 128))
    return pl.pallas_call(
        _mm_kernel,
        grid=(m // tm, n // tn),
        in_specs=[pl.BlockSpec((tm, k), lambda i, j: (i, 0)), pl.BlockSpec((k, tn), lambda i, j: (0, j))],
        out_specs=pl.BlockSpec((tm, tn), lambda i, j: (i, j)),
        out_shape=jax.ShapeDtypeStruct((m, n), out_dtype),
        compiler_params=pltpu.CompilerParams(dimension_semantics=("parallel", "parallel"), vmem_limit_bytes=VMEM_LIMIT),
        name=name,
    )(a, b)


def _mm_f32_kernel(a_ref, b_ref, o_ref):
    o_ref[...] = jnp.dot(a_ref[...], b_ref[...], preferred_element_type=F32, precision=lax.Precision.HIGHEST)


def matmul_f32(a, b, name="matmul_f32"):
    m, k = a.shape
    n = b.shape[1]
    tm = _pick(m, (512, 256, 128, 64, 32, 16, 8))
    return pl.pallas_call(
        _mm_f32_kernel,
        grid=(m // tm,),
        in_specs=[pl.BlockSpec((tm, k), lambda i: (i, 0)), pl.BlockSpec((k, n), lambda i: (0, 0))],
        out_specs=pl.BlockSpec((tm, n), lambda i: (i, 0)),
        out_shape=jax.ShapeDtypeStruct((m, n), F32),
        compiler_params=pltpu.CompilerParams(dimension_semantics=("parallel",), vmem_limit_bytes=VMEM_LIMIT),
        name=name,
    )(a, b)


def _cast_kernel(x_ref, o_ref):
    o_ref[0] = x_ref[0, 0].astype(o_ref.dtype)


def cast_layer_experts(w, layer, dtype=BF16):
    _, n_exp, r, c = w.shape
    return pl.pallas_call(
        _cast_kernel,
        grid=(n_exp,),
        in_specs=[pl.BlockSpec((1, 1, r, c), lambda e: (layer, e, 0, 0))],
        out_specs=pl.BlockSpec((1, r, c), lambda e: (e, 0, 0)),
        out_shape=jax.ShapeDtypeStruct((n_exp, r, c), dtype),
        compiler_params=pltpu.CompilerParams(dimension_semantics=("parallel",), vmem_limit_bytes=VMEM_LIMIT),
        name="cast_experts",
    )(w)


def _moe_kernel(tile_expert_ref, n_used_ref, x_ref, w_ref, wg_ref, wu_ref, wd_ref, o_ref):
    i = pl.program_id(0)

    @pl.when(i < n_used_ref[0])
    def _():
        x = x_ref[...]
        g = jnp.dot(x, wg_ref[0], preferred_element_type=F32)
        u = jnp.dot(x, wu_ref[0], preferred_element_type=F32)
        h = (g * jax.nn.sigmoid(g)) * u * w_ref[...]
        o_ref[...] = jnp.dot(h.astype(BF16), wd_ref[0], preferred_element_type=F32)

    @pl.when(i >= n_used_ref[0])
    def _():
        o_ref[...] = jnp.zeros_like(o_ref)


def moe_experts(tile_expert, n_used, xs, row_w, wg, wu, wd):
    p_rows, dm = xs.shape
    n_tiles = p_rows // MOE_TILE
    grid_spec = pltpu.PrefetchScalarGridSpec(
        num_scalar_prefetch=2,
        grid=(n_tiles,),
        in_specs=[
            pl.BlockSpec((MOE_TILE, dm), lambda i, te, nu: (i, 0)),
            pl.BlockSpec((MOE_TILE, 1), lambda i, te, nu: (i, 0)),
            pl.BlockSpec((1, dm, D_EXPERT), lambda i, te, nu: (te[i], 0, 0)),
            pl.BlockSpec((1, dm, D_EXPERT), lambda i, te, nu: (te[i], 0, 0)),
            pl.BlockSpec((1, D_EXPERT, dm), lambda i, te, nu: (te[i], 0, 0)),
        ],
        out_specs=pl.BlockSpec((MOE_TILE, dm), lambda i, te, nu: (i, 0)),
    )
    return pl.pallas_call(
        _moe_kernel,
        grid_spec=grid_spec,
        out_shape=jax.ShapeDtypeStruct((p_rows, dm), F32),
        compiler_params=pltpu.CompilerParams(dimension_semantics=("arbitrary",), vmem_limit_bytes=VMEM_LIMIT),
        name="moe_experts",
    )(tile_expert, n_used, xs, row_w, wg, wu, wd)


def hier_moe(t, p):
    n_tok = t.shape[0]
    logits = matmul_f32(t, p['moe_w_router'], name="moe_router")
    group_logits = logits[:, :N_GROUPS] + p['moe_b_group']
    group = jnp.argmax(group_logits, axis=-1)
    group_w = jnp.max(jax.nn.softmax(group_logits, axis=-1), axis=-1, keepdims=True)
    exp_logits = (logits[:, N_GROUPS:N_GROUPS + N_EXPERTS] + p['moe_b_expert']).reshape(n_tok, N_GROUPS, EXPERTS_PER_GROUP)
    exp_logits = jnp.take_along_axis(exp_logits, group[:, None, None], axis=1)[:, 0]
    top_logits, top_idx = lax.top_k(exp_logits, TOP_K)
    top_w = jax.nn.softmax(top_logits, axis=-1) * group_w
    expert_id = (group[:, None] * EXPERTS_PER_GROUP + top_idx).astype(jnp.int32)

    n_rows = n_tok * TOP_K
    flat_e = expert_id.reshape(n_rows)
    flat_w = top_w.reshape(n_rows)
    flat_t = jnp.arange(n_rows, dtype=jnp.int32) // TOP_K
    order = jnp.argsort(flat_e, stable=True)
    sorted_e = flat_e[order]
    counts = jnp.zeros((N_EXPERTS,), jnp.int32).at[flat_e].add(1)
    padded = ((counts + MOE_TILE - 1) // MOE_TILE) * MOE_TILE
    pad_end = jnp.cumsum(padded)
    pad_start = pad_end - padded
    start = jnp.cumsum(counts) - counts
    dest = pad_start[sorted_e] + (jnp.arange(n_rows, dtype=jnp.int32) - start[sorted_e])
    p_rows = n_rows + N_EXPERTS * MOE_TILE
    n_tiles = p_rows // MOE_TILE
    row_token = jnp.zeros((p_rows,), jnp.int32).at[dest].set(flat_t[order])
    row_w = jnp.zeros((p_rows,), F32).at[dest].set(flat_w[order])
    pos = jnp.zeros((n_rows,), jnp.int32).at[order].set(dest).reshape(n_tok, TOP_K)
    n_used = (pad_end[-1] // MOE_TILE).astype(jnp.int32)
    tile_idx = jnp.minimum(jnp.arange(n_tiles, dtype=jnp.int32), n_used - 1)
    tile_expert = jnp.minimum(jnp.searchsorted(pad_end, tile_idx * MOE_TILE, side='right'), N_EXPERTS - 1).astype(jnp.int32)

    xs = t.astype(BF16)[row_token]
    ys = moe_experts(tile_expert, n_used.reshape(1), xs, row_w[:, None], p['moe_w_gate'], p['moe_w_up'], p['moe_w_down'])
    return ys[pos[:, 0]] + ys[pos[:, 1]]


def _flip(t, rev):
    return jnp.flip(t, axis=1) if rev else t


def layer_norm(x, g, b):
    mu = jnp.mean(x, axis=-1, keepdims=True)
    var = jnp.mean(jnp.square(x - mu), axis=-1, keepdims=True)
    return (x - mu) * lax.rsqrt(var + LN_EPS) * g + b


def rms_norm(x, g):
    return x * lax.rsqrt(jnp.mean(jnp.square(x), axis=-1, keepdims=True) + RMS_EPS) * g


def l2_normalize(t):
    return t * lax.rsqrt(jnp.sum(t * t, axis=-1, keepdims=True) + 1e-6)


def depthwise_conv(u, w):
    k = w.shape[0]
    return lax.conv_general_dilated(u, w[:, None, :], window_strides=(1,), padding=[(k // 2, k // 2)], dimension_numbers=('NWC', 'WIO', 'NWC'), feature_group_count=u.shape[-1])


def modulate(t, shift, scale):
    return t * (1 + scale) + shift


def mm3(t, w, name):
    bsz, n, k = t.shape
    return matmul(t.reshape(bsz * n, k), w, name=name).reshape(bsz, n, w.shape[1])


def s5_discretize(lam_re, lam_im, log_step, b_re, b_im):
    step = jnp.exp(log_step)[:, None]
    mag = jnp.exp(lam_re * step)
    a_re = mag * jnp.cos(lam_im * step)
    a_im = mag * jnp.sin(lam_im * step)
    den = lam_re * lam_re + lam_im * lam_im
    f_re = ((a_re - 1.0) * lam_re + a_im * lam_im) / den
    f_im = (a_im * lam_re - (a_re - 1.0) * lam_im) / den
    bb_re = f_re[..., None] * b_re - f_im[..., None] * b_im
    bb_im = f_re[..., None] * b_im + f_im[..., None] * b_re
    return a_re, a_im, bb_re, bb_im


def _cmul(x, y):
    return x[0] * y[0] - x[1] * y[1], x[0] * y[1] + x[1] * y[0]


def s5_tables(p):
    hp = lax.Precision.HIGHEST
    t_len, grp, st = S5_T, S5_GROUPS, S5_STATE
    taps, wst, rd, pq = [], [], [], []
    for d in range(2):
        a_re, a_im, bb_re, bb_im = s5_discretize(p['s5_lam_re'][d], p['s5_lam_im'][d], p['s5_log_step'][d], p['s5_b_re'][d], p['s5_b_im'][d])
        c_re, c_im = p['s5_c_re'][d], p['s5_c_im'][d]
        pr, pi = lax.associative_scan(_cmul, (jnp.broadcast_to(a_re, (t_len, grp, st)), jnp.broadcast_to(a_im, (t_len, grp, st))), axis=0)
        pw_re = jnp.concatenate([jnp.ones((1, grp, st), F32), pr], axis=0)
        pw_im = jnp.concatenate([jnp.zeros((1, grp, st), F32), pi], axis=0)
        ca_re = c_re[None] * pw_re[:, :, None, :] - c_im[None] * pw_im[:, :, None, :]
        ca_im = c_re[None] * pw_im[:, :, None, :] + c_im[None] * pw_re[:, :, None, :]
        taps.append(jnp.einsum('tgip,gpj->tgij', ca_re[:t_len], bb_re, precision=hp) - jnp.einsum('tgip,gpj->tgij', ca_im[:t_len], bb_im, precision=hp))
        e_st = jnp.arange(t_len - 1, -1, -1) if d == 0 else jnp.arange(t_len)
        w_re = pw_re[e_st][..., None] * bb_re[None] - pw_im[e_st][..., None] * bb_im[None]
        w_im = pw_re[e_st][..., None] * bb_im[None] + pw_im[e_st][..., None] * bb_re[None]
        to_rows = lambda w: jnp.transpose(w, (1, 0, 3, 2)).reshape(grp, t_len * S5_GROUP, st)
        wst += [to_rows(w_re), to_rows(w_im)]
        e_rd = jnp.arange(1, t_len + 1) if d == 0 else jnp.arange(t_len, 0, -1)
        to_cols = lambda r: jnp.transpose(r, (1, 3, 0, 2)).reshape(grp, st, t_len * S5_GROUP)
        rd += [to_cols(ca_re[e_rd]), to_cols(-ca_im[e_rd])]
        lv_re, lv_im = pw_re[t_len], pw_im[t_len]
        for _ in range(8):
            pq += [jnp.concatenate([lv_re, lv_re], axis=-1), jnp.concatenate([-lv_im, lv_im], axis=-1)]
            lv_re, lv_im = _cmul((lv_re, lv_im), (lv_re, lv_im))
    kf, kb = taps
    blocks = jnp.concatenate([kb[1:][::-1], (kf[0] + kb[0])[None], kf[1:], jnp.zeros_like(kf[:1])], axis=0)
    ext = jnp.transpose(blocks, (1, 3, 0, 2)).reshape(grp, S5_GROUP, 2 * t_len * S5_GROUP)
    return ext, jnp.concatenate(wst, axis=-1).astype(BF16), jnp.concatenate(rd, axis=1).astype(BF16), jnp.stack(pq, axis=1)


def _s5_kernel(u_ref, ext_ref, wst_ref, rd_ref, pq_ref, h0_ref, y_ref, hfin_ref, m_ref, *, n_chunks, bsz):
    cw = S5_T * S5_GROUP
    st2 = 2 * S5_STATE
    ext = ext_ref[0]
    for s in range(S5_T):
        off = S5_GROUP * (S5_T - 1 - s)
        shifted = ext if off == 0 else pltpu.roll(ext, shift=2 * cw - off, axis=1)
        m_ref[S5_GROUP * s:S5_GROUP * (s + 1), :] = shifted[:, :cw].astype(BF16)
    u = u_ref[0]
    rows = u.shape[0]
    y = jnp.dot(u, m_ref[...], preferred_element_type=F32)
    s_all = jnp.dot(u, wst_ref[0], preferred_element_type=F32)
    sf, sb = s_all[:, :st2], s_all[:, st2:]
    row = lax.broadcasted_iota(jnp.int32, (rows, st2), 0)
    c_idx = row % n_chunks

    def cmul_rows(idx, x):
        return pq_ref[0, idx:idx + 1, :] * x + pq_ref[0, idx + 1:idx + 2, :] * pltpu.roll(x, shift=S5_STATE, axis=1)

    h0 = h0_ref[0]
    h0f = jnp.zeros((rows, st2), F32)
    h0b = jnp.zeros((rows, st2), F32)
    for b in range(bsz):
        in_b = (row >= b * n_chunks) & (row < (b + 1) * n_chunks)
        h0f = jnp.where(in_b, h0[b:b + 1, :st2], h0f)
        h0b = jnp.where(in_b, h0[b:b + 1, st2:], h0b)
    hf = jnp.where(c_idx == 0, h0f, pltpu.roll(sf, shift=1, axis=0))
    hb = jnp.where(c_idx == n_chunks - 1, h0b, pltpu.roll(sb, shift=rows - 1, axis=0))
    level, sh = 0, 1
    while sh < n_chunks:
        dn = jnp.where(c_idx >= sh, pltpu.roll(hf, shift=sh, axis=0), 0.0)
        up = jnp.where(c_idx < n_chunks - sh, pltpu.roll(hb, shift=rows - sh, axis=0), 0.0)
        hf = hf + cmul_rows(2 * level, dn)
        hb = hb + cmul_rows(16 + 2 * level, up)
        level, sh = level + 1, sh * 2
    h_in = jnp.concatenate([hf, hb], axis=1)
    y_ref[0] = y + jnp.dot(h_in.astype(BF16), rd_ref[0], preferred_element_type=F32)
    hfin_ref[0] = jnp.concatenate([cmul_rows(0, hf) + sf, cmul_rows(16, hb) + sb], axis=1)


def s5_scan(u, tables, h0, n_chunks, bsz):
    ext, wst, rd, pq = tables
    grp, rows, cw = u.shape
    st4 = 4 * S5_STATE
    blk = lambda *shape: pl.BlockSpec((1,) + shape, lambda i: (i, 0, 0))
    return pl.pallas_call(
        functools.partial(_s5_kernel, n_chunks=n_chunks, bsz=bsz),
        grid=(grp,),
        in_specs=[blk(rows, cw), blk(S5_GROUP, 2 * cw), blk(cw, st4), blk(st4, cw), blk(32, 2 * S5_STATE), blk(bsz, st4)],
        out_specs=[blk(rows, cw), blk(rows, st4)],
        out_shape=[jax.ShapeDtypeStruct((grp, rows, cw), F32), jax.ShapeDtypeStruct((grp, rows, st4), F32)],
        scratch_shapes=[pltpu.VMEM((cw, cw), BF16)],
        compiler_params=pltpu.CompilerParams(dimension_semantics=("parallel",), vmem_limit_bytes=VMEM_LIMIT),
        name="s5_scan",
    )(u, ext, wst, rd, pq, h0)


def s5_sequence(u, tables, h0):
    bsz, n, _ = u.shape
    n_chunks = n // S5_T
    rows = bsz * n_chunks
    rows_pad = -(-rows // 16) * 16
    ug = u.astype(BF16).reshape(bsz, n_chunks, S5_T, S5_GROUPS, S5_GROUP)
    ug = jnp.transpose(ug, (3, 0, 1, 2, 4)).reshape(S5_GROUPS, rows, S5_T * S5_GROUP)
    if rows_pad != rows:
        ug = jnp.pad(ug, ((0, 0), (0, rows_pad - rows), (0, 0)))
    y, hfin = s5_scan(ug, tables, h0, n_chunks, bsz)
    y = y[:, :rows].reshape(S5_GROUPS, bsz, n_chunks, S5_T, S5_GROUP)
    y = jnp.transpose(y, (1, 2, 3, 0, 4)).reshape(bsz, n, S5_WIDTH)
    hfin = hfin[:, :rows].reshape(S5_GROUPS, bsz, n_chunks, 4 * S5_STATE)
    st2 = 2 * S5_STATE
    h_end = jnp.concatenate([hfin[:, :, n_chunks - 1, :st2], hfin[:, :, 0, st2:]], axis=-1)
    return y, h_end


def s5_glu(y, p):
    y = jax.nn.gelu(y)
    y = y * jax.nn.sigmoid(mm3(y, p['s5_glu_w'], "s5_glu") + p['s5_glu_b'])
    return rms_norm(y, p['s5_norm_g'])


def s5_mixer(u, uc, p, need_ctx_out):
    bsz = u.shape[0]
    tables = s5_tables(p)
    yc, h_ctx = s5_sequence(uc, tables, jnp.zeros((S5_GROUPS, bsz, 4 * S5_STATE), F32))
    y, _ = s5_sequence(u, tables, h_ctx)
    out = s5_glu(u * p['s5_d'] + y, p)
    out_c = s5_glu(uc * p['s5_d'] + yc, p) if need_ctx_out else None
    return out, out_c


def dn_features(z_qkv, conv_w):
    bsz, seq_len, _ = z_qkv.shape
    q, k, v = jnp.split(jax.nn.silu(depthwise_conv(z_qkv, conv_w)), 3, axis=-1)
    shp = (bsz, seq_len, DN_HEADS, DN_HEAD_DIM)
    return l2_normalize(q.reshape(shp)) * DN_HEAD_DIM ** -0.5, l2_normalize(k.reshape(shp)), v.reshape(shp)


def dn_gates(z_a, z_b, a_log, dt_bias):
    bsz, seq_len, _ = z_a.shape
    shp = (bsz, seq_len, 2, DN_HEADS)
    g = -jnp.exp(a_log) * jax.nn.softplus(z_a.reshape(shp) + dt_bias)
    beta = jax.nn.sigmoid(z_b.reshape(shp))
    return g, beta


def _dn_kernel(q_ref, k_ref, v_ref, g_ref, gt_ref, b_ref, s0_ref, o_ref, sfin_ref, s_ref, *, n_chunks):
    cs, hd, hg = DN_CHUNK, DN_HEAD_DIM, DN_HEAD_GROUP
    rows = cs * hg
    n_groups = DN_HEADS // hg
    groups = range(n_groups)
    step = pl.program_id(1)

    @pl.when(step == 0)
    def _():
        s_ref[...] = s0_ref[0]

    r_i = lax.broadcasted_iota(jnp.int32, (rows, rows), 0)
    c_i = lax.broadcasted_iota(jnp.int32, (rows, rows), 1)
    same_head = (r_i // cs) == (c_i // cs)
    incl = same_head & ((r_i % cs) >= (c_i % cs))
    strict = same_head & ((r_i % cs) > (c_i % cs))
    upto = same_head & ((r_i % cs) <= (c_i % cs))
    r_c = lax.broadcasted_iota(jnp.int32, (cs, cs), 0)
    c_c = lax.broadcasted_iota(jnp.int32, (cs, cs), 1)
    hp = lax.Precision.HIGHEST
    beta = b_ref[0, 0]
    gc = jnp.dot((r_c >= c_c).astype(F32), g_ref[0, 0], preferred_element_type=F32, precision=hp)
    gc_row = jnp.dot(gt_ref[0, 0], upto.astype(F32), preferred_element_type=F32, precision=hp)
    g_tot = gc[cs - 1:cs, :]
    e_g = jnp.exp(gc)
    e_tail = jnp.exp(g_tot - gc)
    e_tot = jnp.exp(g_tot)

    def head_cols(h):
        return slice(h * hd, (h + 1) * hd)

    def stack_heads(grp, fn):
        return jnp.concatenate([fn(grp * hg + j) for j in range(hg)], axis=0)

    def col(x, h, width):
        return jnp.broadcast_to(x[:, h:h + 1], (cs, width))

    def mm(a, b):
        return jnp.dot(a.astype(BF16), b.astype(BF16), preferred_element_type=F32)

    def mm_split_rhs(a_bf16, b):
        hi = b.astype(BF16)
        lo = (b - hi.astype(F32)).astype(BF16)
        t = jnp.dot(a_bf16, jnp.concatenate([hi, lo], axis=1), preferred_element_type=F32)
        return t[:, :b.shape[1]] + t[:, b.shape[1]:]

    kst = [stack_heads(gr, lambda h: k_ref[0, :, head_cols(h)]) for gr in groups]
    qst = [stack_heads(gr, lambda h: q_ref[0, :, head_cols(h)]) for gr in groups]
    b_col = [stack_heads(gr, lambda h: col(beta, h, hd)) for gr in groups]
    eg_col = [stack_heads(gr, lambda h: col(e_g, h, hd)) for gr in groups]
    kb = [kst[gr] * b_col[gr] for gr in groups]
    decay = []
    for gr in groups:
        diff = stack_heads(gr, lambda h: col(gc, h, rows)) - gc_row[gr:gr + 1, :]
        decay.append(jnp.where(incl, jnp.exp(jnp.where(incl, diff, 0.0)), 0.0))
    kq = [lax.dot_general(jnp.concatenate([kb[gr], qst[gr]], axis=0).astype(BF16), kst[gr].astype(BF16),
                          (((1,), (1,)), ((), ())), preferred_element_type=F32) for gr in groups]
    n = [jnp.where(strict, kq[gr][:rows] * decay[gr], 0.0).astype(BF16) for gr in groups]
    qk = [(kq[gr][rows:] * decay[gr]).astype(BF16) for gr in groups]
    x = [jnp.concatenate([stack_heads(gr, lambda h: v_ref[0, :, head_cols(h)]) * b_col[gr], kb[gr] * eg_col[gr]], axis=1) for gr in groups]
    x = [x[gr] - mm_split_rhs(n[gr], x[gr]) for gr in groups]
    pw = n
    for _ in range(5):
        pw = [jnp.dot(pw[gr], pw[gr], preferred_element_type=F32).astype(BF16) for gr in groups]
        x = [x[gr] + mm_split_rhs(pw[gr], x[gr]) for gr in groups]
    qd = [qst[gr] * eg_col[gr] for gr in groups]
    for gr in groups:
        v_new, o_state = [], []
        for j in range(hg):
            h, rs = gr * hg + j, slice(j * cs, (j + 1) * cs)
            ws = mm(jnp.concatenate([x[gr][rs, hd:], qd[gr][rs]], axis=0), s_ref[h])
            v_new.append(x[gr][rs, :hd] - ws[:cs])
            o_state.append(ws[cs:])
        v_all = jnp.concatenate(v_new, axis=0).astype(BF16)
        o_all = jnp.concatenate(o_state, axis=0) + jnp.dot(qk[gr], v_all, preferred_element_type=F32)
        for j in range(hg):
            h, rs = gr * hg + j, slice(j * cs, (j + 1) * cs)
            o_ref[0, :, head_cols(h)] = o_all[rs]
            kt = (kst[gr][rs] * col(e_tail, h, hd)).astype(BF16)
            s_ref[h] = s_ref[h] * e_tot[:, h:h + 1] + lax.dot_general(kt, v_all[rs], (((0,), (0,)), ((), ())), preferred_element_type=F32)

    @pl.when(step == n_chunks - 1)
    def _():
        sfin_ref[0] = s_ref[...]


def gated_delta_rule(q, k, v, g, beta, s0):
    bsz, seq_len, heads, hd = q.shape
    width = heads * hd
    n_chunks = seq_len // DN_CHUNK
    q, k, v = (t.reshape(bsz, seq_len, width) for t in (q, k, v))
    g4 = g.reshape(bsz, n_chunks, DN_CHUNK, heads)
    gt4 = jnp.swapaxes(g4, 2, 3).reshape(bsz, n_chunks, heads // DN_HEAD_GROUP, DN_HEAD_GROUP * DN_CHUNK)
    b4 = beta.reshape(bsz, n_chunks, DN_CHUNK, heads)
    tok_spec = pl.BlockSpec((1, DN_CHUNK, width), lambda b, i: (b, i, 0))
    gate_spec = pl.BlockSpec((1, 1, DN_CHUNK, heads), lambda b, i: (b, i, 0, 0))
    gate_t_spec = pl.BlockSpec((1, 1, heads // DN_HEAD_GROUP, DN_HEAD_GROUP * DN_CHUNK), lambda b, i: (b, i, 0, 0))
    state_spec = pl.BlockSpec((1, heads, hd, hd), lambda b, i: (b, 0, 0, 0))
    o, s_fin = pl.pallas_call(
        functools.partial(_dn_kernel, n_chunks=n_chunks),
        grid=(bsz, n_chunks),
        in_specs=[tok_spec, tok_spec, tok_spec, gate_spec, gate_t_spec, gate_spec, state_spec],
        out_specs=[tok_spec, state_spec],
        out_shape=[jax.ShapeDtypeStruct((bsz, seq_len, width), F32), jax.ShapeDtypeStruct(s0.shape, F32)],
        scratch_shapes=[pltpu.VMEM((heads, hd, hd), F32)],
        compiler_params=pltpu.CompilerParams(dimension_semantics=("parallel", "arbitrary"), vmem_limit_bytes=VMEM_LIMIT),
        name="delta_rule",
    )(q, k, v, g4, gt4, b4, s0)
    return o.reshape(bsz, seq_len, heads, hd), s_fin


def dn_output(o, z_gate, p):
    bsz, seq_len = o.shape[:2]
    gate = jax.nn.silu(z_gate).reshape(bsz, seq_len, DN_HEADS, DN_HEAD_DIM)
    return (rms_norm(o, p['dn_norm_g']) * gate).reshape(bsz, seq_len, DN_WIDTH)


def deltanet_mixer(z, zc, p, need_ctx_out):
    q, k, v = dn_features(z['qkv'], p['dn_conv_w'])
    g, beta = dn_gates(z['a'], z['b'], p['dn_a_log'], p['dn_dt_bias'])
    qc, kc, vc = dn_features(zc['qkv'], p['dn_conv_w'])
    gc, betac = dn_gates(zc['a'], zc['b'], p['dn_a_log'], p['dn_dt_bias'])
    s0 = jnp.zeros((q.shape[0], DN_HEADS, DN_HEAD_DIM, DN_HEAD_DIM), F32)
    o, oc = 0.0, 0.0
    for d in range(2):
        rev = d == 1
        oc_d, s_ctx = gated_delta_rule(*(_flip(t, rev) for t in (qc, kc, vc, gc[:, :, d], betac[:, :, d])), s0)
        o_d, _ = gated_delta_rule(*(_flip(t, rev) for t in (q, k, v, g[:, :, d], beta[:, :, d])), s_ctx)
        o = o + _flip(o_d, rev)
        if need_ctx_out:
            oc = oc + _flip(oc_d, rev)
    y = dn_output(o, z['g'], p)
    yc = dn_output(oc, zc['g'], p) if need_ctx_out else None
    return y, yc


def hyena_filters(seq_len, p):
    hp = lax.Precision.HIGHEST
    t = jnp.linspace(0.0, 1.0, seq_len, dtype=F32)[:, None]
    w = (2.0 * math.pi / seq_len) * jnp.arange(seq_len, dtype=F32)[:, None]
    f = jnp.linspace(1e-4, HY_BANDS - 1, HY_BANDS, dtype=F32)[None, :]
    feats = jnp.concatenate([t, jnp.cos(w * f), -jnp.sin(w * f)], axis=-1)
    freq = p['hy_f_freq']
    h = jnp.sin(freq * (jnp.dot(feats, p['hy_f_w1'], precision=hp) + p['hy_f_b1']))
    h = jnp.sin(freq * (jnp.dot(h, p['hy_f_w2'], precision=hp) + p['hy_f_b2']))
    h = jnp.sin(freq * (jnp.dot(h, p['hy_f_w3'], precision=hp) + p['hy_f_b3']))
    h = jnp.dot(h, p['hy_f_w4'], precision=hp).reshape(seq_len, 2, HY_WIDTH)
    rates = jnp.linspace(math.log(HY_DECAY_TARGET) / HY_DECAY_LONG_PCT, math.log(HY_DECAY_TARGET) / HY_DECAY_SHORT_PCT, HY_WIDTH, dtype=F32)
    h = h * jnp.exp(-t * jnp.abs(rates))[:, None, :]
    return h[:, 0], h[:, 1]


def two_sided_long_conv(u, h_fwd, h_bwd):
    seq_len = u.shape[1]
    taps = jnp.concatenate([h_fwd[:1] + h_bwd[:1], h_fwd[1:], jnp.zeros_like(h_fwd[:1]), h_bwd[:0:-1]], axis=0)
    u_f = jnp.fft.rfft(u, n=2 * seq_len, axis=1)
    t_f = jnp.fft.rfft(taps, axis=0)
    return jnp.fft.irfft(u_f * t_f[None], n=2 * seq_len, axis=1)[:, :seq_len]


def hyena_mixer(zh, p):
    zs = depthwise_conv(zh, p['hy_conv_w']) + p['hy_conv_b']
    x0, x1, v = jnp.split(zs, 3, axis=-1)
    z = x1 * v
    h_fwd, h_bwd = hyena_filters(zh.shape[1], p)
    y = x0 * (two_sided_long_conv(z, h_fwd, h_bwd) + p['hy_bias'] * z)
    return rms_norm(y, p['hy_norm_g'])


def in_projection(h, p, full):
    bsz, n, k = h.shape
    hb = h.reshape(bsz * n, k).astype(BF16)
    out = {}
    out['s5'] = matmul(hb, p['w_in_s5'], name="in_s5").reshape(bsz, n, -1)
    out['qkv'] = matmul(hb, p['w_in_qkv'], name="in_qkv").reshape(bsz, n, -1)
    ab = matmul(hb, p['w_in_ab'], name="in_ab").reshape(bsz, n, -1)
    out['a'] = ab[..., :2 * DN_HEADS]
    out['b'] = ab[..., 2 * DN_HEADS:4 * DN_HEADS]
    if full:
        out['g'] = matmul(hb, p['w_in_g'], name="in_g").reshape(bsz, n, -1)
        out['hy'] = matmul(hb, p['w_in_hy'], name="in_hy").reshape(bsz, n, -1)
    return out


def token_mixer(h, hc, p, need_ctx_out):
    z = in_projection(h, p, True)
    zc = in_projection(hc, p, need_ctx_out)
    s5_y, s5_yc = s5_mixer(z['s5'], zc['s5'], p, need_ctx_out)
    dn_y, dn_yc = deltanet_mixer(z, zc, p, need_ctx_out)
    y = mm3(jnp.concatenate([s5_y, dn_y, hyena_mixer(z['hy'], p)], axis=-1).astype(BF16), p['w_out'], "out_proj")
    if not need_ctx_out:
        return y, None
    yc = mm3(jnp.concatenate([s5_yc, dn_yc, hyena_mixer(zc['hy'], p)], axis=-1).astype(BF16), p['w_out'], "out_proj_ctx")
    return y, yc


def ada_modulation(c_rows, w_ada, b_ada):
    rows = c_rows.shape[0]
    a = jnp.zeros((8, c_rows.shape[1]), F32).at[:rows].set(jax.nn.silu(c_rows))
    return matmul(a, w_ada, name="ada")[:rows] + b_ada


def trunk_layer(x, xc, c, c_ctx, p, last):
    bsz, n, dm = x.shape
    lc = xc.shape[1]
    ada = ada_modulation(jnp.concatenate([c, c_ctx[None]], axis=0), p['w_ada'], p['b_ada'])
    sh1, sc1, g1, sh2, sc2, g2 = jnp.split(ada[:bsz, None, :], 6, axis=-1)
    mc = jnp.split(ada[bsz], 6, axis=-1)
    y, yc = token_mixer(modulate(x, sh1, sc1), modulate(xc, mc[0], mc[1]), p, not last)
    x = layer_norm(DEEPNORM_ALPHA * x + g1 * y, p['ln1_g'], p['ln1_b'])
    if last:
        f = hier_moe(modulate(x, sh2, sc2).reshape(bsz * n, dm), p).reshape(bsz, n, dm)
        return layer_norm(DEEPNORM_ALPHA * x + g2 * f, p['ln2_g'], p['ln2_b']), None
    xc = layer_norm(DEEPNORM_ALPHA * xc + mc[2] * yc, p['ln1_g'], p['ln1_b'])
    tokens = jnp.concatenate([modulate(x, sh2, sc2).reshape(bsz * n, dm), modulate(xc, mc[3], mc[4]).reshape(bsz * lc, dm)], axis=0)
    f = hier_moe(tokens, p)
    x = layer_norm(DEEPNORM_ALPHA * x + g2 * f[:bsz * n].reshape(bsz, n, dm), p['ln2_g'], p['ln2_b'])
    xc = layer_norm(DEEPNORM_ALPHA * xc + mc[5] * f[bsz * n:].reshape(bsz, lc, dm), p['ln2_g'], p['ln2_b'])
    return x, xc


def kernel(x, c, ctx, c_ctx, w_ada, b_ada, w_in, s5_lam_re, s5_lam_im, s5_log_step, s5_b_re, s5_b_im, s5_c_re, s5_c_im, s5_d, s5_glu_w, s5_glu_b, s5_norm_g, dn_conv_w, dn_a_log, dn_dt_bias, dn_norm_g, hy_conv_w, hy_conv_b, hy_f_w1, hy_f_b1, hy_f_w2, hy_f_b2, hy_f_w3, hy_f_b3, hy_f_freq, hy_f_w4, hy_bias, hy_norm_g, w_out, ln1_g, ln1_b, ln2_g, ln2_b, moe_w_group, moe_b_group, moe_w_expert, moe_b_expert, moe_w_gate, moe_w_up, moe_w_down):
    xc = ctx
    for l in range(DEPTH):
        w_in_l = w_in[l]
        w_ab = jnp.zeros((D_MODEL, LANE), F32).at[:, :4 * DN_HEADS].set(w_in_l[:, COL_DN_A:STATE_COLS])
        w_router = jnp.zeros((D_MODEL, LANE), F32).at[:, :N_GROUPS].set(moe_w_group[l]).at[:, N_GROUPS:N_GROUPS + N_EXPERTS].set(moe_w_expert[l])
        p = {
            'w_ada': w_ada[l], 'b_ada': b_ada[l],
            'w_in_s5': w_in_l[:, COL_S5:COL_DN_QKV].astype(BF16),
            'w_in_qkv': w_in_l[:, COL_DN_QKV:COL_DN_A].astype(BF16),
            'w_in_ab': w_ab.astype(BF16),
            'w_in_g': w_in_l[:, COL_DN_G:COL_HY].astype(BF16),
            'w_in_hy': w_in_l[:, COL_HY:].astype(BF16),
            's5_lam_re': s5_lam_re[l], 's5_lam_im': s5_lam_im[l], 's5_log_step': s5_log_step[l],
            's5_b_re': s5_b_re[l], 's5_b_im': s5_b_im[l], 's5_c_re': s5_c_re[l], 's5_c_im': s5_c_im[l],
            's5_d': s5_d[l], 's5_glu_w': s5_glu_w[l].astype(BF16), 's5_glu_b': s5_glu_b[l], 's5_norm_g': s5_norm_g[l],
            'dn_conv_w': dn_conv_w[l], 'dn_a_log': dn_a_log[l], 'dn_dt_bias': dn_dt_bias[l], 'dn_norm_g': dn_norm_g[l],
            'hy_conv_w': hy_conv_w[l], 'hy_conv_b': hy_conv_b[l],
            'hy_f_w1': hy_f_w1[l], 'hy_f_b1': hy_f_b1[l], 'hy_f_w2': hy_f_w2[l], 'hy_f_b2': hy_f_b2[l],
            'hy_f_w3': hy_f_w3[l], 'hy_f_b3': hy_f_b3[l], 'hy_f_freq': hy_f_freq[l], 'hy_f_w4': hy_f_w4[l],
            'hy_bias': hy_bias[l], 'hy_norm_g': hy_norm_g[l], 'w_out': w_out[l].astype(BF16),
            'ln1_g': ln1_g[l], 'ln1_b': ln1_b[l], 'ln2_g': ln2_g[l], 'ln2_b': ln2_b[l],
            'moe_w_router': w_router, 'moe_b_group': moe_b_group[l], 'moe_b_expert': moe_b_expert[l],
            'moe_w_gate': cast_layer_experts(moe_w_gate, l), 'moe_w_up': cast_layer_experts(moe_w_up, l), 'moe_w_down': cast_layer_experts(moe_w_down, l),
        }
        x, xc = trunk_layer(x, xc, c, c_ctx, p, l == DEPTH - 1)
    return x
```

```python
import functools
import math

import jax
import jax.numpy as jnp
from jax import lax
from jax.experimental import pallas as pl
from jax.experimental.pallas import tpu as pltpu

D_MODEL = 4096
DEPTH = 2
GRID_W = 64

S5_WIDTH = D_MODEL // 4
S5_GROUP = 16
S5_GROUPS = S5_WIDTH // S5_GROUP
S5_STATE = 64

DN_WIDTH = D_MODEL // 2
DN_HEAD_DIM = 128
DN_HEADS = DN_WIDTH // DN_HEAD_DIM
DN_CHUNK = 64
SHORT_CONV = 3

HY_WIDTH = D_MODEL - S5_WIDTH - DN_WIDTH
HY_BANDS = 16
HY_EMB = 1 + 2 * HY_BANDS
HY_HIDDEN = 64
HY_DECAY_SHORT_PCT = 0.3
HY_DECAY_LONG_PCT = 1.5
HY_DECAY_TARGET = 1e-2

MIX_WIDTH = S5_WIDTH + DN_WIDTH + HY_WIDTH

COL_S5 = 0
COL_DN_QKV = COL_S5 + S5_WIDTH
COL_DN_A = COL_DN_QKV + 3 * DN_WIDTH
COL_DN_B = COL_DN_A + 2 * DN_HEADS
STATE_COLS = COL_DN_B + 2 * DN_HEADS
COL_DN_G = STATE_COLS
COL_HY = COL_DN_G + DN_WIDTH
IN_WIDTH = COL_HY + 3 * HY_WIDTH

N_GROUPS = 4
EXPERTS_PER_GROUP = 8
N_EXPERTS = N_GROUPS * EXPERTS_PER_GROUP
TOP_K = 2
D_EXPERT = 512

DEEPNORM_ALPHA = (2 * DEPTH) ** 0.25
LN_EPS = 1e-5
RMS_EPS = 1e-6
F32 = jnp.float32
BF16 = jnp.bfloat16

V7X_VMEM_BYTES = 64 * 1024 * 1024
VMEM_LIMIT = 52 * 1024 * 1024
LANE = 128

MOE_TILE = 256
DN_HEAD_GROUP = 4
S5_T = 64


def _mm_kernel(a_ref, b_ref, o_ref):
    o_ref[...] = jnp.dot(a_ref[...].astype(BF16), b_ref[...].astype(BF16), preferred_element_type=F32).astype(o_ref.dtype)


def _pick(n, prefs):
    for p in prefs:
        if n % p == 0:
            return p
    return n


def matmul(a, b, out_dtype=F32, name="matmul"):
    m, k = a.shape
    n = b.shape[1]
    tm = _pick(m, (512, 256, 128, 64, 32, 16, 8))
    tn = _pick(n, (1024, 512, 256, 128))
    return pl.pallas_call(
        _mm_kernel,
        grid=(m // tm, n // tn),
        in_specs=[pl.BlockSpec((tm, k), lambda i, j: (i, 0)), pl.BlockSpec((k, tn), lambda i, j: (0, j))],
        out_specs=pl.BlockSpec((tm, tn), lambda i, j: (i, j)),
        out_shape=jax.ShapeDtypeStruct((m, n), out_dtype),
        compiler_params=pltpu.CompilerParams(dimension_semantics=("parallel", "parallel"), vmem_limit_bytes=VMEM_LIMIT),
        name=name,
    )(a, b)


def _mm_f32_kernel(a_ref, b_ref, o_ref):
    o_ref[...] = jnp.dot(a_ref[...], b_ref[...], preferred_element_type=F32, precision=lax.Precision.HIGHEST)


def matmul_f32(a, b, name="matmul_f32"):
    m, k = a.shape
    n = b.shape[1]
    tm = _pick(m, (512, 256, 128, 64, 32, 16, 8))
    return pl.pallas_call(
        _mm_f32_kernel,
        grid=(m // tm,),
        in_specs=[pl.BlockSpec((tm, k), lambda i: (i, 0)), pl.BlockSpec((k, n), lambda i: (0, 0))],
        out_specs=pl.BlockSpec((tm, n), lambda i: (i, 0)),
        out_shape=jax.ShapeDtypeStruct((m, n), F32),
        compiler_params=pltpu.CompilerParams(dimension_semantics=("parallel",), vmem_limit_bytes=VMEM_LIMIT),
        name=name,
    )(a, b)


def _cast_kernel(x_ref, o_ref):
    o_ref[0] = x_ref[0, 0].astype(o_ref.dtype)


def cast_layer_experts(w, layer, dtype=BF16):
    _, n_exp, r, c = w.shape
    return pl.pallas_call(
        _cast_kernel,
        grid=(n_exp,),
        in_specs=[pl.BlockSpec((1, 1, r, c), lambda e: (layer, e, 0, 0))],
        out_specs=pl.BlockSpec((1, r, c), lambda e: (e, 0, 0)),
        out_shape=jax.ShapeDtypeStruct((n_exp, r, c), dtype),
        compiler_params=pltpu.CompilerParams(dimension_semantics=("parallel",), vmem_limit_bytes=VMEM_LIMIT),
        name="cast_experts",
    )(w)


def _moe_kernel(tile_expert_ref, n_used_ref, x_ref, w_ref, wg_ref, wu_ref, wd_ref, o_ref):
    i = pl.program_id(0)

    @pl.when(i < n_used_ref[0])
    def _():
        x = x_ref[...]
        g = jnp.dot(x, wg_ref[0], preferred_element_type=F32)
        u = jnp.dot(x, wu_ref[0], preferred_element_type=F32)
        h = (g * jax.nn.sigmoid(g)) * u * w_ref[...]
        o_ref[...] = jnp.dot(h.astype(BF16), wd_ref[0], preferred_element_type=F32)

    @pl.when(i >= n_used_ref[0])
    def _():
        o_ref[...] = jnp.zeros_like(o_ref)


def moe_experts(tile_expert, n_used, xs, row_w, wg, wu, wd):
    p_rows, dm = xs.shape
    n_tiles = p_rows // MOE_TILE
    grid_spec = pltpu.PrefetchScalarGridSpec(
        num_scalar_prefetch=2,
        grid=(n_tiles,),
        in_specs=[
            pl.BlockSpec((MOE_TILE, dm), lambda i, te, nu: (i, 0)),
            pl.BlockSpec((MOE_TILE, 1), lambda i, te, nu: (i, 0)),
            pl.BlockSpec((1, dm, D_EXPERT), lambda i, te, nu: (te[i], 0, 0)),
            pl.BlockSpec((1, dm, D_EXPERT), lambda i, te, nu: (te[i], 0, 0)),
            pl.BlockSpec((1, D_EXPERT, dm), lambda i, te, nu: (te[i], 0, 0)),
        ],
        out_specs=pl.BlockSpec((MOE_TILE, dm), lambda i, te, nu: (i, 0)),
    )
    return pl.pallas_call(
        _moe_kernel,
        grid_spec=grid_spec,
        out_shape=jax.ShapeDtypeStruct((p_rows, dm), F32),
        compiler_params=pltpu.CompilerParams(dimension_semantics=("arbitrary",), vmem_limit_bytes=VMEM_LIMIT),
        name="moe_experts",
    )(tile_expert, n_used, xs, row_w, wg, wu, wd)


def hier_moe(t, p):
    n_tok = t.shape[0]
    logits = matmul_f32(t, p['moe_w_router'], name="moe_router")
    group_logits = logits[:, :N_GROUPS] + p['moe_b_group']
    group = jnp.argmax(group_logits, axis=-1)
    group_w = jnp.max(jax.nn.softmax(group_logits, axis=-1), axis=-1, keepdims=True)
    exp_logits = (logits[:, N_GROUPS:N_GROUPS + N_EXPERTS] + p['moe_b_expert']).reshape(n_tok, N_GROUPS, EXPERTS_PER_GROUP)
    exp_logits = jnp.take_along_axis(exp_logits, group[:, None, None], axis=1)[:, 0]
    top_logits, top_idx = lax.top_k(exp_logits, TOP_K)
    top_w = jax.nn.softmax(top_logits, axis=-1) * group_w
    expert_id = (group[:, None] * EXPERTS_PER_GROUP + top_idx).astype(jnp.int32)

    n_rows = n_tok * TOP_K
    flat_e = expert_id.reshape(n_rows)
    flat_w = top_w.reshape(n_rows)
    flat_t = jnp.arange(n_rows, dtype=jnp.int32) // TOP_K
    order = jnp.argsort(flat_e, stable=True)
    sorted_e = flat_e[order]
    counts = jnp.zeros((N_EXPERTS,), jnp.int32).at[flat_e].add(1)
    padded = ((counts + MOE_TILE - 1) // MOE_TILE) * MOE_TILE
    pad_end = jnp.cumsum(padded)
    pad_start = pad_end - padded
    start = jnp.cumsum(counts) - counts
    dest = pad_start[sorted_e] + (jnp.arange(n_rows, dtype=jnp.int32) - start[sorted_e])
    p_rows = n_rows + N_EXPERTS * MOE_TILE
    n_tiles = p_rows // MOE_TILE
    row_token = jnp.zeros((p_rows,), jnp.int32).at[dest].set(flat_t[order])
    row_w = jnp.zeros((p_rows,), F32).at[dest].set(flat_w[order])
    pos = jnp.zeros((n_rows,), jnp.int32).at[order].set(dest).reshape(n_tok, TOP_K)
    n_used = (pad_end[-1] // MOE_TILE).astype(jnp.int32)
    tile_idx = jnp.minimum(jnp.arange(n_tiles, dtype=jnp.int32), n_used - 1)
    tile_expert = jnp.minimum(jnp.searchsorted(pad_end, tile_idx * MOE_TILE, side='right'), N_EXPERTS - 1).astype(jnp.int32)

    xs = t.astype(BF16)[row_token]
    ys = moe_experts(tile_expert, n_used.reshape(1), xs, row_w[:, None], p['moe_w_gate'], p['moe_w_up'], p['moe_w_down'])
    return ys[pos[:, 0]] + ys[pos[:, 1]]


def _flip(t, rev):
    return jnp.flip(t, axis=1) if rev else t


def layer_norm(x, g, b):
    mu = jnp.mean(x, axis=-1, keepdims=True)
    var = jnp.mean(jnp.square(x - mu), axis=-1, keepdims=True)
    return (x - mu) * lax.rsqrt(var + LN_EPS) * g + b


def rms_norm(x, g):
    return x * lax.rsqrt(jnp.mean(jnp.square(x), axis=-1, keepdims=True) + RMS_EPS) * g


def l2_normalize(t):
    return t * lax.rsqrt(jnp.sum(t * t, axis=-1, keepdims=True) + 1e-6)


def depthwise_conv(u, w):
    k = w.shape[0]
    return lax.conv_general_dilated(u, w[:, None, :], window_strides=(1,), padding=[(k // 2, k // 2)], dimension_numbers=('NWC', 'WIO', 'NWC'), feature_group_count=u.shape[-1])


def modulate(t, shift, scale):
    return t * (1 + scale) + shift


def mm3(t, w, name):
    bsz, n, k = t.shape
    return matmul(t.reshape(bsz * n, k), w, name=name).reshape(bsz, n, w.shape[1])


def s5_discretize(lam_re, lam_im, log_step, b_re, b_im):
    step = jnp.exp(log_step)[:, None]
    mag = jnp.exp(lam_re * step)
    a_re = mag * jnp.cos(lam_im * step)
    a_im = mag * jnp.sin(lam_im * step)
    den = lam_re * lam_re + lam_im * lam_im
    f_re = ((a_re - 1.0) * lam_re + a_im * lam_im) / den
    f_im = (a_im * lam_re - (a_re - 1.0) * lam_im) / den
    bb_re = f_re[..., None] * b_re - f_im[..., None] * b_im
    bb_im = f_re[..., None] * b_im + f_im[..., None] * b_re
    return a_re, a_im, bb_re, bb_im


def _cmul(x, y):
    return x[0] * y[0] - x[1] * y[1], x[0] * y[1] + x[1] * y[0]


def s5_tables(p):
    hp = lax.Precision.HIGHEST
    t_len, grp, st = S5_T, S5_GROUPS, S5_STATE
    taps, wst, rd, pq = [], [], [], []
    for d in range(2):
        a_re, a_im, bb_re, bb_im = s5_discretize(p['s5_lam_re'][d], p['s5_lam_im'][d], p['s5_log_step'][d], p['s5_b_re'][d], p['s5_b_im'][d])
        c_re, c_im = p['s5_c_re'][d], p['s5_c_im'][d]
        pr, pi = lax.associative_scan(_cmul, (jnp.broadcast_to(a_re, (t_len, grp, st)), jnp.broadcast_to(a_im, (t_len, grp, st))), axis=0)
        pw_re = jnp.concatenate([jnp.ones((1, grp, st), F32), pr], axis=0)
        pw_im = jnp.concatenate([jnp.zeros((1, grp, st), F32), pi], axis=0)
        ca_re = c_re[None] * pw_re[:, :, None, :] - c_im[None] * pw_im[:, :, None, :]
        ca_im = c_re[None] * pw_im[:, :, None, :] + c_im[None] * pw_re[:, :, None, :]
        taps.append(jnp.einsum('tgip,gpj->tgij', ca_re[:t_len], bb_re, precision=hp) - jnp.einsum('tgip,gpj->tgij', ca_im[:t_len], bb_im, precision=hp))
        e_st = jnp.arange(t_len - 1, -1, -1) if d == 0 else jnp.arange(t_len)
        w_re = pw_re[e_st][..., None] * bb_re[None] - pw_im[e_st][..., None] * bb_im[None]
        w_im = pw_re[e_st][..., None] * bb_im[None] + pw_im[e_st][..., None] * bb_re[None]
        to_rows = lambda w: jnp.transpose(w, (1, 0, 3, 2)).reshape(grp, t_len * S5_GROUP, st)
        wst += [to_rows(w_re), to_rows(w_im)]
        e_rd = jnp.arange(1, t_len + 1) if d == 0 else jnp.arange(t_len, 0, -1)
        to_cols = lambda r: jnp.transpose(r, (1, 3, 0, 2)).reshape(grp, st, t_len * S5_GROUP)
        rd += [to_cols(ca_re[e_rd]), to_cols(-ca_im[e_rd])]
        lv_re, lv_im = pw_re[t_len], pw_im[t_len]
        for _ in range(8):
            pq += [jnp.concatenate([lv_re, lv_re], axis=-1), jnp.concatenate([-lv_im, lv_im], axis=-1)]
            lv_re, lv_im = _cmul((lv_re, lv_im), (lv_re, lv_im))
    kf, kb = taps
    blocks = jnp.concatenate([kb[1:][::-1], (kf[0] + kb[0])[None], kf[1:], jnp.zeros_like(kf[:1])], axis=0)
    ext = jnp.transpose(blocks, (1, 3, 0, 2)).reshape(grp, S5_GROUP, 2 * t_len * S5_GROUP)
    return ext, jnp.concatenate(wst, axis=-1).astype(BF16), jnp.concatenate(rd, axis=1).astype(BF16), jnp.stack(pq, axis=1)


def _s5_kernel(u_ref, ext_ref, wst_ref, rd_ref, pq_ref, h0_ref, y_ref, hfin_ref, m_ref, *, n_chunks, bsz):
    cw = S5_T * S5_GROUP
    st2 = 2 * S5_STATE
    ext = ext_ref[0]
    for s in range(S5_T):
        off = S5_GROUP * (S5_T - 1 - s)
        shifted = ext if off == 0 else pltpu.roll(ext, shift=2 * cw - off, axis=1)
        m_ref[S5_GROUP * s:S5_GROUP * (s + 1), :] = shifted[:, :cw].astype(BF16)
    u = u_ref[0]
    rows = u.shape[0]
    y = jnp.dot(u, m_ref[...], preferred_element_type=F32)
    s_all = jnp.dot(u, wst_ref[0], preferred_element_type=F32)
    sf, sb = s_all[:, :st2], s_all[:, st2:]
    row = lax.broadcasted_iota(jnp.int32, (rows, st2), 0)
    c_idx = row % n_chunks

    def cmul_rows(idx, x):
        return pq_ref[0, idx:idx + 1, :] * x + pq_ref[0, idx + 1:idx + 2, :] * pltpu.roll(x, shift=S5_STATE, axis=1)

    h0 = h0_ref[0]
    h0f = jnp.zeros((rows, st2), F32)
    h0b = jnp.zeros((rows, st2), F32)
    for b in range(bsz):
        in_b = (row >= b * n_chunks) & (row < (b + 1) * n_chunks)
        h0f = jnp.where(in_b, h0[b:b + 1, :st2], h0f)
        h0b = jnp.where(in_b, h0[b:b + 1, st2:], h0b)
    hf = jnp.where(c_idx == 0, h0f, pltpu.roll(sf, shift=1, axis=0))
    hb = jnp.where(c_idx == n_chunks - 1, h0b, pltpu.roll(sb, shift=rows - 1, axis=0))
    level, sh = 0, 1
    while sh < n_chunks:
        dn = jnp.where(c_idx >= sh, pltpu.roll(hf, shift=sh, axis=0), 0.0)
        up = jnp.where(c_idx < n_chunks - sh, pltpu.roll(hb, shift=rows - sh, axis=0), 0.0)
        hf = hf + cmul_rows(2 * level, dn)
        hb = hb + cmul_rows(16 + 2 * level, up)
        level, sh = level + 1, sh * 2
    h_in = jnp.concatenate([hf, hb], axis=1)
    y_ref[0] = y + jnp.dot(h_in.astype(BF16), rd_ref[0], preferred_element_type=F32)
    hfin_ref[0] = jnp.concatenate([cmul_rows(0, hf) + sf, cmul_rows(16, hb) + sb], axis=1)


def s5_scan(u, tables, h0, n_chunks, bsz):
    ext, wst, rd, pq = tables
    grp, rows, cw = u.shape
    st4 = 4 * S5_STATE
    blk = lambda *shape: pl.BlockSpec((1,) + shape, lambda i: (i, 0, 0))
    return pl.pallas_call(
        functools.partial(_s5_kernel, n_chunks=n_chunks, bsz=bsz),
        grid=(grp,),
        in_specs=[blk(rows, cw), blk(S5_GROUP, 2 * cw), blk(cw, st4), blk(st4, cw), blk(32, 2 * S5_STATE), blk(bsz, st4)],
        out_specs=[blk(rows, cw), blk(rows, st4)],
        out_shape=[jax.ShapeDtypeStruct((grp, rows, cw), F32), jax.ShapeDtypeStruct((grp, rows, st4), F32)],
        scratch_shapes=[pltpu.VMEM((cw, cw), BF16)],
        compiler_params=pltpu.CompilerParams(dimension_semantics=("parallel",), vmem_limit_bytes=VMEM_LIMIT),
        name="s5_scan",
    )(u, ext, wst, rd, pq, h0)


def s5_sequence(u, tables, h0):
    bsz, n, _ = u.shape
    n_chunks = n // S5_T
    rows = bsz * n_chunks
    rows_pad = -(-rows // 16) * 16
    ug = u.astype(BF16).reshape(bsz, n_chunks, S5_T, S5_GROUPS, S5_GROUP)
    ug = jnp.transpose(ug, (3, 0, 1, 2, 4)).reshape(S5_GROUPS, rows, S5_T * S5_GROUP)
    if rows_pad != rows:
        ug = jnp.pad(ug, ((0, 0), (0, rows_pad - rows), (0, 0)))
    y, hfin = s5_scan(ug, tables, h0, n_chunks, bsz)
    y = y[:, :rows].reshape(S5_GROUPS, bsz, n_chunks, S5_T, S5_GROUP)
    y = jnp.transpose(y, (1, 2, 3, 0, 4)).reshape(bsz, n, S5_WIDTH)
    hfin = hfin[:, :rows].reshape(S5_GROUPS, bsz, n_chunks, 4 * S5_STATE)
    st2 = 2 * S5_STATE
    h_end = jnp.concatenate([hfin[:, :, n_chunks - 1, :st2], hfin[:, :, 0, st2:]], axis=-1)
    return y, h_end


def s5_glu(y, p):
    y = jax.nn.gelu(y)
    y = y * jax.nn.sigmoid(mm3(y, p['s5_glu_w'], "s5_glu") + p['s5_glu_b'])
    return rms_norm(y, p['s5_norm_g'])


def s5_mixer(u, uc, p, need_ctx_out):
    bsz = u.shape[0]
    tables = s5_tables(p)
    yc, h_ctx = s5_sequence(uc, tables, jnp.zeros((S5_GROUPS, bsz, 4 * S5_STATE), F32))
    y, _ = s5_sequence(u, tables, h_ctx)
    out = s5_glu(u * p['s5_d'] + y, p)
    out_c = s5_glu(uc * p['s5_d'] + yc, p) if need_ctx_out else None
    return out, out_c


def dn_features(z_qkv, conv_w):
    bsz, seq_len, _ = z_qkv.shape
    q, k, v = jnp.split(jax.nn.silu(depthwise_conv(z_qkv, conv_w)), 3, axis=-1)
    shp = (bsz, seq_len, DN_HEADS, DN_HEAD_DIM)
    return l2_normalize(q.reshape(shp)) * DN_HEAD_DIM ** -0.5, l2_normalize(k.reshape(shp)), v.reshape(shp)


def dn_gates(z_a, z_b, a_log, dt_bias):
    bsz, seq_len, _ = z_a.shape
    shp = (bsz, seq_len, 2, DN_HEADS)
    g = -jnp.exp(a_log) * jax.nn.softplus(z_a.reshape(shp) + dt_bias)
    beta = jax.nn.sigmoid(z_b.reshape(shp))
    return g, beta


def _dn_kernel(q_ref, k_ref, v_ref, g_ref, gt_ref, b_ref, s0_ref, o_ref, sfin_ref, s_ref, *, n_chunks):
    cs, hd, hg = DN_CHUNK, DN_HEAD_DIM, DN_HEAD_GROUP
    rows = cs * hg
    n_groups = DN_HEADS // hg
    groups = range(n_groups)
    step = pl.program_id(1)

    @pl.when(step == 0)
    def _():
        s_ref[...] = s0_ref[0]

    r_i = lax.broadcasted_iota(jnp.int32, (rows, rows), 0)
    c_i = lax.broadcasted_iota(jnp.int32, (rows, rows), 1)
    same_head = (r_i // cs) == (c_i // cs)
    incl = same_head & ((r_i % cs) >= (c_i % cs))
    strict = same_head & ((r_i % cs) > (c_i % cs))
    upto = same_head & ((r_i % cs) <= (c_i % cs))
    r_c = lax.broadcasted_iota(jnp.int32, (cs, cs), 0)
    c_c = lax.broadcasted_iota(jnp.int32, (cs, cs), 1)
    hp = lax.Precision.HIGHEST
    beta = b_ref[0, 0]
    gc = jnp.dot((r_c >= c_c).astype(F32), g_ref[0, 0], preferred_element_type=F32, precision=hp)
    gc_row = jnp.dot(gt_ref[0, 0], upto.astype(F32), preferred_element_type=F32, precision=hp)
    g_tot = gc[cs - 1:cs, :]
    e_g = jnp.exp(gc)
    e_tail = jnp.exp(g_tot - gc)
    e_tot = jnp.exp(g_tot)

    def head_cols(h):
        return slice(h * hd, (h + 1) * hd)

    def stack_heads(grp, fn):
        return jnp.concatenate([fn(grp * hg + j) for j in range(hg)], axis=0)

    def col(x, h, width):
        return jnp.broadcast_to(x[:, h:h + 1], (cs, width))

    def mm(a, b):
        return jnp.dot(a.astype(BF16), b.astype(BF16), preferred_element_type=F32)

    def mm_split_rhs(a_bf16, b):
        hi = b.astype(BF16)
        lo = (b - hi.astype(F32)).astype(BF16)
        t = jnp.dot(a_bf16, jnp.concatenate([hi, lo], axis=1), preferred_element_type=F32)
        return t[:, :b.shape[1]] + t[:, b.shape[1]:]

    kst = [stack_heads(gr, lambda h: k_ref[0, :, head_cols(h)]) for gr in groups]
    qst = [stack_heads(gr, lambda h: q_ref[0, :, head_cols(h)]) for gr in groups]
    b_col = [stack_heads(gr, lambda h: col(beta, h, hd)) for gr in groups]
    eg_col = [stack_heads(gr, lambda h: col(e_g, h, hd)) for gr in groups]
    kb = [kst[gr] * b_col[gr] for gr in groups]
    decay = []
    for gr in groups:
        diff = stack_heads(gr, lambda h: col(gc, h, rows)) - gc_row[gr:gr + 1, :]
        decay.append(jnp.where(incl, jnp.exp(jnp.where(incl, diff, 0.0)), 0.0))
    kq = [lax.dot_general(jnp.concatenate([kb[gr], qst[gr]], axis=0).astype(BF16), kst[gr].astype(BF16),
                          (((1,), (1,)), ((), ())), preferred_element_type=F32) for gr in groups]
    n = [jnp.where(strict, kq[gr][:rows] * decay[gr], 0.0).astype(BF16) for gr in groups]
    qk = [(kq[gr][rows:] * decay[gr]).astype(BF16) for gr in groups]
    x = [jnp.concatenate([stack_heads(gr, lambda h: v_ref[0, :, head_cols(h)]) * b_col[gr], kb[gr] * eg_col[gr]], axis=1) for gr in groups]
    x = [x[gr] - mm_split_rhs(n[gr], x[gr]) for gr in groups]
    pw = n
    for _ in range(5):
        pw = [jnp.dot(pw[gr], pw[gr], preferred_element_type=F32).astype(BF16) for gr in groups]
        x = [x[gr] + mm_split_rhs(pw[gr], x[gr]) for gr in groups]
    qd = [qst[gr] * eg_col[gr] for gr in groups]
    for gr in groups:
        v_new, o_state = [], []
        for j in range(hg):
            h, rs = gr * hg + j, slice(j * cs, (j + 1) * cs)
            ws = mm(jnp.concatenate([x[gr][rs, hd:], qd[gr][rs]], axis=0), s_ref[h])
            v_new.append(x[gr][rs, :hd] - ws[:cs])
            o_state.append(ws[cs:])
        v_all = jnp.concatenate(v_new, axis=0).astype(BF16)
        o_all = jnp.concatenate(o_state, axis=0) + jnp.dot(qk[gr], v_all, preferred_element_type=F32)
        for j in range(hg):
            h, rs = gr * hg + j, slice(j * cs, (j + 1) * cs)
            o_ref[0, :, head_cols(h)] = o_all[rs]
            kt = (kst[gr][rs] * col(e_tail, h, hd)).astype(BF16)
            s_ref[h] = s_ref[h] * e_tot[:, h:h + 1] + lax.dot_general(kt, v_all[rs], (((0,), (0,)), ((), ())), preferred_element_type=F32)

    @pl.when(step == n_chunks - 1)
    def _():
        sfin_ref[0] = s_ref[...]


def gated_delta_rule(q, k, v, g, beta, s0):
    bsz, seq_len, heads, hd = q.shape
    width = heads * hd
    n_chunks = seq_len // DN_CHUNK
    q, k, v = (t.reshape(bsz, seq_len, width) for t in (q, k, v))
    g4 = g.reshape(bsz, n_chunks, DN_CHUNK, heads)
    gt4 = jnp.swapaxes(g4, 2, 3).reshape(bsz, n_chunks, heads // DN_HEAD_GROUP, DN_HEAD_GROUP * DN_CHUNK)
    b4 = beta.reshape(bsz, n_chunks, DN_CHUNK, heads)
    tok_spec = pl.BlockSpec((1, DN_CHUNK, width), lambda b, i: (b, i, 0))
    gate_spec = pl.BlockSpec((1, 1, DN_CHUNK, heads), lambda b, i: (b, i, 0, 0))
    gate_t_spec = pl.BlockSpec((1, 1, heads // DN_HEAD_GROUP, DN_HEAD_GROUP * DN_CHUNK), lambda b, i: (b, i, 0, 0))
    state_spec = pl.BlockSpec((1, heads, hd, hd), lambda b, i: (b, 0, 0, 0))
    o, s_fin = pl.pallas_call(
        functools.partial(_dn_kernel, n_chunks=n_chunks),
        grid=(bsz, n_chunks),
        in_specs=[tok_spec, tok_spec, tok_spec, gate_spec, gate_t_spec, gate_spec, state_spec],
        out_specs=[tok_spec, state_spec],
        out_shape=[jax.ShapeDtypeStruct((bsz, seq_len, width), F32), jax.ShapeDtypeStruct(s0.shape, F32)],
        scratch_shapes=[pltpu.VMEM((heads, hd, hd), F32)],
        compiler_params=pltpu.CompilerParams(dimension_semantics=("parallel", "arbitrary"), vmem_limit_bytes=VMEM_LIMIT),
        name="delta_rule",
    )(q, k, v, g4, gt4, b4, s0)
    return o.reshape(bsz, seq_len, heads, hd), s_fin


def dn_output(o, z_gate, p):
    bsz, seq_len = o.shape[:2]
    gate = jax.nn.silu(z_gate).reshape(bsz, seq_len, DN_HEADS, DN_HEAD_DIM)
    return (rms_norm(o, p['dn_norm_g']) * gate).reshape(bsz, seq_len, DN_WIDTH)


def deltanet_mixer(z, zc, p, need_ctx_out):
    q, k, v = dn_features(z['qkv'], p['dn_conv_w'])
    g, beta = dn_gates(z['a'], z['b'], p['dn_a_log'], p['dn_dt_bias'])
    qc, kc, vc = dn_features(zc['qkv'], p['dn_conv_w'])
    gc, betac = dn_gates(zc['a'], zc['b'], p['dn_a_log'], p['dn_dt_bias'])
    s0 = jnp.zeros((q.shape[0], DN_HEADS, DN_HEAD_DIM, DN_HEAD_DIM), F32)
    o, oc = 0.0, 0.0
    for d in range(2):
        rev = d == 1
        oc_d, s_ctx = gated_delta_rule(*(_flip(t, rev) for t in (qc, kc, vc, gc[:, :, d], betac[:, :, d])), s0)
        o_d, _ = gated_delta_rule(*(_flip(t, rev) for t in (q, k, v, g[:, :, d], beta[:, :, d])), s_ctx)
        o = o + _flip(o_d, rev)
        if need_ctx_out:
            oc = oc + _flip(oc_d, rev)
    y = dn_output(o, z['g'], p)
    yc = dn_output(oc, zc['g'], p) if need_ctx_out else None
    return y, yc


def hyena_filters(seq_len, p):
    hp = lax.Precision.HIGHEST
    t = jnp.linspace(0.0, 1.0, seq_len, dtype=F32)[:, None]
    w = (2.0 * math.pi / seq_len) * jnp.arange(seq_len, dtype=F32)[:, None]
    f = jnp.linspace(1e-4, HY_BANDS - 1, HY_BANDS, dtype=F32)[None, :]
    feats = jnp.concatenate([t, jnp.cos(w * f), -jnp.sin(w * f)], axis=-1)
    freq = p['hy_f_freq']
    h = jnp.sin(freq * (jnp.dot(feats, p['hy_f_w1'], precision=hp) + p['hy_f_b1']))
    h = jnp.sin(freq * (jnp.dot(h, p['hy_f_w2'], precision=hp) + p['hy_f_b2']))
    h = jnp.sin(freq * (jnp.dot(h, p['hy_f_w3'], precision=hp) + p['hy_f_b3']))
    h = jnp.dot(h, p['hy_f_w4'], precision=hp).reshape(seq_len, 2, HY_WIDTH)
    rates = jnp.linspace(math.log(HY_DECAY_TARGET) / HY_DECAY_LONG_PCT, math.log(HY_DECAY_TARGET) / HY_DECAY_SHORT_PCT, HY_WIDTH, dtype=F32)
    h = h * jnp.exp(-t * jnp.abs(rates))[:, None, :]
    return h[:, 0], h[:, 1]


def two_sided_long_conv(u, h_fwd, h_bwd):
    seq_len = u.shape[1]
    taps = jnp.concatenate([h_fwd[:1] + h_bwd[:1], h_fwd[1:], jnp.zeros_like(h_fwd[:1]), h_bwd[:0:-1]], axis=0)
    u_f = jnp.fft.rfft(u, n=2 * seq_len, axis=1)
    t_f = jnp.fft.rfft(taps, axis=0)
    return jnp.fft.irfft(u_f * t_f[None], n=2 * seq_len, axis=1)[:, :seq_len]


FFT_R = 128
FFT_COLS = 8192


def _dft_tables():
    n, big = FFT_R, FFT_R * FFT_R
    idx = jnp.arange(n, dtype=jnp.int32)
    ang = (2.0 * math.pi / n) * ((idx[:, None] * idx[None, :]) % n).astype(F32)
    m = (idx[None, None, :] * (n * idx[None, :, None] + idx[:, None, None])) % big
    g_ang = (2.0 * math.pi / big) * m.astype(F32)
    return jnp.cos(ang), -jnp.sin(ang), jnp.cos(g_ang), -jnp.sin(g_ang)


def _hdot(a, b):
    return jnp.dot(a, b, preferred_element_type=F32, precision=lax.Precision.HIGHEST)


def _fft_rows_kernel(f_re_ref, f_im_ref, x_ref, a_re_ref, a_im_ref):
    x = x_ref[0]
    a_re_ref[0] = _hdot(f_re_ref[...], x)
    a_im_ref[0] = _hdot(f_im_ref[...], x)


def fft_rows(x, f_re, f_im):
    bsz, k, m = x.shape
    out = jax.ShapeDtypeStruct((bsz, FFT_R, m), F32)
    f_spec = pl.BlockSpec((FFT_R, k), lambda b, j: (0, 0))
    o_spec = pl.BlockSpec((1, FFT_R, FFT_COLS), lambda b, j: (b, 0, j))
    return pl.pallas_call(
        _fft_rows_kernel,
        grid=(bsz, m // FFT_COLS),
        in_specs=[f_spec, f_spec, pl.BlockSpec((1, k, FFT_COLS), lambda b, j: (b, 0, j))],
        out_specs=[o_spec, o_spec],
        out_shape=[out, out],
        compiler_params=pltpu.CompilerParams(dimension_semantics=("parallel", "parallel"), vmem_limit_bytes=VMEM_LIMIT),
        name="fft_rows",
    )(f_re, f_im, x)


def _fft_spectrum_kernel(g_re_ref, g_im_ref, a_re_ref, a_im_ref, x_re_ref, x_im_ref):
    gr, gi, ar, ai = g_re_ref[0], g_im_ref[0], a_re_ref[0, 0], a_im_ref[0, 0]
    x_re_ref[0, 0] = _hdot(gr, ar) - _hdot(gi, ai)
    x_im_ref[0, 0] = _hdot(gr, ai) + _hdot(gi, ar)


def _fft_mid_kernel(g_re_ref, g_im_ref, gt_re_ref, gt_im_ref, a_re_ref, a_im_ref, h_re_ref, h_im_ref, b_re_ref, b_im_ref):
    gr, gi, ar, ai = g_re_ref[0], g_im_ref[0], a_re_ref[0, 0], a_im_ref[0, 0]
    xr = _hdot(gr, ar) - _hdot(gi, ai)
    xi = _hdot(gr, ai) + _hdot(gi, ar)
    hr, hi = h_re_ref[0, 0], h_im_ref[0, 0]
    yr = xr * hr - xi * hi
    yi = xr * hi + xi * hr
    tr, ti = gt_re_ref[0], gt_im_ref[0]
    b_re_ref[0, 0] = _hdot(tr, yr) + _hdot(ti, yi)
    b_im_ref[0, 0] = _hdot(tr, yi) - _hdot(ti, yr)


def fft_spectrum(a_re, a_im, g_re, g_im):
    bsz, r, _, c = a_re.shape
    g_spec = pl.BlockSpec((1, r, r), lambda k, b: (k, 0, 0))
    a_spec = pl.BlockSpec((1, 1, r, c), lambda k, b: (b, k, 0, 0))
    out = jax.ShapeDtypeStruct(a_re.shape, F32)
    return pl.pallas_call(
        _fft_spectrum_kernel,
        grid=(r, bsz),
        in_specs=[g_spec, g_spec, a_spec, a_spec],
        out_specs=[a_spec, a_spec],
        out_shape=[out, out],
        compiler_params=pltpu.CompilerParams(dimension_semantics=("parallel", "parallel"), vmem_limit_bytes=VMEM_LIMIT),
        name="fft_spectrum",
    )(g_re, g_im, a_re, a_im)


def fft_filter_mid(a_re, a_im, h_re, h_im, g_re, g_im):
    bsz, r, _, c = a_re.shape
    gt_re, gt_im = jnp.swapaxes(g_re, 1, 2), jnp.swapaxes(g_im, 1, 2)
    g_spec = pl.BlockSpec((1, r, r), lambda k, b: (k, 0, 0))
    a_spec = pl.BlockSpec((1, 1, r, c), lambda k, b: (b, k, 0, 0))
    h_spec = pl.BlockSpec((1, 1, r, c), lambda k, b: (0, k, 0, 0))
    out = jax.ShapeDtypeStruct(a_re.shape, F32)
    return pl.pallas_call(
        _fft_mid_kernel,
        grid=(r, bsz),
        in_specs=[g_spec, g_spec, g_spec, g_spec, a_spec, a_spec, h_spec, h_spec],
        out_specs=[a_spec, a_spec],
        out_shape=[out, out],
        compiler_params=pltpu.CompilerParams(dimension_semantics=("parallel", "parallel"), vmem_limit_bytes=VMEM_LIMIT),
        name="fft_filter_mid",
    )(g_re, g_im, gt_re, gt_im, a_re, a_im, h_re, h_im)


def _ifft_rows_kernel(c_ref, s_ref, b_re_ref, b_im_ref, y_ref):
    y_ref[0] = (_hdot(c_ref[...], b_re_ref[0]) + _hdot(s_ref[...], b_im_ref[0])) * (1.0 / (FFT_R * FFT_R))


def ifft_rows(b_re, b_im, f_re, f_im, rows):
    bsz, r, m = b_re.shape
    f_spec = pl.BlockSpec((rows, r), lambda b, j: (0, 0))
    b_spec = pl.BlockSpec((1, r, FFT_COLS), lambda b, j: (b, 0, j))
    return pl.pallas_call(
        _ifft_rows_kernel,
        grid=(bsz, m // FFT_COLS),
        in_specs=[f_spec, f_spec, b_spec, b_spec],
        out_specs=pl.BlockSpec((1, rows, FFT_COLS), lambda b, j: (b, 0, j)),
        out_shape=jax.ShapeDtypeStruct((bsz, rows, m), F32),
        compiler_params=pltpu.CompilerParams(dimension_semantics=("parallel", "parallel"), vmem_limit_bytes=VMEM_LIMIT),
        name="ifft_rows",
    )(f_re[:rows], f_im[:rows], b_re, b_im)


def long_conv_fft(u, taps):
    bsz, seq_len, c = u.shape
    r = FFT_R
    f_re, f_im, g_re, g_im = _dft_tables()
    t_re, t_im = fft_rows(taps.reshape(1, r, r * c), f_re, f_im)
    h_re, h_im = fft_spectrum(t_re.reshape(1, r, r, c), t_im.reshape(1, r, r, c), g_re, g_im)
    half = seq_len // r
    a_re, a_im = fft_rows(u.reshape(bsz, half, r * c), f_re[:, :half], f_im[:, :half])
    b_re, b_im = fft_filter_mid(a_re.reshape(bsz, r, r, c), a_im.reshape(bsz, r, r, c), h_re, h_im, g_re, g_im)
    y = ifft_rows(b_re.reshape(bsz, r, r * c), b_im.reshape(bsz, r, r * c), f_re, f_im, half)
    return y.reshape(bsz, seq_len, c)


def hyena_mixer(zh, p):
    zs = depthwise_conv(zh, p['hy_conv_w']) + p['hy_conv_b']
    x0, x1, v = jnp.split(zs, 3, axis=-1)
    z = x1 * v
    seq_len = zh.shape[1]
    h_fwd, h_bwd = hyena_filters(seq_len, p)
    if 2 * seq_len == FFT_R * FFT_R:
        taps = jnp.concatenate([h_fwd[:1] + h_bwd[:1] + p['hy_bias'], h_fwd[1:], jnp.zeros_like(h_fwd[:1]), h_bwd[:0:-1]], axis=0)
        y = x0 * long_conv_fft(z, taps)
    else:
        y = x0 * (two_sided_long_conv(z, h_fwd, h_bwd) + p['hy_bias'] * z)
    return rms_norm(y, p['hy_norm_g'])


def in_projection(h, p, full):
    bsz, n, k = h.shape
    hb = h.reshape(bsz * n, k).astype(BF16)
    out = {}
    out['s5'] = matmul(hb, p['w_in_s5'], name="in_s5").reshape(bsz, n, -1)
    out['qkv'] = matmul(hb, p['w_in_qkv'], name="in_qkv").reshape(bsz, n, -1)
    ab = matmul(hb, p['w_in_ab'], name="in_ab").reshape(bsz, n, -1)
    out['a'] = ab[..., :2 * DN_HEADS]
    out['b'] = ab[..., 2 * DN_HEADS:4 * DN_HEADS]
    if full:
        out['g'] = matmul(hb, p['w_in_g'], name="in_g").reshape(bsz, n, -1)
        out['hy'] = matmul(hb, p['w_in_hy'], name="in_hy").reshape(bsz, n, -1)
    return out


def token_mixer(h, hc, p, need_ctx_out):
    z = in_projection(h, p, True)
    zc = in_projection(hc, p, need_ctx_out)
    s5_y, s5_yc = s5_mixer(z['s5'], zc['s5'], p, need_ctx_out)
    dn_y, dn_yc = deltanet_mixer(z, zc, p, need_ctx_out)
    y = mm3(jnp.concatenate([s5_y, dn_y, hyena_mixer(z['hy'], p)], axis=-1).astype(BF16), p['w_out'], "out_proj")
    if not need_ctx_out:
        return y, None
    yc = mm3(jnp.concatenate([s5_yc, dn_yc, hyena_mixer(zc['hy'], p)], axis=-1).astype(BF16), p['w_out'], "out_proj_ctx")
    return y, yc


def _ada_kernel(a_ref, w_ref, o_ref):
    o_ref[...] = jnp.dot(a_ref[...].astype(BF16), w_ref[0].astype(BF16), preferred_element_type=F32)


def ada_modulation(c_rows, w_ada, b_ada, layer):
    rows, k = c_rows.shape
    n = w_ada.shape[2]
    tn = 1024
    a = jnp.zeros((8, k), F32).at[:rows].set(jax.nn.silu(c_rows))
    out = pl.pallas_call(
        _ada_kernel,
        grid=(n // tn,),
        in_specs=[pl.BlockSpec((8, k), lambda j: (0, 0)), pl.BlockSpec((1, k, tn), lambda j: (layer, 0, j))],
        out_specs=pl.BlockSpec((8, tn), lambda j: (0, j)),
        out_shape=jax.ShapeDtypeStruct((8, n), F32),
        compiler_params=pltpu.CompilerParams(dimension_semantics=("parallel",), vmem_limit_bytes=VMEM_LIMIT),
        name="ada",
    )(a, w_ada)
    return out[:rows] + b_ada


def trunk_layer(x, xc, c, c_ctx, p, last):
    bsz, n, dm = x.shape
    lc = xc.shape[1]
    ada = ada_modulation(jnp.concatenate([c, c_ctx[None]], axis=0), p['w_ada'], p['b_ada'], p['layer'])
    sh1, sc1, g1, sh2, sc2, g2 = jnp.split(ada[:bsz, None, :], 6, axis=-1)
    mc = jnp.split(ada[bsz], 6, axis=-1)
    y, yc = token_mixer(modulate(x, sh1, sc1), modulate(xc, mc[0], mc[1]), p, not last)
    x = layer_norm(DEEPNORM_ALPHA * x + g1 * y, p['ln1_g'], p['ln1_b'])
    if last:
        f = hier_moe(modulate(x, sh2, sc2).reshape(bsz * n, dm), p).reshape(bsz, n, dm)
        return layer_norm(DEEPNORM_ALPHA * x + g2 * f, p['ln2_g'], p['ln2_b']), None
    xc = layer_norm(DEEPNORM_ALPHA * xc + mc[2] * yc, p['ln1_g'], p['ln1_b'])
    tokens = jnp.concatenate([modulate(x, sh2, sc2).reshape(bsz * n, dm), modulate(xc, mc[3], mc[4]).reshape(bsz * lc, dm)], axis=0)
    f = hier_moe(tokens, p)
    x = layer_norm(DEEPNORM_ALPHA * x + g2 * f[:bsz * n].reshape(bsz, n, dm), p['ln2_g'], p['ln2_b'])
    xc = layer_norm(DEEPNORM_ALPHA * xc + mc[5] * f[bsz * n:].reshape(bsz, lc, dm), p['ln2_g'], p['ln2_b'])
    return x, xc


def kernel(x, c, ctx, c_ctx, w_ada, b_ada, w_in, s5_lam_re, s5_lam_im, s5_log_step, s5_b_re, s5_b_im, s5_c_re, s5_c_im, s5_d, s5_glu_w, s5_glu_b, s5_norm_g, dn_conv_w, dn_a_log, dn_dt_bias, dn_norm_g, hy_conv_w, hy_conv_b, hy_f_w1, hy_f_b1, hy_f_w2, hy_f_b2, hy_f_w3, hy_f_b3, hy_f_freq, hy_f_w4, hy_bias, hy_norm_g, w_out, ln1_g, ln1_b, ln2_g, ln2_b, moe_w_group, moe_b_group, moe_w_expert, moe_b_expert, moe_w_gate, moe_w_up, moe_w_down):
    xc = ctx
    for l in range(DEPTH):
        w_in_l = w_in[l]
        w_ab = jnp.zeros((D_MODEL, LANE), F32).at[:, :4 * DN_HEADS].set(w_in_l[:, COL_DN_A:STATE_COLS])
        w_router = jnp.zeros((D_MODEL, LANE), F32).at[:, :N_GROUPS].set(moe_w_group[l]).at[:, N_GROUPS:N_GROUPS + N_EXPERTS].set(moe_w_expert[l])
        p = {
            'layer': l, 'w_ada': w_ada, 'b_ada': b_ada[l],
            'w_in_s5': w_in_l[:, COL_S5:COL_DN_QKV].astype(BF16),
            'w_in_qkv': w_in_l[:, COL_DN_QKV:COL_DN_A].astype(BF16),
            'w_in_ab': w_ab.astype(BF16),
            'w_in_g': w_in_l[:, COL_DN_G:COL_HY].astype(BF16),
            'w_in_hy': w_in_l[:, COL_HY:].astype(BF16),
            's5_lam_re': s5_lam_re[l], 's5_lam_im': s5_lam_im[l], 's5_log_step': s5_log_step[l],
            's5_b_re': s5_b_re[l], 's5_b_im': s5_b_im[l], 's5_c_re': s5_c_re[l], 's5_c_im': s5_c_im[l],
            's5_d': s5_d[l], 's5_glu_w': s5_glu_w[l].astype(BF16), 's5_glu_b': s5_glu_b[l], 's5_norm_g': s5_norm_g[l],
            'dn_conv_w': dn_conv_w[l], 'dn_a_log': dn_a_log[l], 'dn_dt_bias': dn_dt_bias[l], 'dn_norm_g': dn_norm_g[l],
            'hy_conv_w': hy_conv_w[l], 'hy_conv_b': hy_conv_b[l],
            'hy_f_w1': hy_f_w1[l], 'hy_f_b1': hy_f_b1[l], 'hy_f_w2': hy_f_w2[l], 'hy_f_b2': hy_f_b2[l],
            'hy_f_w3': hy_f_w3[l], 'hy_f_b3': hy_f_b3[l], 'hy_f_freq': hy_f_freq[l], 'hy_f_w4': hy_f_w4[l],
            'hy_bias': hy_bias[l], 'hy_norm_g': hy_norm_g[l], 'w_out': w_out[l].astype(BF16),
            'ln1_g': ln1_g[l], 'ln1_b': ln1_b[l], 'ln2_g': ln2_g[l], 'ln2_b': ln2_b[l],
            'moe_w_router': w_router, 'moe_b_group': moe_b_group[l], 'moe_b_expert': moe_b_expert[l],
            'moe_w_gate': cast_layer_experts(moe_w_gate, l), 'moe_w_up': cast_layer_experts(moe_w_up, l), 'moe_w_down': cast_layer_experts(moe_w_down, l),
        }
        x, xc = trunk_layer(x, xc, c, c_ctx, p, l == DEPTH - 1)
    return x
```

```python
import functools
import math

import jax
import jax.numpy as jnp
from jax import lax
from jax.experimental import pallas as pl
from jax.experimental.pallas import tpu as pltpu

D_MODEL = 4096
DEPTH = 2
GRID_W = 64

S5_WIDTH = D_MODEL // 4
S5_GROUP = 16
S5_GROUPS = S5_WIDTH // S5_GROUP
S5_STATE = 64

DN_WIDTH = D_MODEL // 2
DN_HEAD_DIM = 128
DN_HEADS = DN_WIDTH // DN_HEAD_DIM
DN_CHUNK = 64
SHORT_CONV = 3

HY_WIDTH = D_MODEL - S5_WIDTH - DN_WIDTH
HY_BANDS = 16
HY_EMB = 1 + 2 * HY_BANDS
HY_HIDDEN = 64
HY_DECAY_SHORT_PCT = 0.3
HY_DECAY_LONG_PCT = 1.5
HY_DECAY_TARGET = 1e-2

MIX_WIDTH = S5_WIDTH + DN_WIDTH + HY_WIDTH

COL_S5 = 0
COL_DN_QKV = COL_S5 + S5_WIDTH
COL_DN_A = COL_DN_QKV + 3 * DN_WIDTH
COL_DN_B = COL_DN_A + 2 * DN_HEADS
STATE_COLS = COL_DN_B + 2 * DN_HEADS
COL_DN_G = STATE_COLS
COL_HY = COL_DN_G + DN_WIDTH
IN_WIDTH = COL_HY + 3 * HY_WIDTH

N_GROUPS = 4
EXPERTS_PER_GROUP = 8
N_EXPERTS = N_GROUPS * EXPERTS_PER_GROUP
TOP_K = 2
D_EXPERT = 512

DEEPNORM_ALPHA = (2 * DEPTH) ** 0.25
LN_EPS = 1e-5
RMS_EPS = 1e-6
F32 = jnp.float32
BF16 = jnp.bfloat16

V7X_VMEM_BYTES = 64 * 1024 * 1024
VMEM_LIMIT = 52 * 1024 * 1024
MOE_VMEM_LIMIT = 56 * 1024 * 1024
LANE = 128

MOE_TILE = 256
DN_HEAD_GROUP = 4
DN_SPLIT_LEVELS = 2
S5_T = 64


def _mm_kernel(a_ref, b_ref, o_ref):
    o_ref[...] = jnp.dot(a_ref[...].astype(BF16), b_ref[...].astype(BF16), preferred_element_type=F32).astype(o_ref.dtype)


def _pick(n, prefs):
    for p in prefs:
        if n % p == 0:
            return p
    return n


def matmul(a, b, out_dtype=F32, name="matmul"):
    m, k = a.shape
    n = b.shape[1]
    tm = _pick(m, (512, 256, 128, 64, 32, 16, 8))
    tn = _pick(n, (1024, 512, 256, 128))
    return pl.pallas_call(
        _mm_kernel,
        grid=(m // tm, n // tn),
        in_specs=[pl.BlockSpec((tm, k), lambda i, j: (i, 0)), pl.BlockSpec((k, tn), lambda i, j: (0, j))],
        out_specs=pl.BlockSpec((tm, tn), lambda i, j: (i, j)),
        out_shape=jax.ShapeDtypeStruct((m, n), out_dtype),
        compiler_params=pltpu.CompilerParams(dimension_semantics=("parallel", "parallel"), vmem_limit_bytes=VMEM_LIMIT),
        name=name,
    )(a, b)


def _mm_f32_kernel(a_ref, b_ref, o_ref):
    o_ref[...] = jnp.dot(a_ref[...], b_ref[...], preferred_element_type=F32, precision=lax.Precision.HIGHEST)


def matmul_f32(a, b, name="matmul_f32"):
    m, k = a.shape
    n = b.shape[1]
    tm = _pick(m, (512, 256, 128, 64, 32, 16, 8))
    return pl.pallas_call(
        _mm_f32_kernel,
        grid=(m // tm,),
        in_specs=[pl.BlockSpec((tm, k), lambda i: (i, 0)), pl.BlockSpec((k, n), lambda i: (0, 0))],
        out_specs=pl.BlockSpec((tm, n), lambda i: (i, 0)),
        out_shape=jax.ShapeDtypeStruct((m, n), F32),
        compiler_params=pltpu.CompilerParams(dimension_semantics=("parallel",), vmem_limit_bytes=VMEM_LIMIT),
        name=name,
    )(a, b)


def _moe_kernel(tile_expert_ref, n_used_ref, x_ref, w_ref, wg_ref, wu_ref, wd_ref, o_ref, wg_s, wu_s, wd_s):
    i = pl.program_id(0)
    prev = tile_expert_ref[jnp.maximum(i - 1, 0)]

    @pl.when((i == 0) | (tile_expert_ref[i] != prev))
    def _():
        wg_s[...] = wg_ref[0, 0].astype(BF16)
        wu_s[...] = wu_ref[0, 0].astype(BF16)
        wd_s[...] = wd_ref[0, 0].astype(BF16)

    @pl.when(i < n_used_ref[0])
    def _():
        x = x_ref[...]
        g = jnp.dot(x, wg_s[...], preferred_element_type=F32)
        u = jnp.dot(x, wu_s[...], preferred_element_type=F32)
        h = (g * jax.nn.sigmoid(g)) * u * w_ref[...]
        o_ref[...] = jnp.dot(h.astype(BF16), wd_s[...], preferred_element_type=F32)

    @pl.when(i >= n_used_ref[0])
    def _():
        o_ref[...] = jnp.zeros_like(o_ref)


def moe_experts(tile_expert, n_used, xs, row_w, wg, wu, wd, layer):
    p_rows, dm = xs.shape
    n_tiles = p_rows // MOE_TILE
    single = pl.Buffered(1)
    grid_spec = pltpu.PrefetchScalarGridSpec(
        num_scalar_prefetch=2,
        grid=(n_tiles,),
        in_specs=[
            pl.BlockSpec((MOE_TILE, dm), lambda i, te, nu: (i, 0)),
            pl.BlockSpec((MOE_TILE, 1), lambda i, te, nu: (i, 0)),
            pl.BlockSpec((1, 1, dm, D_EXPERT), lambda i, te, nu: (layer, te[i], 0, 0), pipeline_mode=single),
            pl.BlockSpec((1, 1, dm, D_EXPERT), lambda i, te, nu: (layer, te[i], 0, 0), pipeline_mode=single),
            pl.BlockSpec((1, 1, D_EXPERT, dm), lambda i, te, nu: (layer, te[i], 0, 0), pipeline_mode=single),
        ],
        out_specs=pl.BlockSpec((MOE_TILE, dm), lambda i, te, nu: (i, 0)),
        scratch_shapes=[pltpu.VMEM((dm, D_EXPERT), BF16), pltpu.VMEM((dm, D_EXPERT), BF16), pltpu.VMEM((D_EXPERT, dm), BF16)],
    )
    return pl.pallas_call(
        _moe_kernel,
        grid_spec=grid_spec,
        out_shape=jax.ShapeDtypeStruct((p_rows, dm), F32),
        compiler_params=pltpu.CompilerParams(dimension_semantics=("arbitrary",), vmem_limit_bytes=MOE_VMEM_LIMIT),
        name="moe_experts",
    )(tile_expert, n_used, xs, row_w, wg, wu, wd)


def hier_moe(t, p):
    n_tok = t.shape[0]
    logits = matmul_f32(t, p['moe_w_router'], name="moe_router")
    group_logits = logits[:, :N_GROUPS] + p['moe_b_group']
    group = jnp.argmax(group_logits, axis=-1)
    group_w = jnp.max(jax.nn.softmax(group_logits, axis=-1), axis=-1, keepdims=True)
    exp_logits = (logits[:, N_GROUPS:N_GROUPS + N_EXPERTS] + p['moe_b_expert']).reshape(n_tok, N_GROUPS, EXPERTS_PER_GROUP)
    exp_logits = jnp.take_along_axis(exp_logits, group[:, None, None], axis=1)[:, 0]
    top_logits, top_idx = lax.top_k(exp_logits, TOP_K)
    top_w = jax.nn.softmax(top_logits, axis=-1) * group_w
    expert_id = (group[:, None] * EXPERTS_PER_GROUP + top_idx).astype(jnp.int32)

    n_rows = n_tok * TOP_K
    flat_e = expert_id.reshape(n_rows)
    flat_w = top_w.reshape(n_rows)
    flat_t = jnp.arange(n_rows, dtype=jnp.int32) // TOP_K
    order = jnp.argsort(flat_e, stable=True)
    sorted_e = flat_e[order]
    counts = jnp.zeros((N_EXPERTS,), jnp.int32).at[flat_e].add(1)
    padded = ((counts + MOE_TILE - 1) // MOE_TILE) * MOE_TILE
    pad_end = jnp.cumsum(padded)
    pad_start = pad_end - padded
    start = jnp.cumsum(counts) - counts
    dest = pad_start[sorted_e] + (jnp.arange(n_rows, dtype=jnp.int32) - start[sorted_e])
    p_rows = n_rows + N_EXPERTS * MOE_TILE
    n_tiles = p_rows // MOE_TILE
    row_token = jnp.zeros((p_rows,), jnp.int32).at[dest].set(flat_t[order])
    row_w = jnp.zeros((p_rows,), F32).at[dest].set(flat_w[order])
    pos = jnp.zeros((n_rows,), jnp.int32).at[order].set(dest).reshape(n_tok, TOP_K)
    n_used = (pad_end[-1] // MOE_TILE).astype(jnp.int32)
    tile_idx = jnp.minimum(jnp.arange(n_tiles, dtype=jnp.int32), n_used - 1)
    tile_expert = jnp.minimum(jnp.searchsorted(pad_end, tile_idx * MOE_TILE, side='right'), N_EXPERTS - 1).astype(jnp.int32)

    xs = t.astype(BF16)[row_token]
    ys = moe_experts(tile_expert, n_used.reshape(1), xs, row_w[:, None], p['moe_w_gate'], p['moe_w_up'], p['moe_w_down'], p['layer'])
    return ys[pos[:, 0]] + ys[pos[:, 1]]


def _flip(t, rev):
    return jnp.flip(t, axis=1) if rev else t


def layer_norm(x, g, b):
    mu = jnp.mean(x, axis=-1, keepdims=True)
    var = jnp.mean(jnp.square(x - mu), axis=-1, keepdims=True)
    return (x - mu) * lax.rsqrt(var + LN_EPS) * g + b


def rms_norm(x, g):
    return x * lax.rsqrt(jnp.mean(jnp.square(x), axis=-1, keepdims=True) + RMS_EPS) * g


def l2_normalize(t):
    return t * lax.rsqrt(jnp.sum(t * t, axis=-1, keepdims=True) + 1e-6)


def depthwise_conv(u, w):
    k = w.shape[0]
    return lax.conv_general_dilated(u, w[:, None, :], window_strides=(1,), padding=[(k // 2, k // 2)], dimension_numbers=('NWC', 'WIO', 'NWC'), feature_group_count=u.shape[-1])


def modulate(t, shift, scale):
    return t * (1 + scale) + shift


def mm3(t, w, name):
    bsz, n, k = t.shape
    return matmul(t.reshape(bsz * n, k), w, name=name).reshape(bsz, n, w.shape[1])


def s5_discretize(lam_re, lam_im, log_step, b_re, b_im):
    step = jnp.exp(log_step)[:, None]
    mag = jnp.exp(lam_re * step)
    a_re = mag * jnp.cos(lam_im * step)
    a_im = mag * jnp.sin(lam_im * step)
    den = lam_re * lam_re + lam_im * lam_im
    f_re = ((a_re - 1.0) * lam_re + a_im * lam_im) / den
    f_im = (a_im * lam_re - (a_re - 1.0) * lam_im) / den
    bb_re = f_re[..., None] * b_re - f_im[..., None] * b_im
    bb_im = f_re[..., None] * b_im + f_im[..., None] * b_re
    return a_re, a_im, bb_re, bb_im


def _cmul(x, y):
    return x[0] * y[0] - x[1] * y[1], x[0] * y[1] + x[1] * y[0]


def s5_tables(p):
    hp = lax.Precision.HIGHEST
    t_len, grp, st = S5_T, S5_GROUPS, S5_STATE
    taps, wst, rd, pq = [], [], [], []
    for d in range(2):
        a_re, a_im, bb_re, bb_im = s5_discretize(p['s5_lam_re'][d], p['s5_lam_im'][d], p['s5_log_step'][d], p['s5_b_re'][d], p['s5_b_im'][d])
        c_re, c_im = p['s5_c_re'][d], p['s5_c_im'][d]
        pr, pi = lax.associative_scan(_cmul, (jnp.broadcast_to(a_re, (t_len, grp, st)), jnp.broadcast_to(a_im, (t_len, grp, st))), axis=0)
        pw_re = jnp.concatenate([jnp.ones((1, grp, st), F32), pr], axis=0)
        pw_im = jnp.concatenate([jnp.zeros((1, grp, st), F32), pi], axis=0)
        ca_re = c_re[None] * pw_re[:, :, None, :] - c_im[None] * pw_im[:, :, None, :]
        ca_im = c_re[None] * pw_im[:, :, None, :] + c_im[None] * pw_re[:, :, None, :]
        taps.append(jnp.einsum('tgip,gpj->tgij', ca_re[:t_len], bb_re, precision=hp) - jnp.einsum('tgip,gpj->tgij', ca_im[:t_len], bb_im, precision=hp))
        e_st = jnp.arange(t_len - 1, -1, -1) if d == 0 else jnp.arange(t_len)
        w_re = pw_re[e_st][..., None] * bb_re[None] - pw_im[e_st][..., None] * bb_im[None]
        w_im = pw_re[e_st][..., None] * bb_im[None] + pw_im[e_st][..., None] * bb_re[None]
        to_rows = lambda w: jnp.transpose(w, (1, 0, 3, 2)).reshape(grp, t_len * S5_GROUP, st)
        wst += [to_rows(w_re), to_rows(w_im)]
        e_rd = jnp.arange(1, t_len + 1) if d == 0 else jnp.arange(t_len, 0, -1)
        to_cols = lambda r: jnp.transpose(r, (1, 3, 0, 2)).reshape(grp, st, t_len * S5_GROUP)
        rd += [to_cols(ca_re[e_rd]), to_cols(-ca_im[e_rd])]
        lv_re, lv_im = pw_re[t_len], pw_im[t_len]
        for _ in range(8):
            pq += [jnp.concatenate([lv_re, lv_re], axis=-1), jnp.concatenate([-lv_im, lv_im], axis=-1)]
            lv_re, lv_im = _cmul((lv_re, lv_im), (lv_re, lv_im))
    kf, kb = taps
    blocks = jnp.concatenate([kb[1:][::-1], (kf[0] + kb[0])[None], kf[1:], jnp.zeros_like(kf[:1])], axis=0)
    ext = jnp.transpose(blocks, (1, 3, 0, 2)).reshape(grp, S5_GROUP, 2 * t_len * S5_GROUP)
    return ext, jnp.concatenate(wst, axis=-1).astype(BF16), jnp.concatenate(rd, axis=1).astype(BF16), jnp.stack(pq, axis=1)


def _s5_kernel(u_ref, ext_ref, wst_ref, rd_ref, pq_ref, h0_ref, y_ref, hfin_ref, m_ref, *, n_chunks, bsz):
    cw = S5_T * S5_GROUP
    st2 = 2 * S5_STATE
    ext = ext_ref[0]
    for s in range(S5_T):
        off = S5_GROUP * (S5_T - 1 - s)
        shifted = ext if off == 0 else pltpu.roll(ext, shift=2 * cw - off, axis=1)
        m_ref[S5_GROUP * s:S5_GROUP * (s + 1), :] = shifted[:, :cw].astype(BF16)
    u = u_ref[0]
    rows = u.shape[0]
    y = jnp.dot(u, m_ref[...], preferred_element_type=F32)
    s_all = jnp.dot(u, wst_ref[0], preferred_element_type=F32)
    sf, sb = s_all[:, :st2], s_all[:, st2:]
    row = lax.broadcasted_iota(jnp.int32, (rows, st2), 0)
    c_idx = row % n_chunks

    def cmul_rows(idx, x):
        return pq_ref[0, idx:idx + 1, :] * x + pq_ref[0, idx + 1:idx + 2, :] * pltpu.roll(x, shift=S5_STATE, axis=1)

    h0 = h0_ref[0]
    h0f = jnp.zeros((rows, st2), F32)
    h0b = jnp.zeros((rows, st2), F32)
    for b in range(bsz):
        in_b = (row >= b * n_chunks) & (row < (b + 1) * n_chunks)
        h0f = jnp.where(in_b, h0[b:b + 1, :st2], h0f)
        h0b = jnp.where(in_b, h0[b:b + 1, st2:], h0b)
    hf = jnp.where(c_idx == 0, h0f, pltpu.roll(sf, shift=1, axis=0))
    hb = jnp.where(c_idx == n_chunks - 1, h0b, pltpu.roll(sb, shift=rows - 1, axis=0))
    level, sh = 0, 1
    while sh < n_chunks:
        dn = jnp.where(c_idx >= sh, pltpu.roll(hf, shift=sh, axis=0), 0.0)
        up = jnp.where(c_idx < n_chunks - sh, pltpu.roll(hb, shift=rows - sh, axis=0), 0.0)
        hf = hf + cmul_rows(2 * level, dn)
        hb = hb + cmul_rows(16 + 2 * level, up)
        level, sh = level + 1, sh * 2
    h_in = jnp.concatenate([hf, hb], axis=1)
    y_ref[0] = y + jnp.dot(h_in.astype(BF16), rd_ref[0], preferred_element_type=F32)
    hfin_ref[0] = jnp.concatenate([cmul_rows(0, hf) + sf, cmul_rows(16, hb) + sb], axis=1)


def s5_scan(u, tables, h0, n_chunks, bsz):
    ext, wst, rd, pq = tables
    grp, rows, cw = u.shape
    st4 = 4 * S5_STATE
    blk = lambda *shape: pl.BlockSpec((1,) + shape, lambda i: (i, 0, 0))
    return pl.pallas_call(
        functools.partial(_s5_kernel, n_chunks=n_chunks, bsz=bsz),
        grid=(grp,),
        in_specs=[blk(rows, cw), blk(S5_GROUP, 2 * cw), blk(cw, st4), blk(st4, cw), blk(32, 2 * S5_STATE), blk(bsz, st4)],
        out_specs=[blk(rows, cw), blk(rows, st4)],
        out_shape=[jax.ShapeDtypeStruct((grp, rows, cw), F32), jax.ShapeDtypeStruct((grp, rows, st4), F32)],
        scratch_shapes=[pltpu.VMEM((cw, cw), BF16)],
        compiler_params=pltpu.CompilerParams(dimension_semantics=("parallel",), vmem_limit_bytes=VMEM_LIMIT),
        name="s5_scan",
    )(u, ext, wst, rd, pq, h0)


def s5_sequence(u, tables, h0):
    bsz, n, _ = u.shape
    n_chunks = n // S5_T
    rows = bsz * n_chunks
    rows_pad = -(-rows // 16) * 16
    ug = u.astype(BF16).reshape(bsz, n_chunks, S5_T, S5_GROUPS, S5_GROUP)
    ug = jnp.transpose(ug, (3, 0, 1, 2, 4)).reshape(S5_GROUPS, rows, S5_T * S5_GROUP)
    if rows_pad != rows:
        ug = jnp.pad(ug, ((0, 0), (0, rows_pad - rows), (0, 0)))
    y, hfin = s5_scan(ug, tables, h0, n_chunks, bsz)
    y = y[:, :rows].reshape(S5_GROUPS, bsz, n_chunks, S5_T, S5_GROUP)
    y = jnp.transpose(y, (1, 2, 3, 0, 4)).reshape(bsz, n, S5_WIDTH)
    hfin = hfin[:, :rows].reshape(S5_GROUPS, bsz, n_chunks, 4 * S5_STATE)
    st2 = 2 * S5_STATE
    h_end = jnp.concatenate([hfin[:, :, n_chunks - 1, :st2], hfin[:, :, 0, st2:]], axis=-1)
    return y, h_end


def s5_glu(y, p):
    y = jax.nn.gelu(y)
    y = y * jax.nn.sigmoid(mm3(y, p['s5_glu_w'], "s5_glu") + p['s5_glu_b'])
    return rms_norm(y, p['s5_norm_g'])


def s5_mixer(u, uc, p, need_ctx_out):
    bsz = u.shape[0]
    tables = s5_tables(p)
    yc, h_ctx = s5_sequence(uc, tables, jnp.zeros((S5_GROUPS, bsz, 4 * S5_STATE), F32))
    y, _ = s5_sequence(u, tables, h_ctx)
    out = s5_glu(u * p['s5_d'] + y, p)
    out_c = s5_glu(uc * p['s5_d'] + yc, p) if need_ctx_out else None
    return out, out_c


def dn_features(z_qkv, conv_w):
    bsz, seq_len, _ = z_qkv.shape
    q, k, v = jnp.split(jax.nn.silu(depthwise_conv(z_qkv, conv_w)), 3, axis=-1)
    shp = (bsz, seq_len, DN_HEADS, DN_HEAD_DIM)
    return l2_normalize(q.reshape(shp)) * DN_HEAD_DIM ** -0.5, l2_normalize(k.reshape(shp)), v.reshape(shp)


def dn_gates(z_a, z_b, a_log, dt_bias):
    bsz, seq_len, _ = z_a.shape
    shp = (bsz, seq_len, 2, DN_HEADS)
    g = -jnp.exp(a_log) * jax.nn.softplus(z_a.reshape(shp) + dt_bias)
    beta = jax.nn.sigmoid(z_b.reshape(shp))
    return g, beta


def _dn_kernel(q_ref, k_ref, v_ref, g_ref, gt_ref, b_ref, s0_ref, o_ref, sfin_ref, s_ref, *, n_chunks):
    cs, hd, hg = DN_CHUNK, DN_HEAD_DIM, DN_HEAD_GROUP
    rows = cs * hg
    n_groups = DN_HEADS // hg
    groups = range(n_groups)
    step = pl.program_id(1)

    @pl.when(step == 0)
    def _():
        s_ref[...] = s0_ref[0]

    r_i = lax.broadcasted_iota(jnp.int32, (rows, rows), 0)
    c_i = lax.broadcasted_iota(jnp.int32, (rows, rows), 1)
    same_head = (r_i // cs) == (c_i // cs)
    incl = same_head & ((r_i % cs) >= (c_i % cs))
    strict = same_head & ((r_i % cs) > (c_i % cs))
    upto = same_head & ((r_i % cs) <= (c_i % cs))
    r_c = lax.broadcasted_iota(jnp.int32, (cs, cs), 0)
    c_c = lax.broadcasted_iota(jnp.int32, (cs, cs), 1)
    hp = lax.Precision.HIGHEST
    beta = b_ref[0, 0]
    gc = jnp.dot((r_c >= c_c).astype(F32), g_ref[0, 0], preferred_element_type=F32, precision=hp)
    gc_row = jnp.dot(gt_ref[0, 0], upto.astype(F32), preferred_element_type=F32, precision=hp)
    g_tot = gc[cs - 1:cs, :]
    e_g = jnp.exp(gc)
    e_tail = jnp.exp(g_tot - gc)
    e_tot = jnp.exp(g_tot)

    def head_cols(h):
        return slice(h * hd, (h + 1) * hd)

    def stack_heads(grp, fn):
        return jnp.concatenate([fn(grp * hg + j) for j in range(hg)], axis=0)

    def col(x, h, width):
        return jnp.broadcast_to(x[:, h:h + 1], (cs, width))

    def mm(a, b):
        return jnp.dot(a.astype(BF16), b.astype(BF16), preferred_element_type=F32)

    def mm_split_rhs(a_bf16, b):
        hi = b.astype(BF16)
        lo = (b - hi.astype(F32)).astype(BF16)
        t = jnp.dot(a_bf16, jnp.concatenate([hi, lo], axis=1), preferred_element_type=F32)
        return t[:, :b.shape[1]] + t[:, b.shape[1]:]

    kst = [stack_heads(gr, lambda h: k_ref[0, :, head_cols(h)]) for gr in groups]
    qst = [stack_heads(gr, lambda h: q_ref[0, :, head_cols(h)]) for gr in groups]
    b_col = [stack_heads(gr, lambda h: col(beta, h, hd)) for gr in groups]
    eg_col = [stack_heads(gr, lambda h: col(e_g, h, hd)) for gr in groups]
    kb = [kst[gr] * b_col[gr] for gr in groups]
    decay = []
    for gr in groups:
        diff = stack_heads(gr, lambda h: col(gc, h, rows)) - gc_row[gr:gr + 1, :]
        decay.append(jnp.where(incl, jnp.exp(jnp.where(incl, diff, 0.0)), 0.0))
    kq = [lax.dot_general(jnp.concatenate([kb[gr], qst[gr]], axis=0).astype(BF16), kst[gr].astype(BF16),
                          (((1,), (1,)), ((), ())), preferred_element_type=F32) for gr in groups]
    n = [jnp.where(strict, kq[gr][:rows] * decay[gr], 0.0).astype(BF16) for gr in groups]
    qk = [(kq[gr][rows:] * decay[gr]).astype(BF16) for gr in groups]
    x = [jnp.concatenate([stack_heads(gr, lambda h: v_ref[0, :, head_cols(h)]) * b_col[gr], kb[gr] * eg_col[gr]], axis=1) for gr in groups]
    x = [x[gr] - mm_split_rhs(n[gr], x[gr]) for gr in groups]
    pw = n
    for level in range(5):
        pw = [jnp.dot(pw[gr], pw[gr], preferred_element_type=F32).astype(BF16) for gr in groups]
        if level < DN_SPLIT_LEVELS:
            x = [x[gr] + mm_split_rhs(pw[gr], x[gr]) for gr in groups]
        else:
            x = [x[gr] + jnp.dot(pw[gr], x[gr].astype(BF16), preferred_element_type=F32) for gr in groups]
    qd = [qst[gr] * eg_col[gr] for gr in groups]
    for gr in groups:
        v_new, o_state = [], []
        for j in range(hg):
            h, rs = gr * hg + j, slice(j * cs, (j + 1) * cs)
            ws = mm(jnp.concatenate([x[gr][rs, hd:], qd[gr][rs]], axis=0), s_ref[h])
            v_new.append(x[gr][rs, :hd] - ws[:cs])
            o_state.append(ws[cs:])
        v_all = jnp.concatenate(v_new, axis=0).astype(BF16)
        o_all = jnp.concatenate(o_state, axis=0) + jnp.dot(qk[gr], v_all, preferred_element_type=F32)
        for j in range(hg):
            h, rs = gr * hg + j, slice(j * cs, (j + 1) * cs)
            o_ref[0, :, head_cols(h)] = o_all[rs]
            kt = (kst[gr][rs] * col(e_tail, h, hd)).astype(BF16)
            s_ref[h] = s_ref[h] * e_tot[:, h:h + 1] + lax.dot_general(kt, v_all[rs], (((0,), (0,)), ((), ())), preferred_element_type=F32)

    @pl.when(step == n_chunks - 1)
    def _():
        sfin_ref[0] = s_ref[...]


def gated_delta_rule(q, k, v, g, beta, s0):
    bsz, seq_len, heads, hd = q.shape
    width = heads * hd
    n_chunks = seq_len // DN_CHUNK
    q, k, v = (t.reshape(bsz, seq_len, width) for t in (q, k, v))
    g4 = g.reshape(bsz, n_chunks, DN_CHUNK, heads)
    gt4 = jnp.swapaxes(g4, 2, 3).reshape(bsz, n_chunks, heads // DN_HEAD_GROUP, DN_HEAD_GROUP * DN_CHUNK)
    b4 = beta.reshape(bsz, n_chunks, DN_CHUNK, heads)
    tok_spec = pl.BlockSpec((1, DN_CHUNK, width), lambda b, i: (b, i, 0))
    gate_spec = pl.BlockSpec((1, 1, DN_CHUNK, heads), lambda b, i: (b, i, 0, 0))
    gate_t_spec = pl.BlockSpec((1, 1, heads // DN_HEAD_GROUP, DN_HEAD_GROUP * DN_CHUNK), lambda b, i: (b, i, 0, 0))
    state_spec = pl.BlockSpec((1, heads, hd, hd), lambda b, i: (b, 0, 0, 0))
    o, s_fin = pl.pallas_call(
        functools.partial(_dn_kernel, n_chunks=n_chunks),
        grid=(bsz, n_chunks),
        in_specs=[tok_spec, tok_spec, tok_spec, gate_spec, gate_t_spec, gate_spec, state_spec],
        out_specs=[tok_spec, state_spec],
        out_shape=[jax.ShapeDtypeStruct((bsz, seq_len, width), F32), jax.ShapeDtypeStruct(s0.shape, F32)],
        scratch_shapes=[pltpu.VMEM((heads, hd, hd), F32)],
        compiler_params=pltpu.CompilerParams(dimension_semantics=("parallel", "arbitrary"), vmem_limit_bytes=VMEM_LIMIT),
        name="delta_rule",
    )(q, k, v, g4, gt4, b4, s0)
    return o.reshape(bsz, seq_len, heads, hd), s_fin


def dn_output(o, z_gate, p):
    bsz, seq_len = o.shape[:2]
    gate = jax.nn.silu(z_gate).reshape(bsz, seq_len, DN_HEADS, DN_HEAD_DIM)
    return (rms_norm(o, p['dn_norm_g']) * gate).reshape(bsz, seq_len, DN_WIDTH)


def deltanet_mixer(z, zc, p, need_ctx_out):
    q, k, v = dn_features(z['qkv'], p['dn_conv_w'])
    g, beta = dn_gates(z['a'], z['b'], p['dn_a_log'], p['dn_dt_bias'])
    qc, kc, vc = dn_features(zc['qkv'], p['dn_conv_w'])
    gc, betac = dn_gates(zc['a'], zc['b'], p['dn_a_log'], p['dn_dt_bias'])
    s0 = jnp.zeros((q.shape[0], DN_HEADS, DN_HEAD_DIM, DN_HEAD_DIM), F32)
    o, oc = 0.0, 0.0
    for d in range(2):
        rev = d == 1
        oc_d, s_ctx = gated_delta_rule(*(_flip(t, rev) for t in (qc, kc, vc, gc[:, :, d], betac[:, :, d])), s0)
        o_d, _ = gated_delta_rule(*(_flip(t, rev) for t in (q, k, v, g[:, :, d], beta[:, :, d])), s_ctx)
        o = o + _flip(o_d, rev)
        if need_ctx_out:
            oc = oc + _flip(oc_d, rev)
    y = dn_output(o, z['g'], p)
    yc = dn_output(oc, zc['g'], p) if need_ctx_out else None
    return y, yc


def hyena_filters(seq_len, p):
    hp = lax.Precision.HIGHEST
    t = jnp.linspace(0.0, 1.0, seq_len, dtype=F32)[:, None]
    w = (2.0 * math.pi / seq_len) * jnp.arange(seq_len, dtype=F32)[:, None]
    f = jnp.linspace(1e-4, HY_BANDS - 1, HY_BANDS, dtype=F32)[None, :]
    feats = jnp.concatenate([t, jnp.cos(w * f), -jnp.sin(w * f)], axis=-1)
    freq = p['hy_f_freq']
    h = jnp.sin(freq * (jnp.dot(feats, p['hy_f_w1'], precision=hp) + p['hy_f_b1']))
    h = jnp.sin(freq * (jnp.dot(h, p['hy_f_w2'], precision=hp) + p['hy_f_b2']))
    h = jnp.sin(freq * (jnp.dot(h, p['hy_f_w3'], precision=hp) + p['hy_f_b3']))
    h = jnp.dot(h, p['hy_f_w4'], precision=hp).reshape(seq_len, 2, HY_WIDTH)
    rates = jnp.linspace(math.log(HY_DECAY_TARGET) / HY_DECAY_LONG_PCT, math.log(HY_DECAY_TARGET) / HY_DECAY_SHORT_PCT, HY_WIDTH, dtype=F32)
    h = h * jnp.exp(-t * jnp.abs(rates))[:, None, :]
    return h[:, 0], h[:, 1]


def two_sided_long_conv(u, h_fwd, h_bwd):
    seq_len = u.shape[1]
    taps = jnp.concatenate([h_fwd[:1] + h_bwd[:1], h_fwd[1:], jnp.zeros_like(h_fwd[:1]), h_bwd[:0:-1]], axis=0)
    u_f = jnp.fft.rfft(u, n=2 * seq_len, axis=1)
    t_f = jnp.fft.rfft(taps, axis=0)
    return jnp.fft.irfft(u_f * t_f[None], n=2 * seq_len, axis=1)[:, :seq_len]


FFT_R = 128
FFT_COLS = 8192


def _dft_tables():
    n, big = FFT_R, FFT_R * FFT_R
    idx = jnp.arange(n, dtype=jnp.int32)
    ang = (2.0 * math.pi / n) * ((idx[:, None] * idx[None, :]) % n).astype(F32)
    m = (idx[None, None, :] * (n * idx[None, :, None] + idx[:, None, None])) % big
    g_ang = (2.0 * math.pi / big) * m.astype(F32)
    return jnp.cos(ang), -jnp.sin(ang), jnp.cos(g_ang), -jnp.sin(g_ang)


def _split(x):
    hi = x.astype(BF16)
    return hi, (x - hi.astype(F32)).astype(BF16)


def _hdot(a, b):
    (a_hi, a_lo), (b_hi, b_lo) = a, b
    m = a_hi.shape[0]
    t = jnp.dot(jnp.concatenate([a_hi, a_lo], axis=0), b_hi, preferred_element_type=F32)
    return t[:m] + t[m:] + jnp.dot(a_hi, b_lo, preferred_element_type=F32)


def _fft_rows_kernel(f_re_ref, f_im_ref, x_ref, a_re_ref, a_im_ref):
    x = _split(x_ref[0])
    a_re_ref[0] = _hdot(_split(f_re_ref[...]), x)
    a_im_ref[0] = _hdot(_split(f_im_ref[...]), x)


def fft_rows(x, f_re, f_im):
    bsz, k, m = x.shape
    out = jax.ShapeDtypeStruct((bsz, FFT_R, m), F32)
    f_spec = pl.BlockSpec((FFT_R, k), lambda b, j: (0, 0))
    o_spec = pl.BlockSpec((1, FFT_R, FFT_COLS), lambda b, j: (b, 0, j))
    return pl.pallas_call(
        _fft_rows_kernel,
        grid=(bsz, m // FFT_COLS),
        in_specs=[f_spec, f_spec, pl.BlockSpec((1, k, FFT_COLS), lambda b, j: (b, 0, j))],
        out_specs=[o_spec, o_spec],
        out_shape=[out, out],
        compiler_params=pltpu.CompilerParams(dimension_semantics=("parallel", "parallel"), vmem_limit_bytes=VMEM_LIMIT),
        name="fft_rows",
    )(f_re, f_im, x)


def _fft_spectrum_kernel(g_re_ref, g_im_ref, a_re_ref, a_im_ref, x_re_ref, x_im_ref):
    gr, gi, ar, ai = (_split(t) for t in (g_re_ref[0], g_im_ref[0], a_re_ref[0, 0], a_im_ref[0, 0]))
    x_re_ref[0, 0] = _hdot(gr, ar) - _hdot(gi, ai)
    x_im_ref[0, 0] = _hdot(gr, ai) + _hdot(gi, ar)


def _fft_mid_kernel(g_re_ref, g_im_ref, gt_re_ref, gt_im_ref, a_re_ref, a_im_ref, h_re_ref, h_im_ref, b_re_ref, b_im_ref):
    gr, gi, ar, ai = (_split(t) for t in (g_re_ref[0], g_im_ref[0], a_re_ref[0, 0], a_im_ref[0, 0]))
    xr = _hdot(gr, ar) - _hdot(gi, ai)
    xi = _hdot(gr, ai) + _hdot(gi, ar)
    hr, hi = h_re_ref[0, 0], h_im_ref[0, 0]
    yr = _split(xr * hr - xi * hi)
    yi = _split(xr * hi + xi * hr)
    tr, ti = _split(gt_re_ref[0]), _split(gt_im_ref[0])
    b_re_ref[0, 0] = _hdot(tr, yr) + _hdot(ti, yi)
    b_im_ref[0, 0] = _hdot(tr, yi) - _hdot(ti, yr)


def fft_spectrum(a_re, a_im, g_re, g_im):
    bsz, r, _, c = a_re.shape
    g_spec = pl.BlockSpec((1, r, r), lambda k, b: (k, 0, 0))
    a_spec = pl.BlockSpec((1, 1, r, c), lambda k, b: (b, k, 0, 0))
    out = jax.ShapeDtypeStruct(a_re.shape, F32)
    return pl.pallas_call(
        _fft_spectrum_kernel,
        grid=(r, bsz),
        in_specs=[g_spec, g_spec, a_spec, a_spec],
        out_specs=[a_spec, a_spec],
        out_shape=[out, out],
        compiler_params=pltpu.CompilerParams(dimension_semantics=("parallel", "parallel"), vmem_limit_bytes=VMEM_LIMIT),
        name="fft_spectrum",
    )(g_re, g_im, a_re, a_im)


def fft_filter_mid(a_re, a_im, h_re, h_im, g_re, g_im):
    bsz, r, _, c = a_re.shape
    gt_re, gt_im = jnp.swapaxes(g_re, 1, 2), jnp.swapaxes(g_im, 1, 2)
    g_spec = pl.BlockSpec((1, r, r), lambda k, b: (k, 0, 0))
    a_spec = pl.BlockSpec((1, 1, r, c), lambda k, b: (b, k, 0, 0))
    h_spec = pl.BlockSpec((1, 1, r, c), lambda k, b: (0, k, 0, 0))
    out = jax.ShapeDtypeStruct(a_re.shape, F32)
    return pl.pallas_call(
        _fft_mid_kernel,
        grid=(r, bsz),
        in_specs=[g_spec, g_spec, g_spec, g_spec, a_spec, a_spec, h_spec, h_spec],
        out_specs=[a_spec, a_spec],
        out_shape=[out, out],
        compiler_params=pltpu.CompilerParams(dimension_semantics=("parallel", "parallel"), vmem_limit_bytes=VMEM_LIMIT),
        name="fft_filter_mid",
    )(g_re, g_im, gt_re, gt_im, a_re, a_im, h_re, h_im)


def _ifft_rows_kernel(c_ref, s_ref, b_re_ref, b_im_ref, y_ref):
    y = _hdot(_split(c_ref[...]), _split(b_re_ref[0])) + _hdot(_split(s_ref[...]), _split(b_im_ref[0]))
    y_ref[0] = y * (1.0 / (FFT_R * FFT_R))


def ifft_rows(b_re, b_im, f_re, f_im, rows):
    bsz, r, m = b_re.shape
    f_spec = pl.BlockSpec((rows, r), lambda b, j: (0, 0))
    b_spec = pl.BlockSpec((1, r, FFT_COLS), lambda b, j: (b, 0, j))
    return pl.pallas_call(
        _ifft_rows_kernel,
        grid=(bsz, m // FFT_COLS),
        in_specs=[f_spec, f_spec, b_spec, b_spec],
        out_specs=pl.BlockSpec((1, rows, FFT_COLS), lambda b, j: (b, 0, j)),
        out_shape=jax.ShapeDtypeStruct((bsz, rows, m), F32),
        compiler_params=pltpu.CompilerParams(dimension_semantics=("parallel", "parallel"), vmem_limit_bytes=VMEM_LIMIT),
        name="ifft_rows",
    )(f_re[:rows], f_im[:rows], b_re, b_im)


def long_conv_fft(u, taps):
    bsz, seq_len, c = u.shape
    r = FFT_R
    f_re, f_im, g_re, g_im = _dft_tables()
    t_re, t_im = fft_rows(taps.reshape(1, r, r * c), f_re, f_im)
    h_re, h_im = fft_spectrum(t_re.reshape(1, r, r, c), t_im.reshape(1, r, r, c), g_re, g_im)
    half = seq_len // r
    a_re, a_im = fft_rows(u.reshape(bsz, half, r * c), f_re[:, :half], f_im[:, :half])
    b_re, b_im = fft_filter_mid(a_re.reshape(bsz, r, r, c), a_im.reshape(bsz, r, r, c), h_re, h_im, g_re, g_im)
    y = ifft_rows(b_re.reshape(bsz, r, r * c), b_im.reshape(bsz, r, r * c), f_re, f_im, half)
    return y.reshape(bsz, seq_len, c)


def hyena_mixer(zh, p):
    zs = depthwise_conv(zh, p['hy_conv_w']) + p['hy_conv_b']
    x0, x1, v = jnp.split(zs, 3, axis=-1)
    z = x1 * v
    seq_len = zh.shape[1]
    h_fwd, h_bwd = hyena_filters(seq_len, p)
    if 2 * seq_len == FFT_R * FFT_R:
        taps = jnp.concatenate([h_fwd[:1] + h_bwd[:1] + p['hy_bias'], h_fwd[1:], jnp.zeros_like(h_fwd[:1]), h_bwd[:0:-1]], axis=0)
        y = x0 * long_conv_fft(z, taps)
    else:
        y = x0 * (two_sided_long_conv(z, h_fwd, h_bwd) + p['hy_bias'] * z)
    return rms_norm(y, p['hy_norm_g'])


def in_projection(h, p, full):
    bsz, n, k = h.shape
    hb = h.reshape(bsz * n, k).astype(BF16)
    out = {}
    out['s5'] = matmul(hb, p['w_in_s5'], name="in_s5").reshape(bsz, n, -1)
    out['qkv'] = matmul(hb, p['w_in_qkv'], name="in_qkv").reshape(bsz, n, -1)
    ab = matmul(hb, p['w_in_ab'], name="in_ab").reshape(bsz, n, -1)
    out['a'] = ab[..., :2 * DN_HEADS]
    out['b'] = ab[..., 2 * DN_HEADS:4 * DN_HEADS]
    if full:
        out['g'] = matmul(hb, p['w_in_g'], name="in_g").reshape(bsz, n, -1)
        out['hy'] = matmul(hb, p['w_in_hy'], name="in_hy").reshape(bsz, n, -1)
    return out


def token_mixer(h, hc, p, need_ctx_out):
    z = in_projection(h, p, True)
    zc = in_projection(hc, p, need_ctx_out)
    s5_y, s5_yc = s5_mixer(z['s5'], zc['s5'], p, need_ctx_out)
    dn_y, dn_yc = deltanet_mixer(z, zc, p, need_ctx_out)
    y = mm3(jnp.concatenate([s5_y, dn_y, hyena_mixer(z['hy'], p)], axis=-1).astype(BF16), p['w_out'], "out_proj")
    if not need_ctx_out:
        return y, None
    yc = mm3(jnp.concatenate([s5_yc, dn_yc, hyena_mixer(zc['hy'], p)], axis=-1).astype(BF16), p['w_out'], "out_proj_ctx")
    return y, yc


def _ada_kernel(a_ref, w_ref, o_ref):
    o_ref[...] = jnp.dot(a_ref[...].astype(BF16), w_ref[0].astype(BF16), preferred_element_type=F32)


def ada_modulation(c_rows, w_ada, b_ada, layer):
    rows, k = c_rows.shape
    n = w_ada.shape[2]
    tn = 1024
    a = jnp.zeros((8, k), F32).at[:rows].set(jax.nn.silu(c_rows))
    out = pl.pallas_call(
        _ada_kernel,
        grid=(n // tn,),
        in_specs=[pl.BlockSpec((8, k), lambda j: (0, 0)), pl.BlockSpec((1, k, tn), lambda j: (layer, 0, j))],
        out_specs=pl.BlockSpec((8, tn), lambda j: (0, j)),
        out_shape=jax.ShapeDtypeStruct((8, n), F32),
        compiler_params=pltpu.CompilerParams(dimension_semantics=("parallel",), vmem_limit_bytes=VMEM_LIMIT),
        name="ada",
    )(a, w_ada)
    return out[:rows] + b_ada


def trunk_layer(x, xc, c, c_ctx, p, last):
    bsz, n, dm = x.shape
    lc = xc.shape[1]
    ada = ada_modulation(jnp.concatenate([c, c_ctx[None]], axis=0), p['w_ada'], p['b_ada'], p['layer'])
    sh1, sc1, g1, sh2, sc2, g2 = jnp.split(ada[:bsz, None, :], 6, axis=-1)
    mc = jnp.split(ada[bsz], 6, axis=-1)
    y, yc = token_mixer(modulate(x, sh1, sc1), modulate(xc, mc[0], mc[1]), p, not last)
    x = layer_norm(DEEPNORM_ALPHA * x + g1 * y, p['ln1_g'], p['ln1_b'])
    if last:
        f = hier_moe(modulate(x, sh2, sc2).reshape(bsz * n, dm), p).reshape(bsz, n, dm)
        return layer_norm(DEEPNORM_ALPHA * x + g2 * f, p['ln2_g'], p['ln2_b']), None
    xc = layer_norm(DEEPNORM_ALPHA * xc + mc[2] * yc, p['ln1_g'], p['ln1_b'])
    tokens = jnp.concatenate([modulate(x, sh2, sc2).reshape(bsz * n, dm), modulate(xc, mc[3], mc[4]).reshape(bsz * lc, dm)], axis=0)
    f = hier_moe(tokens, p)
    x = layer_norm(DEEPNORM_ALPHA * x + g2 * f[:bsz * n].reshape(bsz, n, dm), p['ln2_g'], p['ln2_b'])
    xc = layer_norm(DEEPNORM_ALPHA * xc + mc[5] * f[bsz * n:].reshape(bsz, lc, dm), p['ln2_g'], p['ln2_b'])
    return x, xc


def kernel(x, c, ctx, c_ctx, w_ada, b_ada, w_in, s5_lam_re, s5_lam_im, s5_log_step, s5_b_re, s5_b_im, s5_c_re, s5_c_im, s5_d, s5_glu_w, s5_glu_b, s5_norm_g, dn_conv_w, dn_a_log, dn_dt_bias, dn_norm_g, hy_conv_w, hy_conv_b, hy_f_w1, hy_f_b1, hy_f_w2, hy_f_b2, hy_f_w3, hy_f_b3, hy_f_freq, hy_f_w4, hy_bias, hy_norm_g, w_out, ln1_g, ln1_b, ln2_g, ln2_b, moe_w_group, moe_b_group, moe_w_expert, moe_b_expert, moe_w_gate, moe_w_up, moe_w_down):
    xc = ctx
    for l in range(DEPTH):
        w_in_l = w_in[l]
        w_ab = jnp.zeros((D_MODEL, LANE), F32).at[:, :4 * DN_HEADS].set(w_in_l[:, COL_DN_A:STATE_COLS])
        w_router = jnp.zeros((D_MODEL, LANE), F32).at[:, :N_GROUPS].set(moe_w_group[l]).at[:, N_GROUPS:N_GROUPS + N_EXPERTS].set(moe_w_expert[l])
        p = {
            'layer': l, 'w_ada': w_ada, 'b_ada': b_ada[l],
            'w_in_s5': w_in_l[:, COL_S5:COL_DN_QKV].astype(BF16),
            'w_in_qkv': w_in_l[:, COL_DN_QKV:COL_DN_A].astype(BF16),
            'w_in_ab': w_ab.astype(BF16),
            'w_in_g': w_in_l[:, COL_DN_G:COL_HY].astype(BF16),
            'w_in_hy': w_in_l[:, COL_HY:].astype(BF16),
            's5_lam_re': s5_lam_re[l], 's5_lam_im': s5_lam_im[l], 's5_log_step': s5_log_step[l],
            's5_b_re': s5_b_re[l], 's5_b_im': s5_b_im[l], 's5_c_re': s5_c_re[l], 's5_c_im': s5_c_im[l],
            's5_d': s5_d[l], 's5_glu_w': s5_glu_w[l].astype(BF16), 's5_glu_b': s5_glu_b[l], 's5_norm_g': s5_norm_g[l],
            'dn_conv_w': dn_conv_w[l], 'dn_a_log': dn_a_log[l], 'dn_dt_bias': dn_dt_bias[l], 'dn_norm_g': dn_norm_g[l],
            'hy_conv_w': hy_conv_w[l], 'hy_conv_b': hy_conv_b[l],
            'hy_f_w1': hy_f_w1[l], 'hy_f_b1': hy_f_b1[l], 'hy_f_w2': hy_f_w2[l], 'hy_f_b2': hy_f_b2[l],
            'hy_f_w3': hy_f_w3[l], 'hy_f_b3': hy_f_b3[l], 'hy_f_freq': hy_f_freq[l], 'hy_f_w4': hy_f_w4[l],
            'hy_bias': hy_bias[l], 'hy_norm_g': hy_norm_g[l], 'w_out': w_out[l].astype(BF16),
            'ln1_g': ln1_g[l], 'ln1_b': ln1_b[l], 'ln2_g': ln2_g[l], 'ln2_b': ln2_b[l],
            'moe_w_router': w_router, 'moe_b_group': moe_b_group[l], 'moe_b_expert': moe_b_expert[l],
            'moe_w_gate': moe_w_gate, 'moe_w_up': moe_w_up, 'moe_w_down': moe_w_down,
        }
        x, xc = trunk_layer(x, xc, c, c_ctx, p, l == DEPTH - 1)
    return x
```

```python
import functools
import math

import jax
import jax.numpy as jnp
from jax import lax
from jax.experimental import pallas as pl
from jax.experimental.pallas import tpu as pltpu

D_MODEL = 4096
DEPTH = 2
GRID_W = 64

S5_WIDTH = D_MODEL // 4
S5_GROUP = 16
S5_GROUPS = S5_WIDTH // S5_GROUP
S5_STATE = 64

DN_WIDTH = D_MODEL // 2
DN_HEAD_DIM = 128
DN_HEADS = DN_WIDTH // DN_HEAD_DIM
DN_CHUNK = 64
SHORT_CONV = 3

HY_WIDTH = D_MODEL - S5_WIDTH - DN_WIDTH
HY_BANDS = 16
HY_EMB = 1 + 2 * HY_BANDS
HY_HIDDEN = 64
HY_DECAY_SHORT_PCT = 0.3
HY_DECAY_LONG_PCT = 1.5
HY_DECAY_TARGET = 1e-2

MIX_WIDTH = S5_WIDTH + DN_WIDTH + HY_WIDTH

COL_S5 = 0
COL_DN_QKV = COL_S5 + S5_WIDTH
COL_DN_A = COL_DN_QKV + 3 * DN_WIDTH
COL_DN_B = COL_DN_A + 2 * DN_HEADS
STATE_COLS = COL_DN_B + 2 * DN_HEADS
COL_DN_G = STATE_COLS
COL_HY = COL_DN_G + DN_WIDTH
IN_WIDTH = COL_HY + 3 * HY_WIDTH

N_GROUPS = 4
EXPERTS_PER_GROUP = 8
N_EXPERTS = N_GROUPS * EXPERTS_PER_GROUP
TOP_K = 2
D_EXPERT = 512

DEEPNORM_ALPHA = (2 * DEPTH) ** 0.25
LN_EPS = 1e-5
RMS_EPS = 1e-6
F32 = jnp.float32
BF16 = jnp.bfloat16

V7X_VMEM_BYTES = 64 * 1024 * 1024
VMEM_LIMIT = 52 * 1024 * 1024
MOE_VMEM_LIMIT = 56 * 1024 * 1024
LANE = 128

MOE_TILE = 256
DN_HEAD_GROUP = 4
DN_SPLIT_LEVELS = 2
S5_T = 64


def _mm_kernel(a_ref, b_ref, o_ref):
    o_ref[...] = jnp.dot(a_ref[...].astype(BF16), b_ref[...].astype(BF16), preferred_element_type=F32).astype(o_ref.dtype)


def _pick(n, prefs):
    for p in prefs:
        if n % p == 0:
            return p
    return n


def matmul(a, b, out_dtype=F32, name="matmul"):
    m, k = a.shape
    n = b.shape[1]
    tm = _pick(m, (512, 256, 128, 64, 32, 16, 8))
    tn = _pick(n, (1024, 512, 256, 128))
    return pl.pallas_call(
        _mm_kernel,
        grid=(m // tm, n // tn),
        in_specs=[pl.BlockSpec((tm, k), lambda i, j: (i, 0)), pl.BlockSpec((k, tn), lambda i, j: (0, j))],
        out_specs=pl.BlockSpec((tm, tn), lambda i, j: (i, j)),
        out_shape=jax.ShapeDtypeStruct((m, n), out_dtype),
        compiler_params=pltpu.CompilerParams(dimension_semantics=("parallel", "parallel"), vmem_limit_bytes=VMEM_LIMIT),
        name=name,
    )(a, b)


def _mm_f32_kernel(a_ref, b_ref, o_ref):
    o_ref[...] = _hdot(_split(a_ref[...]), _split(b_ref[...]))


def matmul_f32(a, b, name="matmul_f32"):
    m, k = a.shape
    n = b.shape[1]
    tm = _pick(m, (512, 256, 128, 64, 32, 16, 8))
    return pl.pallas_call(
        _mm_f32_kernel,
        grid=(m // tm,),
        in_specs=[pl.BlockSpec((tm, k), lambda i: (i, 0)), pl.BlockSpec((k, n), lambda i: (0, 0))],
        out_specs=pl.BlockSpec((tm, n), lambda i: (i, 0)),
        out_shape=jax.ShapeDtypeStruct((m, n), F32),
        compiler_params=pltpu.CompilerParams(dimension_semantics=("parallel",), vmem_limit_bytes=VMEM_LIMIT),
        name=name,
    )(a, b)


def _moe_kernel(tile_expert_ref, n_used_ref, x_ref, w_ref, wg_ref, wu_ref, wd_ref, o_ref, wg_s, wu_s, wd_s):
    i = pl.program_id(0)
    prev = tile_expert_ref[jnp.maximum(i - 1, 0)]

    @pl.when((i == 0) | (tile_expert_ref[i] != prev))
    def _():
        wg_s[...] = wg_ref[0, 0].astype(BF16)
        wu_s[...] = wu_ref[0, 0].astype(BF16)
        wd_s[...] = wd_ref[0, 0].astype(BF16)

    @pl.when(i < n_used_ref[0])
    def _():
        x = x_ref[...]
        g = jnp.dot(x, wg_s[...], preferred_element_type=F32)
        u = jnp.dot(x, wu_s[...], preferred_element_type=F32)
        h = (g * jax.nn.sigmoid(g)) * u * w_ref[...]
        o_ref[...] = jnp.dot(h.astype(BF16), wd_s[...], preferred_element_type=F32).astype(o_ref.dtype)

    @pl.when(i >= n_used_ref[0])
    def _():
        o_ref[...] = jnp.zeros_like(o_ref)


def moe_experts(tile_expert, n_used, xs, row_w, wg, wu, wd, layer):
    p_rows, dm = xs.shape
    n_tiles = p_rows // MOE_TILE
    single = pl.Buffered(1)
    grid_spec = pltpu.PrefetchScalarGridSpec(
        num_scalar_prefetch=2,
        grid=(n_tiles,),
        in_specs=[
            pl.BlockSpec((MOE_TILE, dm), lambda i, te, nu: (i, 0)),
            pl.BlockSpec((MOE_TILE, 1), lambda i, te, nu: (i, 0)),
            pl.BlockSpec((1, 1, dm, D_EXPERT), lambda i, te, nu: (layer, te[i], 0, 0), pipeline_mode=single),
            pl.BlockSpec((1, 1, dm, D_EXPERT), lambda i, te, nu: (layer, te[i], 0, 0), pipeline_mode=single),
            pl.BlockSpec((1, 1, D_EXPERT, dm), lambda i, te, nu: (layer, te[i], 0, 0), pipeline_mode=single),
        ],
        out_specs=pl.BlockSpec((MOE_TILE, dm), lambda i, te, nu: (i, 0)),
        scratch_shapes=[pltpu.VMEM((dm, D_EXPERT), BF16), pltpu.VMEM((dm, D_EXPERT), BF16), pltpu.VMEM((D_EXPERT, dm), BF16)],
    )
    return pl.pallas_call(
        _moe_kernel,
        grid_spec=grid_spec,
        out_shape=jax.ShapeDtypeStruct((p_rows, dm), BF16),
        compiler_params=pltpu.CompilerParams(dimension_semantics=("arbitrary",), vmem_limit_bytes=MOE_VMEM_LIMIT),
        name="moe_experts",
    )(tile_expert, n_used, xs, row_w, wg, wu, wd)


def hier_moe(t, p):
    n_tok = t.shape[0]
    logits = matmul_f32(t, p['moe_w_router'], name="moe_router")
    group_logits = logits[:, :N_GROUPS] + p['moe_b_group']
    group = jnp.argmax(group_logits, axis=-1)
    group_w = jnp.max(jax.nn.softmax(group_logits, axis=-1), axis=-1, keepdims=True)
    exp_logits = (logits[:, N_GROUPS:N_GROUPS + N_EXPERTS] + p['moe_b_expert']).reshape(n_tok, N_GROUPS, EXPERTS_PER_GROUP)
    exp_logits = jnp.take_along_axis(exp_logits, group[:, None, None], axis=1)[:, 0]
    top_logits, top_idx = lax.top_k(exp_logits, TOP_K)
    top_w = jax.nn.softmax(top_logits, axis=-1) * group_w
    expert_id = (group[:, None] * EXPERTS_PER_GROUP + top_idx).astype(jnp.int32)

    n_rows = n_tok * TOP_K
    flat_e = expert_id.reshape(n_rows)
    flat_w = top_w.reshape(n_rows)
    flat_t = jnp.arange(n_rows, dtype=jnp.int32) // TOP_K
    order = jnp.argsort(flat_e, stable=True)
    sorted_e = flat_e[order]
    counts = jnp.zeros((N_EXPERTS,), jnp.int32).at[flat_e].add(1)
    padded = ((counts + MOE_TILE - 1) // MOE_TILE) * MOE_TILE
    pad_end = jnp.cumsum(padded)
    pad_start = pad_end - padded
    start = jnp.cumsum(counts) - counts
    dest = pad_start[sorted_e] + (jnp.arange(n_rows, dtype=jnp.int32) - start[sorted_e])
    p_rows = n_rows + N_EXPERTS * MOE_TILE
    n_tiles = p_rows // MOE_TILE
    row_token = jnp.zeros((p_rows,), jnp.int32).at[dest].set(flat_t[order])
    row_w = jnp.zeros((p_rows,), F32).at[dest].set(flat_w[order])
    pos = jnp.zeros((n_rows,), jnp.int32).at[order].set(dest).reshape(n_tok, TOP_K)
    n_used = (pad_end[-1] // MOE_TILE).astype(jnp.int32)
    tile_idx = jnp.minimum(jnp.arange(n_tiles, dtype=jnp.int32), n_used - 1)
    tile_expert = jnp.minimum(jnp.searchsorted(pad_end, tile_idx * MOE_TILE, side='right'), N_EXPERTS - 1).astype(jnp.int32)

    xs = t.astype(BF16)[row_token]
    ys = moe_experts(tile_expert, n_used.reshape(1), xs, row_w[:, None], p['moe_w_gate'], p['moe_w_up'], p['moe_w_down'], p['layer'])
    return ys[pos[:, 0]].astype(F32) + ys[pos[:, 1]].astype(F32)


def _flip(t, rev):
    return jnp.flip(t, axis=1) if rev else t


def layer_norm(x, g, b):
    mu = jnp.mean(x, axis=-1, keepdims=True)
    var = jnp.mean(jnp.square(x - mu), axis=-1, keepdims=True)
    return (x - mu) * lax.rsqrt(var + LN_EPS) * g + b


def rms_norm(x, g):
    return x * lax.rsqrt(jnp.mean(jnp.square(x), axis=-1, keepdims=True) + RMS_EPS) * g


def l2_normalize(t):
    return t * lax.rsqrt(jnp.sum(t * t, axis=-1, keepdims=True) + 1e-6)


def depthwise_conv(u, w):
    k = w.shape[0]
    return lax.conv_general_dilated(u, w[:, None, :], window_strides=(1,), padding=[(k // 2, k // 2)], dimension_numbers=('NWC', 'WIO', 'NWC'), feature_group_count=u.shape[-1])


def modulate(t, shift, scale):
    return t * (1 + scale) + shift


def mm3(t, w, name):
    bsz, n, k = t.shape
    return matmul(t.reshape(bsz * n, k), w, name=name).reshape(bsz, n, w.shape[1])


def s5_discretize(lam_re, lam_im, log_step, b_re, b_im):
    step = jnp.exp(log_step)[:, None]
    mag = jnp.exp(lam_re * step)
    a_re = mag * jnp.cos(lam_im * step)
    a_im = mag * jnp.sin(lam_im * step)
    den = lam_re * lam_re + lam_im * lam_im
    f_re = ((a_re - 1.0) * lam_re + a_im * lam_im) / den
    f_im = (a_im * lam_re - (a_re - 1.0) * lam_im) / den
    bb_re = f_re[..., None] * b_re - f_im[..., None] * b_im
    bb_im = f_re[..., None] * b_im + f_im[..., None] * b_re
    return a_re, a_im, bb_re, bb_im


def _cmul(x, y):
    return x[0] * y[0] - x[1] * y[1], x[0] * y[1] + x[1] * y[0]


def s5_tables(p):
    hp = lax.Precision.HIGHEST
    t_len, grp, st = S5_T, S5_GROUPS, S5_STATE
    taps, wst, rd, pq = [], [], [], []
    for d in range(2):
        a_re, a_im, bb_re, bb_im = s5_discretize(p['s5_lam_re'][d], p['s5_lam_im'][d], p['s5_log_step'][d], p['s5_b_re'][d], p['s5_b_im'][d])
        c_re, c_im = p['s5_c_re'][d], p['s5_c_im'][d]
        pr, pi = lax.associative_scan(_cmul, (jnp.broadcast_to(a_re, (t_len, grp, st)), jnp.broadcast_to(a_im, (t_len, grp, st))), axis=0)
        pw_re = jnp.concatenate([jnp.ones((1, grp, st), F32), pr], axis=0)
        pw_im = jnp.concatenate([jnp.zeros((1, grp, st), F32), pi], axis=0)
        ca_re = c_re[None] * pw_re[:, :, None, :] - c_im[None] * pw_im[:, :, None, :]
        ca_im = c_re[None] * pw_im[:, :, None, :] + c_im[None] * pw_re[:, :, None, :]
        taps.append(jnp.einsum('tgip,gpj->tgij', ca_re[:t_len], bb_re, precision=hp) - jnp.einsum('tgip,gpj->tgij', ca_im[:t_len], bb_im, precision=hp))
        e_st = jnp.arange(t_len - 1, -1, -1) if d == 0 else jnp.arange(t_len)
        w_re = pw_re[e_st][..., None] * bb_re[None] - pw_im[e_st][..., None] * bb_im[None]
        w_im = pw_re[e_st][..., None] * bb_im[None] + pw_im[e_st][..., None] * bb_re[None]
        to_rows = lambda w: jnp.transpose(w, (1, 0, 3, 2)).reshape(grp, t_len * S5_GROUP, st)
        wst += [to_rows(w_re), to_rows(w_im)]
        e_rd = jnp.arange(1, t_len + 1) if d == 0 else jnp.arange(t_len, 0, -1)
        to_cols = lambda r: jnp.transpose(r, (1, 3, 0, 2)).reshape(grp, st, t_len * S5_GROUP)
        rd += [to_cols(ca_re[e_rd]), to_cols(-ca_im[e_rd])]
        lv_re, lv_im = pw_re[t_len], pw_im[t_len]
        for _ in range(8):
            pq += [jnp.concatenate([lv_re, lv_re], axis=-1), jnp.concatenate([-lv_im, lv_im], axis=-1)]
            lv_re, lv_im = _cmul((lv_re, lv_im), (lv_re, lv_im))
    kf, kb = taps
    blocks = jnp.concatenate([kb[1:][::-1], (kf[0] + kb[0])[None], kf[1:], jnp.zeros_like(kf[:1])], axis=0)
    ext = jnp.transpose(blocks, (1, 3, 0, 2)).reshape(grp, S5_GROUP, 2 * t_len * S5_GROUP)
    return ext, jnp.concatenate(wst, axis=-1).astype(BF16), jnp.concatenate(rd, axis=1).astype(BF16), jnp.stack(pq, axis=1)


def _s5_kernel(u_ref, ext_ref, wst_ref, rd_ref, pq_ref, h0_ref, y_ref, hfin_ref, m_ref, *, n_chunks, bsz):
    cw = S5_T * S5_GROUP
    st2 = 2 * S5_STATE
    ext = ext_ref[0]
    for s in range(S5_T):
        off = S5_GROUP * (S5_T - 1 - s)
        shifted = ext if off == 0 else pltpu.roll(ext, shift=2 * cw - off, axis=1)
        m_ref[S5_GROUP * s:S5_GROUP * (s + 1), :] = shifted[:, :cw].astype(BF16)
    u = u_ref[0]
    rows = u.shape[0]
    y = jnp.dot(u, m_ref[...], preferred_element_type=F32)
    s_all = jnp.dot(u, wst_ref[0], preferred_element_type=F32)
    sf, sb = s_all[:, :st2], s_all[:, st2:]
    row = lax.broadcasted_iota(jnp.int32, (rows, st2), 0)
    c_idx = row % n_chunks

    def cmul_rows(idx, x):
        return pq_ref[0, idx:idx + 1, :] * x + pq_ref[0, idx + 1:idx + 2, :] * pltpu.roll(x, shift=S5_STATE, axis=1)

    h0 = h0_ref[0]
    h0f = jnp.zeros((rows, st2), F32)
    h0b = jnp.zeros((rows, st2), F32)
    for b in range(bsz):
        in_b = (row >= b * n_chunks) & (row < (b + 1) * n_chunks)
        h0f = jnp.where(in_b, h0[b:b + 1, :st2], h0f)
        h0b = jnp.where(in_b, h0[b:b + 1, st2:], h0b)
    hf = jnp.where(c_idx == 0, h0f, pltpu.roll(sf, shift=1, axis=0))
    hb = jnp.where(c_idx == n_chunks - 1, h0b, pltpu.roll(sb, shift=rows - 1, axis=0))
    level, sh = 0, 1
    while sh < n_chunks:
        dn = jnp.where(c_idx >= sh, pltpu.roll(hf, shift=sh, axis=0), 0.0)
        up = jnp.where(c_idx < n_chunks - sh, pltpu.roll(hb, shift=rows - sh, axis=0), 0.0)
        hf = hf + cmul_rows(2 * level, dn)
        hb = hb + cmul_rows(16 + 2 * level, up)
        level, sh = level + 1, sh * 2
    h_in = jnp.concatenate([hf, hb], axis=1)
    y_ref[0] = y + jnp.dot(h_in.astype(BF16), rd_ref[0], preferred_element_type=F32)
    hfin_ref[0] = jnp.concatenate([cmul_rows(0, hf) + sf, cmul_rows(16, hb) + sb], axis=1)


def s5_scan(u, tables, h0, n_chunks, bsz):
    ext, wst, rd, pq = tables
    grp, rows, cw = u.shape
    st4 = 4 * S5_STATE
    blk = lambda *shape: pl.BlockSpec((1,) + shape, lambda i: (i, 0, 0))
    return pl.pallas_call(
        functools.partial(_s5_kernel, n_chunks=n_chunks, bsz=bsz),
        grid=(grp,),
        in_specs=[blk(rows, cw), blk(S5_GROUP, 2 * cw), blk(cw, st4), blk(st4, cw), blk(32, 2 * S5_STATE), blk(bsz, st4)],
        out_specs=[blk(rows, cw), blk(rows, st4)],
        out_shape=[jax.ShapeDtypeStruct((grp, rows, cw), F32), jax.ShapeDtypeStruct((grp, rows, st4), F32)],
        scratch_shapes=[pltpu.VMEM((cw, cw), BF16)],
        compiler_params=pltpu.CompilerParams(dimension_semantics=("parallel",), vmem_limit_bytes=VMEM_LIMIT),
        name="s5_scan",
    )(u, ext, wst, rd, pq, h0)


def s5_sequence(u, tables, h0):
    bsz, n, _ = u.shape
    n_chunks = n // S5_T
    rows = bsz * n_chunks
    rows_pad = -(-rows // 16) * 16
    ug = u.astype(BF16).reshape(bsz, n_chunks, S5_T, S5_GROUPS, S5_GROUP)
    ug = jnp.transpose(ug, (3, 0, 1, 2, 4)).reshape(S5_GROUPS, rows, S5_T * S5_GROUP)
    if rows_pad != rows:
        ug = jnp.pad(ug, ((0, 0), (0, rows_pad - rows), (0, 0)))
    y, hfin = s5_scan(ug, tables, h0, n_chunks, bsz)
    y = y[:, :rows].reshape(S5_GROUPS, bsz, n_chunks, S5_T, S5_GROUP)
    y = jnp.transpose(y, (1, 2, 3, 0, 4)).reshape(bsz, n, S5_WIDTH)
    hfin = hfin[:, :rows].reshape(S5_GROUPS, bsz, n_chunks, 4 * S5_STATE)
    st2 = 2 * S5_STATE
    h_end = jnp.concatenate([hfin[:, :, n_chunks - 1, :st2], hfin[:, :, 0, st2:]], axis=-1)
    return y, h_end


def s5_glu(y, p):
    y = jax.nn.gelu(y)
    y = y * jax.nn.sigmoid(mm3(y, p['s5_glu_w'], "s5_glu") + p['s5_glu_b'])
    return rms_norm(y, p['s5_norm_g'])


def s5_mixer(u, uc, p, need_ctx_out):
    bsz = u.shape[0]
    tables = s5_tables(p)
    yc, h_ctx = s5_sequence(uc, tables, jnp.zeros((S5_GROUPS, bsz, 4 * S5_STATE), F32))
    y, _ = s5_sequence(u, tables, h_ctx)
    out = s5_glu(u * p['s5_d'] + y, p)
    out_c = s5_glu(uc * p['s5_d'] + yc, p) if need_ctx_out else None
    return out, out_c


def dn_features(z_qkv, conv_w):
    bsz, seq_len, _ = z_qkv.shape
    q, k, v = jnp.split(jax.nn.silu(depthwise_conv(z_qkv, conv_w)), 3, axis=-1)
    shp = (bsz, seq_len, DN_HEADS, DN_HEAD_DIM)
    return l2_normalize(q.reshape(shp)) * DN_HEAD_DIM ** -0.5, l2_normalize(k.reshape(shp)), v.reshape(shp)


def dn_gates(z_a, z_b, a_log, dt_bias):
    bsz, seq_len, _ = z_a.shape
    shp = (bsz, seq_len, 2, DN_HEADS)
    g = -jnp.exp(a_log) * jax.nn.softplus(z_a.reshape(shp) + dt_bias)
    beta = jax.nn.sigmoid(z_b.reshape(shp))
    return g, beta


def _dn_kernel(q_ref, k_ref, v_ref, g_ref, gt_ref, b_ref, s0_ref, o_ref, sfin_ref, s_ref, *, n_chunks):
    cs, hd, hg = DN_CHUNK, DN_HEAD_DIM, DN_HEAD_GROUP
    rows = cs * hg
    n_groups = DN_HEADS // hg
    groups = range(n_groups)
    step = pl.program_id(1)

    @pl.when(step == 0)
    def _():
        s_ref[...] = s0_ref[0]

    r_i = lax.broadcasted_iota(jnp.int32, (rows, rows), 0)
    c_i = lax.broadcasted_iota(jnp.int32, (rows, rows), 1)
    same_head = (r_i // cs) == (c_i // cs)
    incl = same_head & ((r_i % cs) >= (c_i % cs))
    strict = same_head & ((r_i % cs) > (c_i % cs))
    upto = same_head & ((r_i % cs) <= (c_i % cs))
    r_c = lax.broadcasted_iota(jnp.int32, (cs, cs), 0)
    c_c = lax.broadcasted_iota(jnp.int32, (cs, cs), 1)
    hp = lax.Precision.HIGHEST
    beta = b_ref[0, 0]
    gc = jnp.dot((r_c >= c_c).astype(F32), g_ref[0, 0], preferred_element_type=F32, precision=hp)
    gc_row = jnp.dot(gt_ref[0, 0], upto.astype(F32), preferred_element_type=F32, precision=hp)
    g_tot = gc[cs - 1:cs, :]
    e_g = jnp.exp(gc)
    e_tail = jnp.exp(g_tot - gc)
    e_tot = jnp.exp(g_tot)

    def head_cols(h):
        return slice(h * hd, (h + 1) * hd)

    def stack_heads(grp, fn):
        return jnp.concatenate([fn(grp * hg + j) for j in range(hg)], axis=0)

    def col(x, h, width):
        return jnp.broadcast_to(x[:, h:h + 1], (cs, width))

    def mm(a, b):
        return jnp.dot(a.astype(BF16), b.astype(BF16), preferred_element_type=F32)

    def mm_split_rhs(a_bf16, b):
        hi = b.astype(BF16)
        lo = (b - hi.astype(F32)).astype(BF16)
        t = jnp.dot(a_bf16, jnp.concatenate([hi, lo], axis=1), preferred_element_type=F32)
        return t[:, :b.shape[1]] + t[:, b.shape[1]:]

    kst = [stack_heads(gr, lambda h: k_ref[0, :, head_cols(h)]) for gr in groups]
    qst = [stack_heads(gr, lambda h: q_ref[0, :, head_cols(h)]) for gr in groups]
    b_col = [stack_heads(gr, lambda h: col(beta, h, hd)) for gr in groups]
    eg_col = [stack_heads(gr, lambda h: col(e_g, h, hd)) for gr in groups]
    kb = [kst[gr] * b_col[gr] for gr in groups]
    decay = []
    for gr in groups:
        diff = stack_heads(gr, lambda h: col(gc, h, rows)) - gc_row[gr:gr + 1, :]
        decay.append(jnp.where(incl, jnp.exp(jnp.where(incl, diff, 0.0)), 0.0))
    kq = [lax.dot_general(jnp.concatenate([kb[gr], qst[gr]], axis=0).astype(BF16), kst[gr].astype(BF16),
                          (((1,), (1,)), ((), ())), preferred_element_type=F32) for gr in groups]
    n = [jnp.where(strict, kq[gr][:rows] * decay[gr], 0.0).astype(BF16) for gr in groups]
    qk = [(kq[gr][rows:] * decay[gr]).astype(BF16) for gr in groups]
    x = [jnp.concatenate([stack_heads(gr, lambda h: v_ref[0, :, head_cols(h)]) * b_col[gr], kb[gr] * eg_col[gr]], axis=1) for gr in groups]
    x = [x[gr] - mm_split_rhs(n[gr], x[gr]) for gr in groups]
    pw = n
    for level in range(5):
        pw = [jnp.dot(pw[gr], pw[gr], preferred_element_type=F32).astype(BF16) for gr in groups]
        if level < DN_SPLIT_LEVELS:
            x = [x[gr] + mm_split_rhs(pw[gr], x[gr]) for gr in groups]
        else:
            x = [x[gr] + jnp.dot(pw[gr], x[gr].astype(BF16), preferred_element_type=F32) for gr in groups]
    qd = [qst[gr] * eg_col[gr] for gr in groups]
    for gr in groups:
        v_new, o_state = [], []
        for j in range(hg):
            h, rs = gr * hg + j, slice(j * cs, (j + 1) * cs)
            ws = mm(jnp.concatenate([x[gr][rs, hd:], qd[gr][rs]], axis=0), s_ref[h])
            v_new.append(x[gr][rs, :hd] - ws[:cs])
            o_state.append(ws[cs:])
        v_all = jnp.concatenate(v_new, axis=0).astype(BF16)
        o_all = jnp.concatenate(o_state, axis=0) + jnp.dot(qk[gr], v_all, preferred_element_type=F32)
        for j in range(hg):
            h, rs = gr * hg + j, slice(j * cs, (j + 1) * cs)
            o_ref[0, :, head_cols(h)] = o_all[rs]
            kt = (kst[gr][rs] * col(e_tail, h, hd)).astype(BF16)
            s_ref[h] = s_ref[h] * e_tot[:, h:h + 1] + lax.dot_general(kt, v_all[rs], (((0,), (0,)), ((), ())), preferred_element_type=F32)

    @pl.when(step == n_chunks - 1)
    def _():
        sfin_ref[0] = s_ref[...]


def gated_delta_rule(q, k, v, g, beta, s0):
    bsz, seq_len, heads, hd = q.shape
    width = heads * hd
    n_chunks = seq_len // DN_CHUNK
    q, k, v = (t.reshape(bsz, seq_len, width) for t in (q, k, v))
    g4 = g.reshape(bsz, n_chunks, DN_CHUNK, heads)
    gt4 = jnp.swapaxes(g4, 2, 3).reshape(bsz, n_chunks, heads // DN_HEAD_GROUP, DN_HEAD_GROUP * DN_CHUNK)
    b4 = beta.reshape(bsz, n_chunks, DN_CHUNK, heads)
    tok_spec = pl.BlockSpec((1, DN_CHUNK, width), lambda b, i: (b, i, 0))
    gate_spec = pl.BlockSpec((1, 1, DN_CHUNK, heads), lambda b, i: (b, i, 0, 0))
    gate_t_spec = pl.BlockSpec((1, 1, heads // DN_HEAD_GROUP, DN_HEAD_GROUP * DN_CHUNK), lambda b, i: (b, i, 0, 0))
    state_spec = pl.BlockSpec((1, heads, hd, hd), lambda b, i: (b, 0, 0, 0))
    o, s_fin = pl.pallas_call(
        functools.partial(_dn_kernel, n_chunks=n_chunks),
        grid=(bsz, n_chunks),
        in_specs=[tok_spec, tok_spec, tok_spec, gate_spec, gate_t_spec, gate_spec, state_spec],
        out_specs=[tok_spec, state_spec],
        out_shape=[jax.ShapeDtypeStruct((bsz, seq_len, width), F32), jax.ShapeDtypeStruct(s0.shape, F32)],
        scratch_shapes=[pltpu.VMEM((heads, hd, hd), F32)],
        compiler_params=pltpu.CompilerParams(dimension_semantics=("parallel", "arbitrary"), vmem_limit_bytes=VMEM_LIMIT),
        name="delta_rule",
    )(q, k, v, g4, gt4, b4, s0)
    return o.reshape(bsz, seq_len, heads, hd), s_fin


def dn_output(o, z_gate, p):
    bsz, seq_len = o.shape[:2]
    gate = jax.nn.silu(z_gate).reshape(bsz, seq_len, DN_HEADS, DN_HEAD_DIM)
    return (rms_norm(o, p['dn_norm_g']) * gate).reshape(bsz, seq_len, DN_WIDTH)


def deltanet_mixer(z, zc, p, need_ctx_out):
    q, k, v = dn_features(z['qkv'], p['dn_conv_w'])
    g, beta = dn_gates(z['a'], z['b'], p['dn_a_log'], p['dn_dt_bias'])
    qc, kc, vc = dn_features(zc['qkv'], p['dn_conv_w'])
    gc, betac = dn_gates(zc['a'], zc['b'], p['dn_a_log'], p['dn_dt_bias'])
    s0 = jnp.zeros((q.shape[0], DN_HEADS, DN_HEAD_DIM, DN_HEAD_DIM), F32)
    o, oc = 0.0, 0.0
    for d in range(2):
        rev = d == 1
        oc_d, s_ctx = gated_delta_rule(*(_flip(t, rev) for t in (qc, kc, vc, gc[:, :, d], betac[:, :, d])), s0)
        o_d, _ = gated_delta_rule(*(_flip(t, rev) for t in (q, k, v, g[:, :, d], beta[:, :, d])), s_ctx)
        o = o + _flip(o_d, rev)
        if need_ctx_out:
            oc = oc + _flip(oc_d, rev)
    y = dn_output(o, z['g'], p)
    yc = dn_output(oc, zc['g'], p) if need_ctx_out else None
    return y, yc


def hyena_filters(seq_len, p):
    hp = lax.Precision.HIGHEST
    t = jnp.linspace(0.0, 1.0, seq_len, dtype=F32)[:, None]
    w = (2.0 * math.pi / seq_len) * jnp.arange(seq_len, dtype=F32)[:, None]
    f = jnp.linspace(1e-4, HY_BANDS - 1, HY_BANDS, dtype=F32)[None, :]
    feats = jnp.concatenate([t, jnp.cos(w * f), -jnp.sin(w * f)], axis=-1)
    freq = p['hy_f_freq']
    h = jnp.sin(freq * (jnp.dot(feats, p['hy_f_w1'], precision=hp) + p['hy_f_b1']))
    h = jnp.sin(freq * (jnp.dot(h, p['hy_f_w2'], precision=hp) + p['hy_f_b2']))
    h = jnp.sin(freq * (jnp.dot(h, p['hy_f_w3'], precision=hp) + p['hy_f_b3']))
    h = jnp.dot(h, p['hy_f_w4'], precision=hp).reshape(seq_len, 2, HY_WIDTH)
    rates = jnp.linspace(math.log(HY_DECAY_TARGET) / HY_DECAY_LONG_PCT, math.log(HY_DECAY_TARGET) / HY_DECAY_SHORT_PCT, HY_WIDTH, dtype=F32)
    h = h * jnp.exp(-t * jnp.abs(rates))[:, None, :]
    return h[:, 0], h[:, 1]


def two_sided_long_conv(u, h_fwd, h_bwd):
    seq_len = u.shape[1]
    taps = jnp.concatenate([h_fwd[:1] + h_bwd[:1], h_fwd[1:], jnp.zeros_like(h_fwd[:1]), h_bwd[:0:-1]], axis=0)
    u_f = jnp.fft.rfft(u, n=2 * seq_len, axis=1)
    t_f = jnp.fft.rfft(taps, axis=0)
    return jnp.fft.irfft(u_f * t_f[None], n=2 * seq_len, axis=1)[:, :seq_len]


FFT_R = 128
FFT_COLS = 8192


def _dft_tables():
    n, big = FFT_R, FFT_R * FFT_R
    idx = jnp.arange(n, dtype=jnp.int32)
    ang = (2.0 * math.pi / n) * ((idx[:, None] * idx[None, :]) % n).astype(F32)
    m = (idx[None, None, :] * (n * idx[None, :, None] + idx[:, None, None])) % big
    g_ang = (2.0 * math.pi / big) * m.astype(F32)
    return jnp.cos(ang), -jnp.sin(ang), jnp.cos(g_ang), -jnp.sin(g_ang)


def _split(x):
    hi = x.astype(BF16)
    return hi, (x - hi.astype(F32)).astype(BF16)


def _hdot(a, b):
    (a_hi, a_lo), (b_hi, b_lo) = a, b
    m = a_hi.shape[0]
    t = jnp.dot(jnp.concatenate([a_hi, a_lo], axis=0), b_hi, preferred_element_type=F32)
    return t[:m] + t[m:] + jnp.dot(a_hi, b_lo, preferred_element_type=F32)


def _fft_rows_kernel(f_re_ref, f_im_ref, x_ref, a_re_ref, a_im_ref):
    x = _split(x_ref[0])
    a_re_ref[0] = _hdot(_split(f_re_ref[...]), x)
    a_im_ref[0] = _hdot(_split(f_im_ref[...]), x)


def fft_rows(x, f_re, f_im):
    bsz, k, m = x.shape
    out = jax.ShapeDtypeStruct((bsz, FFT_R, m), F32)
    f_spec = pl.BlockSpec((FFT_R, k), lambda b, j: (0, 0))
    o_spec = pl.BlockSpec((1, FFT_R, FFT_COLS), lambda b, j: (b, 0, j))
    return pl.pallas_call(
        _fft_rows_kernel,
        grid=(bsz, m // FFT_COLS),
        in_specs=[f_spec, f_spec, pl.BlockSpec((1, k, FFT_COLS), lambda b, j: (b, 0, j))],
        out_specs=[o_spec, o_spec],
        out_shape=[out, out],
        compiler_params=pltpu.CompilerParams(dimension_semantics=("parallel", "parallel"), vmem_limit_bytes=VMEM_LIMIT),
        name="fft_rows",
    )(f_re, f_im, x)


def _fft_spectrum_kernel(g_re_ref, g_im_ref, a_re_ref, a_im_ref, x_re_ref, x_im_ref):
    gr, gi, ar, ai = (_split(t) for t in (g_re_ref[0], g_im_ref[0], a_re_ref[0, 0], a_im_ref[0, 0]))
    x_re_ref[0, 0] = _hdot(gr, ar) - _hdot(gi, ai)
    x_im_ref[0, 0] = _hdot(gr, ai) + _hdot(gi, ar)


def _fft_mid_kernel(g_re_ref, g_im_ref, gt_re_ref, gt_im_ref, a_re_ref, a_im_ref, h_re_ref, h_im_ref, b_re_ref, b_im_ref):
    gr, gi, ar, ai = (_split(t) for t in (g_re_ref[0], g_im_ref[0], a_re_ref[0, 0], a_im_ref[0, 0]))
    xr = _hdot(gr, ar) - _hdot(gi, ai)
    xi = _hdot(gr, ai) + _hdot(gi, ar)
    hr, hi = h_re_ref[0, 0], h_im_ref[0, 0]
    yr = _split(xr * hr - xi * hi)
    yi = _split(xr * hi + xi * hr)
    tr, ti = _split(gt_re_ref[0]), _split(gt_im_ref[0])
    b_re_ref[0, 0] = _hdot(tr, yr) + _hdot(ti, yi)
    b_im_ref[0, 0] = _hdot(tr, yi) - _hdot(ti, yr)


def fft_spectrum(a_re, a_im, g_re, g_im):
    bsz, r, _, c = a_re.shape
    g_spec = pl.BlockSpec((1, r, r), lambda k, b: (k, 0, 0))
    a_spec = pl.BlockSpec((1, 1, r, c), lambda k, b: (b, k, 0, 0))
    out = jax.ShapeDtypeStruct(a_re.shape, F32)
    return pl.pallas_call(
        _fft_spectrum_kernel,
        grid=(r, bsz),
        in_specs=[g_spec, g_spec, a_spec, a_spec],
        out_specs=[a_spec, a_spec],
        out_shape=[out, out],
        compiler_params=pltpu.CompilerParams(dimension_semantics=("parallel", "parallel"), vmem_limit_bytes=VMEM_LIMIT),
        name="fft_spectrum",
    )(g_re, g_im, a_re, a_im)


def fft_filter_mid(a_re, a_im, h_re, h_im, g_re, g_im):
    bsz, r, _, c = a_re.shape
    gt_re, gt_im = jnp.swapaxes(g_re, 1, 2), jnp.swapaxes(g_im, 1, 2)
    g_spec = pl.BlockSpec((1, r, r), lambda k, b: (k, 0, 0))
    a_spec = pl.BlockSpec((1, 1, r, c), lambda k, b: (b, k, 0, 0))
    h_spec = pl.BlockSpec((1, 1, r, c), lambda k, b: (0, k, 0, 0))
    out = jax.ShapeDtypeStruct(a_re.shape, F32)
    return pl.pallas_call(
        _fft_mid_kernel,
        grid=(r, bsz),
        in_specs=[g_spec, g_spec, g_spec, g_spec, a_spec, a_spec, h_spec, h_spec],
        out_specs=[a_spec, a_spec],
        out_shape=[out, out],
        compiler_params=pltpu.CompilerParams(dimension_semantics=("parallel", "parallel"), vmem_limit_bytes=VMEM_LIMIT),
        name="fft_filter_mid",
    )(g_re, g_im, gt_re, gt_im, a_re, a_im, h_re, h_im)


def _ifft_rows_kernel(c_ref, s_ref, b_re_ref, b_im_ref, y_ref):
    y = _hdot(_split(c_ref[...]), _split(b_re_ref[0])) + _hdot(_split(s_ref[...]), _split(b_im_ref[0]))
    y_ref[0] = y * (1.0 / (FFT_R * FFT_R))


def ifft_rows(b_re, b_im, f_re, f_im, rows):
    bsz, r, m = b_re.shape
    f_spec = pl.BlockSpec((rows, r), lambda b, j: (0, 0))
    b_spec = pl.BlockSpec((1, r, FFT_COLS), lambda b, j: (b, 0, j))
    return pl.pallas_call(
        _ifft_rows_kernel,
        grid=(bsz, m // FFT_COLS),
        in_specs=[f_spec, f_spec, b_spec, b_spec],
        out_specs=pl.BlockSpec((1, rows, FFT_COLS), lambda b, j: (b, 0, j)),
        out_shape=jax.ShapeDtypeStruct((bsz, rows, m), F32),
        compiler_params=pltpu.CompilerParams(dimension_semantics=("parallel", "parallel"), vmem_limit_bytes=VMEM_LIMIT),
        name="ifft_rows",
    )(f_re[:rows], f_im[:rows], b_re, b_im)


def long_conv_fft(u, taps):
    bsz, seq_len, c = u.shape
    r = FFT_R
    f_re, f_im, g_re, g_im = _dft_tables()
    t_re, t_im = fft_rows(taps.reshape(1, r, r * c), f_re, f_im)
    h_re, h_im = fft_spectrum(t_re.reshape(1, r, r, c), t_im.reshape(1, r, r, c), g_re, g_im)
    half = seq_len // r
    a_re, a_im = fft_rows(u.reshape(bsz, half, r * c), f_re[:, :half], f_im[:, :half])
    b_re, b_im = fft_filter_mid(a_re.reshape(bsz, r, r, c), a_im.reshape(bsz, r, r, c), h_re, h_im, g_re, g_im)
    y = ifft_rows(b_re.reshape(bsz, r, r * c), b_im.reshape(bsz, r, r * c), f_re, f_im, half)
    return y.reshape(bsz, seq_len, c)


def hyena_mixer(zh, p):
    zs = depthwise_conv(zh, p['hy_conv_w']) + p['hy_conv_b']
    x0, x1, v = jnp.split(zs, 3, axis=-1)
    z = x1 * v
    seq_len = zh.shape[1]
    h_fwd, h_bwd = hyena_filters(seq_len, p)
    if 2 * seq_len == FFT_R * FFT_R:
        taps = jnp.concatenate([h_fwd[:1] + h_bwd[:1] + p['hy_bias'], h_fwd[1:], jnp.zeros_like(h_fwd[:1]), h_bwd[:0:-1]], axis=0)
        y = x0 * long_conv_fft(z, taps)
    else:
        y = x0 * (two_sided_long_conv(z, h_fwd, h_bwd) + p['hy_bias'] * z)
    return rms_norm(y, p['hy_norm_g'])


def in_projection(h, p, full):
    bsz, n, k = h.shape
    hb = h.reshape(bsz * n, k).astype(BF16)
    out = {}
    out['s5'] = matmul(hb, p['w_in_s5'], name="in_s5").reshape(bsz, n, -1)
    out['qkv'] = matmul(hb, p['w_in_qkv'], name="in_qkv").reshape(bsz, n, -1)
    ab = matmul(hb, p['w_in_ab'], name="in_ab").reshape(bsz, n, -1)
    out['a'] = ab[..., :2 * DN_HEADS]
    out['b'] = ab[..., 2 * DN_HEADS:4 * DN_HEADS]
    if full:
        out['g'] = matmul(hb, p['w_in_g'], name="in_g").reshape(bsz, n, -1)
        out['hy'] = matmul(hb, p['w_in_hy'], name="in_hy").reshape(bsz, n, -1)
    return out


def token_mixer(h, hc, p, need_ctx_out):
    z = in_projection(h, p, True)
    zc = in_projection(hc, p, need_ctx_out)
    s5_y, s5_yc = s5_mixer(z['s5'], zc['s5'], p, need_ctx_out)
    dn_y, dn_yc = deltanet_mixer(z, zc, p, need_ctx_out)
    y = mm3(jnp.concatenate([s5_y, dn_y, hyena_mixer(z['hy'], p)], axis=-1).astype(BF16), p['w_out'], "out_proj")
    if not need_ctx_out:
        return y, None
    yc = mm3(jnp.concatenate([s5_yc, dn_yc, hyena_mixer(zc['hy'], p)], axis=-1).astype(BF16), p['w_out'], "out_proj_ctx")
    return y, yc


def _ada_kernel(a_ref, w_ref, o_ref):
    o_ref[...] = jnp.dot(a_ref[...].astype(BF16), w_ref[0].astype(BF16), preferred_element_type=F32)


def ada_modulation(c_rows, w_ada, b_ada, layer):
    rows, k = c_rows.shape
    n = w_ada.shape[2]
    tn = 1024
    a = jnp.zeros((8, k), F32).at[:rows].set(jax.nn.silu(c_rows))
    out = pl.pallas_call(
        _ada_kernel,
        grid=(n // tn,),
        in_specs=[pl.BlockSpec((8, k), lambda j: (0, 0)), pl.BlockSpec((1, k, tn), lambda j: (layer, 0, j))],
        out_specs=pl.BlockSpec((8, tn), lambda j: (0, j)),
        out_shape=jax.ShapeDtypeStruct((8, n), F32),
        compiler_params=pltpu.CompilerParams(dimension_semantics=("parallel",), vmem_limit_bytes=VMEM_LIMIT),
        name="ada",
    )(a, w_ada)
    return out[:rows] + b_ada


def trunk_layer(x, xc, c, c_ctx, p, last):
    bsz, n, dm = x.shape
    lc = xc.shape[1]
    ada = ada_modulation(jnp.concatenate([c, c_ctx[None]], axis=0), p['w_ada'], p['b_ada'], p['layer'])
    sh1, sc1, g1, sh2, sc2, g2 = jnp.split(ada[:bsz, None, :], 6, axis=-1)
    mc = jnp.split(ada[bsz], 6, axis=-1)
    y, yc = token_mixer(modulate(x, sh1, sc1), modulate(xc, mc[0], mc[1]), p, not last)
    x = layer_norm(DEEPNORM_ALPHA * x + g1 * y, p['ln1_g'], p['ln1_b'])
    if last:
        f = hier_moe(modulate(x, sh2, sc2).reshape(bsz * n, dm), p).reshape(bsz, n, dm)
        return layer_norm(DEEPNORM_ALPHA * x + g2 * f, p['ln2_g'], p['ln2_b']), None
    xc = layer_norm(DEEPNORM_ALPHA * xc + mc[2] * yc, p['ln1_g'], p['ln1_b'])
    tokens = jnp.concatenate([modulate(x, sh2, sc2).reshape(bsz * n, dm), modulate(xc, mc[3], mc[4]).reshape(bsz * lc, dm)], axis=0)
    f = hier_moe(tokens, p)
    x = layer_norm(DEEPNORM_ALPHA * x + g2 * f[:bsz * n].reshape(bsz, n, dm), p['ln2_g'], p['ln2_b'])
    xc = layer_norm(DEEPNORM_ALPHA * xc + mc[5] * f[bsz * n:].reshape(bsz, lc, dm), p['ln2_g'], p['ln2_b'])
    return x, xc


def kernel(x, c, ctx, c_ctx, w_ada, b_ada, w_in, s5_lam_re, s5_lam_im, s5_log_step, s5_b_re, s5_b_im, s5_c_re, s5_c_im, s5_d, s5_glu_w, s5_glu_b, s5_norm_g, dn_conv_w, dn_a_log, dn_dt_bias, dn_norm_g, hy_conv_w, hy_conv_b, hy_f_w1, hy_f_b1, hy_f_w2, hy_f_b2, hy_f_w3, hy_f_b3, hy_f_freq, hy_f_w4, hy_bias, hy_norm_g, w_out, ln1_g, ln1_b, ln2_g, ln2_b, moe_w_group, moe_b_group, moe_w_expert, moe_b_expert, moe_w_gate, moe_w_up, moe_w_down):
    xc = ctx
    for l in range(DEPTH):
        w_in_l = w_in[l]
        w_ab = jnp.zeros((D_MODEL, LANE), F32).at[:, :4 * DN_HEADS].set(w_in_l[:, COL_DN_A:STATE_COLS])
        w_router = jnp.zeros((D_MODEL, LANE), F32).at[:, :N_GROUPS].set(moe_w_group[l]).at[:, N_GROUPS:N_GROUPS + N_EXPERTS].set(moe_w_expert[l])
        p = {
            'layer': l, 'w_ada': w_ada, 'b_ada': b_ada[l],
            'w_in_s5': w_in_l[:, COL_S5:COL_DN_QKV].astype(BF16),
            'w_in_qkv': w_in_l[:, COL_DN_QKV:COL_DN_A].astype(BF16),
            'w_in_ab': w_ab.astype(BF16),
            'w_in_g': w_in_l[:, COL_DN_G:COL_HY].astype(BF16),
            'w_in_hy': w_in_l[:, COL_HY:].astype(BF16),
            's5_lam_re': s5_lam_re[l], 's5_lam_im': s5_lam_im[l], 's5_log_step': s5_log_step[l],
            's5_b_re': s5_b_re[l], 's5_b_im': s5_b_im[l], 's5_c_re': s5_c_re[l], 's5_c_im': s5_c_im[l],
            's5_d': s5_d[l], 's5_glu_w': s5_glu_w[l].astype(BF16), 's5_glu_b': s5_glu_b[l], 's5_norm_g': s5_norm_g[l],
            'dn_conv_w': dn_conv_w[l], 'dn_a_log': dn_a_log[l], 'dn_dt_bias': dn_dt_bias[l], 'dn_norm_g': dn_norm_g[l],
            'hy_conv_w': hy_conv_w[l], 'hy_conv_b': hy_conv_b[l],
            'hy_f_w1': hy_f_w1[l], 'hy_f_b1': hy_f_b1[l], 'hy_f_w2': hy_f_w2[l], 'hy_f_b2': hy_f_b2[l],
            'hy_f_w3': hy_f_w3[l], 'hy_f_b3': hy_f_b3[l], 'hy_f_freq': hy_f_freq[l], 'hy_f_w4': hy_f_w4[l],
            'hy_bias': hy_bias[l], 'hy_norm_g': hy_norm_g[l], 'w_out': w_out[l].astype(BF16),
            'ln1_g': ln1_g[l], 'ln1_b': ln1_b[l], 'ln2_g': ln2_g[l], 'ln2_b': ln2_b[l],
            'moe_w_router': w_router, 'moe_b_group': moe_b_group[l], 'moe_b_expert': moe_b_expert[l],
            'moe_w_gate': moe_w_gate, 'moe_w_up': moe_w_up, 'moe_w_down': moe_w_down,
        }
        x, xc = trunk_layer(x, xc, c, c_ctx, p, l == DEPTH - 1)
    return x
```

```python
import functools
import math

import jax
import jax.numpy as jnp
from jax import lax
from jax.experimental import pallas as pl
from jax.experimental.pallas import tpu as pltpu

D_MODEL = 4096
DEPTH = 2
GRID_W = 64

S5_WIDTH = D_MODEL // 4
S5_GROUP = 16
S5_GROUPS = S5_WIDTH // S5_GROUP
S5_STATE = 64

DN_WIDTH = D_MODEL // 2
DN_HEAD_DIM = 128
DN_HEADS = DN_WIDTH // DN_HEAD_DIM
DN_CHUNK = 64
SHORT_CONV = 3

HY_WIDTH = D_MODEL - S5_WIDTH - DN_WIDTH
HY_BANDS = 16
HY_EMB = 1 + 2 * HY_BANDS
HY_HIDDEN = 64
HY_DECAY_SHORT_PCT = 0.3
HY_DECAY_LONG_PCT = 1.5
HY_DECAY_TARGET = 1e-2

MIX_WIDTH = S5_WIDTH + DN_WIDTH + HY_WIDTH

COL_S5 = 0
COL_DN_QKV = COL_S5 + S5_WIDTH
COL_DN_A = COL_DN_QKV + 3 * DN_WIDTH
COL_DN_B = COL_DN_A + 2 * DN_HEADS
STATE_COLS = COL_DN_B + 2 * DN_HEADS
COL_DN_G = STATE_COLS
COL_HY = COL_DN_G + DN_WIDTH
IN_WIDTH = COL_HY + 3 * HY_WIDTH

N_GROUPS = 4
EXPERTS_PER_GROUP = 8
N_EXPERTS = N_GROUPS * EXPERTS_PER_GROUP
TOP_K = 2
D_EXPERT = 512

DEEPNORM_ALPHA = (2 * DEPTH) ** 0.25
LN_EPS = 1e-5
RMS_EPS = 1e-6
F32 = jnp.float32
BF16 = jnp.bfloat16

V7X_VMEM_BYTES = 64 * 1024 * 1024
VMEM_LIMIT = 52 * 1024 * 1024
MOE_VMEM_LIMIT = 56 * 1024 * 1024
LANE = 128

MOE_TILE = 256
DN_HEAD_GROUP = 4
DN_SPLIT_LEVELS = 2
S5_T = 64


def _mm_kernel(a_ref, b_ref, o_ref):
    o_ref[...] = jnp.dot(a_ref[...].astype(BF16), b_ref[...].astype(BF16), preferred_element_type=F32).astype(o_ref.dtype)


def _pick(n, prefs):
    for p in prefs:
        if n % p == 0:
            return p
    return n


def matmul(a, b, out_dtype=F32, name="matmul"):
    m, k = a.shape
    n = b.shape[1]
    tm = _pick(m, (512, 256, 128, 64, 32, 16, 8))
    tn = _pick(n, (1024, 512, 256, 128))
    return pl.pallas_call(
        _mm_kernel,
        grid=(m // tm, n // tn),
        in_specs=[pl.BlockSpec((tm, k), lambda i, j: (i, 0)), pl.BlockSpec((k, tn), lambda i, j: (0, j))],
        out_specs=pl.BlockSpec((tm, tn), lambda i, j: (i, j)),
        out_shape=jax.ShapeDtypeStruct((m, n), out_dtype),
        compiler_params=pltpu.CompilerParams(dimension_semantics=("parallel", "parallel"), vmem_limit_bytes=VMEM_LIMIT),
        name=name,
    )(a, b)


def _mm_f32_kernel(a_ref, b_ref, o_ref):
    o_ref[...] = _hdot(_split(a_ref[...]), _split(b_ref[...]))


def matmul_f32(a, b, name="matmul_f32"):
    m, k = a.shape
    n = b.shape[1]
    tm = _pick(m, (512, 256, 128, 64, 32, 16, 8))
    return pl.pallas_call(
        _mm_f32_kernel,
        grid=(m // tm,),
        in_specs=[pl.BlockSpec((tm, k), lambda i: (i, 0)), pl.BlockSpec((k, n), lambda i: (0, 0))],
        out_specs=pl.BlockSpec((tm, n), lambda i: (i, 0)),
        out_shape=jax.ShapeDtypeStruct((m, n), F32),
        compiler_params=pltpu.CompilerParams(dimension_semantics=("parallel",), vmem_limit_bytes=VMEM_LIMIT),
        name=name,
    )(a, b)


def _moe_kernel(tile_expert_ref, n_used_ref, x_ref, w_ref, wg_ref, wu_ref, wd_ref, o_ref, wg_s, wu_s, wd_s):
    i = pl.program_id(0)
    prev = tile_expert_ref[jnp.maximum(i - 1, 0)]

    @pl.when((i == 0) | (tile_expert_ref[i] != prev))
    def _():
        wg_s[...] = wg_ref[0, 0].astype(BF16)
        wu_s[...] = wu_ref[0, 0].astype(BF16)
        wd_s[...] = wd_ref[0, 0].astype(BF16)

    @pl.when(i < n_used_ref[0])
    def _():
        x = x_ref[...]
        g = jnp.dot(x, wg_s[...], preferred_element_type=F32)
        u = jnp.dot(x, wu_s[...], preferred_element_type=F32)
        h = (g * jax.nn.sigmoid(g)) * u * w_ref[...]
        o_ref[...] = jnp.dot(h.astype(BF16), wd_s[...], preferred_element_type=F32).astype(o_ref.dtype)

    @pl.when(i >= n_used_ref[0])
    def _():
        o_ref[...] = jnp.zeros_like(o_ref)


def moe_experts(tile_expert, n_used, xs, row_w, wg, wu, wd, layer):
    p_rows, dm = xs.shape
    n_tiles = p_rows // MOE_TILE
    single = pl.Buffered(1)
    grid_spec = pltpu.PrefetchScalarGridSpec(
        num_scalar_prefetch=2,
        grid=(n_tiles,),
        in_specs=[
            pl.BlockSpec((MOE_TILE, dm), lambda i, te, nu: (i, 0)),
            pl.BlockSpec((MOE_TILE, 1), lambda i, te, nu: (i, 0)),
            pl.BlockSpec((1, 1, dm, D_EXPERT), lambda i, te, nu: (layer, te[i], 0, 0), pipeline_mode=single),
            pl.BlockSpec((1, 1, dm, D_EXPERT), lambda i, te, nu: (layer, te[i], 0, 0), pipeline_mode=single),
            pl.BlockSpec((1, 1, D_EXPERT, dm), lambda i, te, nu: (layer, te[i], 0, 0), pipeline_mode=single),
        ],
        out_specs=pl.BlockSpec((MOE_TILE, dm), lambda i, te, nu: (i, 0)),
        scratch_shapes=[pltpu.VMEM((dm, D_EXPERT), BF16), pltpu.VMEM((dm, D_EXPERT), BF16), pltpu.VMEM((D_EXPERT, dm), BF16)],
    )
    return pl.pallas_call(
        _moe_kernel,
        grid_spec=grid_spec,
        out_shape=jax.ShapeDtypeStruct((p_rows, dm), BF16),
        compiler_params=pltpu.CompilerParams(dimension_semantics=("arbitrary",), vmem_limit_bytes=MOE_VMEM_LIMIT),
        name="moe_experts",
    )(tile_expert, n_used, xs, row_w, wg, wu, wd)


def hier_moe(t, p):
    n_tok = t.shape[0]
    logits = matmul_f32(t, p['moe_w_router'], name="moe_router")
    group_logits = logits[:, :N_GROUPS] + p['moe_b_group']
    group = jnp.argmax(group_logits, axis=-1)
    group_w = jnp.max(jax.nn.softmax(group_logits, axis=-1), axis=-1, keepdims=True)
    exp_logits = (logits[:, N_GROUPS:N_GROUPS + N_EXPERTS] + p['moe_b_expert']).reshape(n_tok, N_GROUPS, EXPERTS_PER_GROUP)
    exp_logits = jnp.take_along_axis(exp_logits, group[:, None, None], axis=1)[:, 0]
    top_logits, top_idx = lax.top_k(exp_logits, TOP_K)
    top_w = jax.nn.softmax(top_logits, axis=-1) * group_w
    expert_id = (group[:, None] * EXPERTS_PER_GROUP + top_idx).astype(jnp.int32)

    n_rows = n_tok * TOP_K
    flat_e = expert_id.reshape(n_rows)
    flat_w = top_w.reshape(n_rows)
    flat_t = jnp.arange(n_rows, dtype=jnp.int32) // TOP_K
    order = jnp.argsort(flat_e, stable=True)
    sorted_e = flat_e[order]
    counts = jnp.zeros((N_EXPERTS,), jnp.int32).at[flat_e].add(1)
    padded = ((counts + MOE_TILE - 1) // MOE_TILE) * MOE_TILE
    pad_end = jnp.cumsum(padded)
    pad_start = pad_end - padded
    start = jnp.cumsum(counts) - counts
    dest = pad_start[sorted_e] + (jnp.arange(n_rows, dtype=jnp.int32) - start[sorted_e])
    p_rows = n_rows + N_EXPERTS * MOE_TILE
    n_tiles = p_rows // MOE_TILE
    row_token = jnp.zeros((p_rows,), jnp.int32).at[dest].set(flat_t[order])
    row_w = jnp.zeros((p_rows,), F32).at[dest].set(flat_w[order])
    pos = jnp.zeros((n_rows,), jnp.int32).at[order].set(dest).reshape(n_tok, TOP_K)
    n_used = (pad_end[-1] // MOE_TILE).astype(jnp.int32)
    tile_idx = jnp.minimum(jnp.arange(n_tiles, dtype=jnp.int32), n_used - 1)
    tile_expert = jnp.minimum(jnp.searchsorted(pad_end, tile_idx * MOE_TILE, side='right'), N_EXPERTS - 1).astype(jnp.int32)

    xs = t.astype(BF16)[row_token]
    ys = moe_experts(tile_expert, n_used.reshape(1), xs, row_w[:, None], p['moe_w_gate'], p['moe_w_up'], p['moe_w_down'], p['layer'])
    return ys[pos[:, 0]].astype(F32) + ys[pos[:, 1]].astype(F32)


def _flip(t, rev):
    return jnp.flip(t, axis=1) if rev else t


def layer_norm(x, g, b):
    mu = jnp.mean(x, axis=-1, keepdims=True)
    var = jnp.mean(jnp.square(x - mu), axis=-1, keepdims=True)
    return (x - mu) * lax.rsqrt(var + LN_EPS) * g + b


def rms_norm(x, g):
    return x * lax.rsqrt(jnp.mean(jnp.square(x), axis=-1, keepdims=True) + RMS_EPS) * g


def l2_normalize(t):
    return t * lax.rsqrt(jnp.sum(t * t, axis=-1, keepdims=True) + 1e-6)


def depthwise_conv(u, w):
    k = w.shape[0]
    return lax.conv_general_dilated(u, w[:, None, :], window_strides=(1,), padding=[(k // 2, k // 2)], dimension_numbers=('NWC', 'WIO', 'NWC'), feature_group_count=u.shape[-1])


def modulate(t, shift, scale):
    return t * (1 + scale) + shift


def mm3(t, w, name):
    bsz, n, k = t.shape
    return matmul(t.reshape(bsz * n, k), w, name=name).reshape(bsz, n, w.shape[1])


def s5_discretize(lam_re, lam_im, log_step, b_re, b_im):
    step = jnp.exp(log_step)[:, None]
    mag = jnp.exp(lam_re * step)
    a_re = mag * jnp.cos(lam_im * step)
    a_im = mag * jnp.sin(lam_im * step)
    den = lam_re * lam_re + lam_im * lam_im
    f_re = ((a_re - 1.0) * lam_re + a_im * lam_im) / den
    f_im = (a_im * lam_re - (a_re - 1.0) * lam_im) / den
    bb_re = f_re[..., None] * b_re - f_im[..., None] * b_im
    bb_im = f_re[..., None] * b_im + f_im[..., None] * b_re
    return a_re, a_im, bb_re, bb_im


def _cmul(x, y):
    return x[0] * y[0] - x[1] * y[1], x[0] * y[1] + x[1] * y[0]


def s5_tables(p):
    hp = lax.Precision.HIGHEST
    t_len, grp, st = S5_T, S5_GROUPS, S5_STATE
    taps, wst, rd, pq = [], [], [], []
    for d in range(2):
        a_re, a_im, bb_re, bb_im = s5_discretize(p['s5_lam_re'][d], p['s5_lam_im'][d], p['s5_log_step'][d], p['s5_b_re'][d], p['s5_b_im'][d])
        c_re, c_im = p['s5_c_re'][d], p['s5_c_im'][d]
        pr, pi = lax.associative_scan(_cmul, (jnp.broadcast_to(a_re, (t_len, grp, st)), jnp.broadcast_to(a_im, (t_len, grp, st))), axis=0)
        pw_re = jnp.concatenate([jnp.ones((1, grp, st), F32), pr], axis=0)
        pw_im = jnp.concatenate([jnp.zeros((1, grp, st), F32), pi], axis=0)
        ca_re = c_re[None] * pw_re[:, :, None, :] - c_im[None] * pw_im[:, :, None, :]
        ca_im = c_re[None] * pw_im[:, :, None, :] + c_im[None] * pw_re[:, :, None, :]
        taps.append(jnp.einsum('tgip,gpj->tgij', ca_re[:t_len], bb_re, precision=hp) - jnp.einsum('tgip,gpj->tgij', ca_im[:t_len], bb_im, precision=hp))
        e_st = jnp.arange(t_len - 1, -1, -1) if d == 0 else jnp.arange(t_len)
        w_re = pw_re[e_st][..., None] * bb_re[None] - pw_im[e_st][..., None] * bb_im[None]
        w_im = pw_re[e_st][..., None] * bb_im[None] + pw_im[e_st][..., None] * bb_re[None]
        to_rows = lambda w: jnp.transpose(w, (1, 0, 3, 2)).reshape(grp, t_len * S5_GROUP, st)
        wst += [to_rows(w_re), to_rows(w_im)]
        e_rd = jnp.arange(1, t_len + 1) if d == 0 else jnp.arange(t_len, 0, -1)
        to_cols = lambda r: jnp.transpose(r, (1, 3, 0, 2)).reshape(grp, st, t_len * S5_GROUP)
        rd += [to_cols(ca_re[e_rd]), to_cols(-ca_im[e_rd])]
        lv_re, lv_im = pw_re[t_len], pw_im[t_len]
        for _ in range(8):
            pq += [jnp.concatenate([lv_re, lv_re], axis=-1), jnp.concatenate([-lv_im, lv_im], axis=-1)]
            lv_re, lv_im = _cmul((lv_re, lv_im), (lv_re, lv_im))
    kf, kb = taps
    blocks = jnp.concatenate([kb[1:][::-1], (kf[0] + kb[0])[None], kf[1:], jnp.zeros_like(kf[:1])], axis=0)
    ext = jnp.transpose(blocks, (1, 3, 0, 2)).reshape(grp, S5_GROUP, 2 * t_len * S5_GROUP)
    return ext, jnp.concatenate(wst, axis=-1).astype(BF16), jnp.concatenate(rd, axis=1).astype(BF16), jnp.stack(pq, axis=1)


def _s5_kernel(u_ref, ext_ref, wst_ref, rd_ref, pq_ref, h0_ref, y_ref, hfin_ref, m_ref, *, n_chunks, bsz):
    cw = S5_T * S5_GROUP
    st2 = 2 * S5_STATE
    ext = ext_ref[0]
    for s in range(S5_T):
        off = S5_GROUP * (S5_T - 1 - s)
        shifted = ext if off == 0 else pltpu.roll(ext, shift=2 * cw - off, axis=1)
        m_ref[S5_GROUP * s:S5_GROUP * (s + 1), :] = shifted[:, :cw].astype(BF16)
    u = u_ref[0]
    rows = u.shape[0]
    y = jnp.dot(u, m_ref[...], preferred_element_type=F32)
    s_all = jnp.dot(u, wst_ref[0], preferred_element_type=F32)
    sf, sb = s_all[:, :st2], s_all[:, st2:]
    row = lax.broadcasted_iota(jnp.int32, (rows, st2), 0)
    c_idx = row % n_chunks

    def cmul_rows(idx, x):
        return pq_ref[0, idx:idx + 1, :] * x + pq_ref[0, idx + 1:idx + 2, :] * pltpu.roll(x, shift=S5_STATE, axis=1)

    h0 = h0_ref[0]
    h0f = jnp.zeros((rows, st2), F32)
    h0b = jnp.zeros((rows, st2), F32)
    for b in range(bsz):
        in_b = (row >= b * n_chunks) & (row < (b + 1) * n_chunks)
        h0f = jnp.where(in_b, h0[b:b + 1, :st2], h0f)
        h0b = jnp.where(in_b, h0[b:b + 1, st2:], h0b)
    hf = jnp.where(c_idx == 0, h0f, pltpu.roll(sf, shift=1, axis=0))
    hb = jnp.where(c_idx == n_chunks - 1, h0b, pltpu.roll(sb, shift=rows - 1, axis=0))
    level, sh = 0, 1
    while sh < n_chunks:
        dn = jnp.where(c_idx >= sh, pltpu.roll(hf, shift=sh, axis=0), 0.0)
        up = jnp.where(c_idx < n_chunks - sh, pltpu.roll(hb, shift=rows - sh, axis=0), 0.0)
        hf = hf + cmul_rows(2 * level, dn)
        hb = hb + cmul_rows(16 + 2 * level, up)
        level, sh = level + 1, sh * 2
    h_in = jnp.concatenate([hf, hb], axis=1)
    y_ref[0] = y + jnp.dot(h_in.astype(BF16), rd_ref[0], preferred_element_type=F32)
    hfin_ref[0] = jnp.concatenate([cmul_rows(0, hf) + sf, cmul_rows(16, hb) + sb], axis=1)


def s5_scan(u, tables, h0, n_chunks, bsz):
    ext, wst, rd, pq = tables
    grp, rows, cw = u.shape
    st4 = 4 * S5_STATE
    blk = lambda *shape: pl.BlockSpec((1,) + shape, lambda i: (i, 0, 0))
    return pl.pallas_call(
        functools.partial(_s5_kernel, n_chunks=n_chunks, bsz=bsz),
        grid=(grp,),
        in_specs=[blk(rows, cw), blk(S5_GROUP, 2 * cw), blk(cw, st4), blk(st4, cw), blk(32, 2 * S5_STATE), blk(bsz, st4)],
        out_specs=[blk(rows, cw), blk(rows, st4)],
        out_shape=[jax.ShapeDtypeStruct((grp, rows, cw), F32), jax.ShapeDtypeStruct((grp, rows, st4), F32)],
        scratch_shapes=[pltpu.VMEM((cw, cw), BF16)],
        compiler_params=pltpu.CompilerParams(dimension_semantics=("parallel",), vmem_limit_bytes=VMEM_LIMIT),
        name="s5_scan",
    )(u, ext, wst, rd, pq, h0)


def s5_sequence(u, tables, h0):
    bsz, n, _ = u.shape
    n_chunks = n // S5_T
    rows = bsz * n_chunks
    rows_pad = -(-rows // 16) * 16
    ug = u.astype(BF16).reshape(bsz, n_chunks, S5_T, S5_GROUPS, S5_GROUP)
    ug = jnp.transpose(ug, (3, 0, 1, 2, 4)).reshape(S5_GROUPS, rows, S5_T * S5_GROUP)
    if rows_pad != rows:
        ug = jnp.pad(ug, ((0, 0), (0, rows_pad - rows), (0, 0)))
    y, hfin = s5_scan(ug, tables, h0, n_chunks, bsz)
    y = y[:, :rows].reshape(S5_GROUPS, bsz, n_chunks, S5_T, S5_GROUP)
    y = jnp.transpose(y, (1, 2, 3, 0, 4)).reshape(bsz, n, S5_WIDTH)
    hfin = hfin[:, :rows].reshape(S5_GROUPS, bsz, n_chunks, 4 * S5_STATE)
    st2 = 2 * S5_STATE
    h_end = jnp.concatenate([hfin[:, :, n_chunks - 1, :st2], hfin[:, :, 0, st2:]], axis=-1)
    return y, h_end


def s5_glu(y, p):
    y = jax.nn.gelu(y)
    y = y * jax.nn.sigmoid(mm3(y, p['s5_glu_w'], "s5_glu") + p['s5_glu_b'])
    return rms_norm(y, p['s5_norm_g'])


def s5_mixer(u, uc, p, need_ctx_out):
    bsz = u.shape[0]
    tables = s5_tables(p)
    yc, h_ctx = s5_sequence(uc, tables, jnp.zeros((S5_GROUPS, bsz, 4 * S5_STATE), F32))
    y, _ = s5_sequence(u, tables, h_ctx)
    out = s5_glu(u * p['s5_d'] + y, p)
    out_c = s5_glu(uc * p['s5_d'] + yc, p) if need_ctx_out else None
    return out, out_c


def dn_features(z_qkv, conv_w):
    bsz, seq_len, _ = z_qkv.shape
    q, k, v = jnp.split(jax.nn.silu(depthwise_conv(z_qkv, conv_w)), 3, axis=-1)
    shp = (bsz, seq_len, DN_HEADS, DN_HEAD_DIM)
    return l2_normalize(q.reshape(shp)) * DN_HEAD_DIM ** -0.5, l2_normalize(k.reshape(shp)), v.reshape(shp)


def dn_gates(z_a, z_b, a_log, dt_bias):
    bsz, seq_len, _ = z_a.shape
    shp = (bsz, seq_len, 2, DN_HEADS)
    g = -jnp.exp(a_log) * jax.nn.softplus(z_a.reshape(shp) + dt_bias)
    beta = jax.nn.sigmoid(z_b.reshape(shp))
    return g, beta


def _dn_kernel(q_ref, k_ref, v_ref, g_ref, gt_ref, b_ref, s0_ref, o_ref, sfin_ref, s_ref, *, n_chunks):
    cs, hd, hg = DN_CHUNK, DN_HEAD_DIM, DN_HEAD_GROUP
    rows = cs * hg
    n_groups = DN_HEADS // hg
    groups = range(n_groups)
    step = pl.program_id(1)

    @pl.when(step == 0)
    def _():
        s_ref[...] = s0_ref[0]

    r_i = lax.broadcasted_iota(jnp.int32, (rows, rows), 0)
    c_i = lax.broadcasted_iota(jnp.int32, (rows, rows), 1)
    same_head = (r_i // cs) == (c_i // cs)
    incl = same_head & ((r_i % cs) >= (c_i % cs))
    strict = same_head & ((r_i % cs) > (c_i % cs))
    upto = same_head & ((r_i % cs) <= (c_i % cs))
    r_c = lax.broadcasted_iota(jnp.int32, (cs, cs), 0)
    c_c = lax.broadcasted_iota(jnp.int32, (cs, cs), 1)
    hp = lax.Precision.HIGHEST
    beta = b_ref[0, 0]
    gc = jnp.dot((r_c >= c_c).astype(F32), g_ref[0, 0], preferred_element_type=F32, precision=hp)
    gc_row = jnp.dot(gt_ref[0, 0], upto.astype(F32), preferred_element_type=F32, precision=hp)
    g_tot = gc[cs - 1:cs, :]
    e_g = jnp.exp(gc)
    e_tail = jnp.exp(g_tot - gc)
    e_tot = jnp.exp(g_tot)

    def head_cols(h):
        return slice(h * hd, (h + 1) * hd)

    def stack_heads(grp, fn):
        return jnp.concatenate([fn(grp * hg + j) for j in range(hg)], axis=0)

    def col(x, h, width):
        return jnp.broadcast_to(x[:, h:h + 1], (cs, width))

    def mm(a, b):
        return jnp.dot(a.astype(BF16), b.astype(BF16), preferred_element_type=F32)

    def mm_split_rhs(a_bf16, b):
        hi = b.astype(BF16)
        lo = (b - hi.astype(F32)).astype(BF16)
        t = jnp.dot(a_bf16, jnp.concatenate([hi, lo], axis=1), preferred_element_type=F32)
        return t[:, :b.shape[1]] + t[:, b.shape[1]:]

    kst = [stack_heads(gr, lambda h: k_ref[0, :, head_cols(h)]) for gr in groups]
    qst = [stack_heads(gr, lambda h: q_ref[0, :, head_cols(h)]) for gr in groups]
    b_col = [stack_heads(gr, lambda h: col(beta, h, hd)) for gr in groups]
    eg_col = [stack_heads(gr, lambda h: col(e_g, h, hd)) for gr in groups]
    kb = [kst[gr] * b_col[gr] for gr in groups]
    decay = []
    for gr in groups:
        diff = stack_heads(gr, lambda h: col(gc, h, rows)) - gc_row[gr:gr + 1, :]
        decay.append(jnp.where(incl, jnp.exp(jnp.where(incl, diff, 0.0)), 0.0))
    kq = [lax.dot_general(jnp.concatenate([kb[gr], qst[gr]], axis=0).astype(BF16), kst[gr].astype(BF16),
                          (((1,), (1,)), ((), ())), preferred_element_type=F32) for gr in groups]
    n = [jnp.where(strict, kq[gr][:rows] * decay[gr], 0.0).astype(BF16) for gr in groups]
    qk = [(kq[gr][rows:] * decay[gr]).astype(BF16) for gr in groups]
    x = [jnp.concatenate([stack_heads(gr, lambda h: v_ref[0, :, head_cols(h)]) * b_col[gr], kb[gr] * eg_col[gr]], axis=1) for gr in groups]
    x = [x[gr] - mm_split_rhs(n[gr], x[gr]) for gr in groups]
    pw = n
    for level in range(5):
        pw = [jnp.dot(pw[gr], pw[gr], preferred_element_type=F32).astype(BF16) for gr in groups]
        if level < DN_SPLIT_LEVELS:
            x = [x[gr] + mm_split_rhs(pw[gr], x[gr]) for gr in groups]
        else:
            x = [x[gr] + jnp.dot(pw[gr], x[gr].astype(BF16), preferred_element_type=F32) for gr in groups]
    qd = [qst[gr] * eg_col[gr] for gr in groups]
    for gr in groups:
        v_new, o_state = [], []
        for j in range(hg):
            h, rs = gr * hg + j, slice(j * cs, (j + 1) * cs)
            ws = mm(jnp.concatenate([x[gr][rs, hd:], qd[gr][rs]], axis=0), s_ref[h])
            v_new.append(x[gr][rs, :hd] - ws[:cs])
            o_state.append(ws[cs:])
        v_all = jnp.concatenate(v_new, axis=0).astype(BF16)
        o_all = jnp.concatenate(o_state, axis=0) + jnp.dot(qk[gr], v_all, preferred_element_type=F32)
        for j in range(hg):
            h, rs = gr * hg + j, slice(j * cs, (j + 1) * cs)
            o_ref[0, :, head_cols(h)] = o_all[rs]
            kt = (kst[gr][rs] * col(e_tail, h, hd)).astype(BF16)
            s_ref[h] = s_ref[h] * e_tot[:, h:h + 1] + lax.dot_general(kt, v_all[rs], (((0,), (0,)), ((), ())), preferred_element_type=F32)

    @pl.when(step == n_chunks - 1)
    def _():
        sfin_ref[0] = s_ref[...]


def gated_delta_rule(q, k, v, g, beta, s0):
    bsz, seq_len, heads, hd = q.shape
    width = heads * hd
    n_chunks = seq_len // DN_CHUNK
    q, k, v = (t.reshape(bsz, seq_len, width) for t in (q, k, v))
    g4 = g.reshape(bsz, n_chunks, DN_CHUNK, heads)
    gt4 = jnp.swapaxes(g4, 2, 3).reshape(bsz, n_chunks, heads // DN_HEAD_GROUP, DN_HEAD_GROUP * DN_CHUNK)
    b4 = beta.reshape(bsz, n_chunks, DN_CHUNK, heads)
    tok_spec = pl.BlockSpec((1, DN_CHUNK, width), lambda b, i: (b, i, 0))
    gate_spec = pl.BlockSpec((1, 1, DN_CHUNK, heads), lambda b, i: (b, i, 0, 0))
    gate_t_spec = pl.BlockSpec((1, 1, heads // DN_HEAD_GROUP, DN_HEAD_GROUP * DN_CHUNK), lambda b, i: (b, i, 0, 0))
    state_spec = pl.BlockSpec((1, heads, hd, hd), lambda b, i: (b, 0, 0, 0))
    o, s_fin = pl.pallas_call(
        functools.partial(_dn_kernel, n_chunks=n_chunks),
        grid=(bsz, n_chunks),
        in_specs=[tok_spec, tok_spec, tok_spec, gate_spec, gate_t_spec, gate_spec, state_spec],
        out_specs=[tok_spec, state_spec],
        out_shape=[jax.ShapeDtypeStruct((bsz, seq_len, width), F32), jax.ShapeDtypeStruct(s0.shape, F32)],
        scratch_shapes=[pltpu.VMEM((heads, hd, hd), F32)],
        compiler_params=pltpu.CompilerParams(dimension_semantics=("parallel", "arbitrary"), vmem_limit_bytes=VMEM_LIMIT),
        name="delta_rule",
    )(q, k, v, g4, gt4, b4, s0)
    return o.reshape(bsz, seq_len, heads, hd), s_fin


def dn_output(o, z_gate, p):
    bsz, seq_len = o.shape[:2]
    gate = jax.nn.silu(z_gate).reshape(bsz, seq_len, DN_HEADS, DN_HEAD_DIM)
    return (rms_norm(o, p['dn_norm_g']) * gate).reshape(bsz, seq_len, DN_WIDTH)


def deltanet_mixer(z, zc, p, need_ctx_out):
    q, k, v = dn_features(z['qkv'], p['dn_conv_w'])
    g, beta = dn_gates(z['a'], z['b'], p['dn_a_log'], p['dn_dt_bias'])
    qc, kc, vc = dn_features(zc['qkv'], p['dn_conv_w'])
    gc, betac = dn_gates(zc['a'], zc['b'], p['dn_a_log'], p['dn_dt_bias'])
    s0 = jnp.zeros((q.shape[0], DN_HEADS, DN_HEAD_DIM, DN_HEAD_DIM), F32)
    k, v, kc, vc = (t.reshape(t.shape[0], t.shape[1], DN_WIDTH) for t in (k, v, kc, vc))
    o, oc = 0.0, 0.0
    for d in range(2):
        rev = d == 1
        oc_d, s_ctx = gated_delta_rule(*(_flip(t, rev) for t in (qc, kc, vc, gc[:, :, d], betac[:, :, d])), s0)
        o_d, _ = gated_delta_rule(*(_flip(t, rev) for t in (q, k, v, g[:, :, d], beta[:, :, d])), s_ctx)
        o = o + _flip(o_d, rev)
        if need_ctx_out:
            oc = oc + _flip(oc_d, rev)
    y = dn_output(o, z['g'], p)
    yc = dn_output(oc, zc['g'], p) if need_ctx_out else None
    return y, yc


def hyena_filters(seq_len, p):
    hp = lax.Precision.HIGHEST
    t = jnp.linspace(0.0, 1.0, seq_len, dtype=F32)[:, None]
    w = (2.0 * math.pi / seq_len) * jnp.arange(seq_len, dtype=F32)[:, None]
    f = jnp.linspace(1e-4, HY_BANDS - 1, HY_BANDS, dtype=F32)[None, :]
    feats = jnp.concatenate([t, jnp.cos(w * f), -jnp.sin(w * f)], axis=-1)
    freq = p['hy_f_freq']
    h = jnp.sin(freq * (jnp.dot(feats, p['hy_f_w1'], precision=hp) + p['hy_f_b1']))
    h = jnp.sin(freq * (jnp.dot(h, p['hy_f_w2'], precision=hp) + p['hy_f_b2']))
    h = jnp.sin(freq * (jnp.dot(h, p['hy_f_w3'], precision=hp) + p['hy_f_b3']))
    h = jnp.dot(h, p['hy_f_w4'], precision=hp).reshape(seq_len, 2, HY_WIDTH)
    rates = jnp.linspace(math.log(HY_DECAY_TARGET) / HY_DECAY_LONG_PCT, math.log(HY_DECAY_TARGET) / HY_DECAY_SHORT_PCT, HY_WIDTH, dtype=F32)
    h = h * jnp.exp(-t * jnp.abs(rates))[:, None, :]
    return h[:, 0], h[:, 1]


def two_sided_long_conv(u, h_fwd, h_bwd):
    seq_len = u.shape[1]
    taps = jnp.concatenate([h_fwd[:1] + h_bwd[:1], h_fwd[1:], jnp.zeros_like(h_fwd[:1]), h_bwd[:0:-1]], axis=0)
    u_f = jnp.fft.rfft(u, n=2 * seq_len, axis=1)
    t_f = jnp.fft.rfft(taps, axis=0)
    return jnp.fft.irfft(u_f * t_f[None], n=2 * seq_len, axis=1)[:, :seq_len]


FFT_R = 128
FFT_COLS = 8192


def _dft_tables():
    n, big = FFT_R, FFT_R * FFT_R
    idx = jnp.arange(n, dtype=jnp.int32)
    ang = (2.0 * math.pi / n) * ((idx[:, None] * idx[None, :]) % n).astype(F32)
    m = (idx[None, None, :] * (n * idx[None, :, None] + idx[:, None, None])) % big
    g_ang = (2.0 * math.pi / big) * m.astype(F32)
    return jnp.cos(ang), -jnp.sin(ang), jnp.cos(g_ang), -jnp.sin(g_ang)


def _split(x):
    hi = x.astype(BF16)
    return hi, (x - hi.astype(F32)).astype(BF16)


def _hdot(a, b):
    (a_hi, a_lo), (b_hi, b_lo) = a, b
    m = a_hi.shape[0]
    t = jnp.dot(jnp.concatenate([a_hi, a_lo], axis=0), b_hi, preferred_element_type=F32)
    return t[:m] + t[m:] + jnp.dot(a_hi, b_lo, preferred_element_type=F32)


def _fft_rows_kernel(f_re_ref, f_im_ref, x_ref, a_re_ref, a_im_ref):
    x = _split(x_ref[0])
    a_re_ref[0] = _hdot(_split(f_re_ref[...]), x)
    a_im_ref[0] = _hdot(_split(f_im_ref[...]), x)


def fft_rows(x, f_re, f_im):
    bsz, k, m = x.shape
    out = jax.ShapeDtypeStruct((bsz, FFT_R, m), F32)
    f_spec = pl.BlockSpec((FFT_R, k), lambda b, j: (0, 0))
    o_spec = pl.BlockSpec((1, FFT_R, FFT_COLS), lambda b, j: (b, 0, j))
    return pl.pallas_call(
        _fft_rows_kernel,
        grid=(bsz, m // FFT_COLS),
        in_specs=[f_spec, f_spec, pl.BlockSpec((1, k, FFT_COLS), lambda b, j: (b, 0, j))],
        out_specs=[o_spec, o_spec],
        out_shape=[out, out],
        compiler_params=pltpu.CompilerParams(dimension_semantics=("parallel", "parallel"), vmem_limit_bytes=VMEM_LIMIT),
        name="fft_rows",
    )(f_re, f_im, x)


def _fft_spectrum_kernel(g_re_ref, g_im_ref, a_re_ref, a_im_ref, x_re_ref, x_im_ref):
    gr, gi, ar, ai = (_split(t) for t in (g_re_ref[0], g_im_ref[0], a_re_ref[0, 0], a_im_ref[0, 0]))
    x_re_ref[0, 0] = _hdot(gr, ar) - _hdot(gi, ai)
    x_im_ref[0, 0] = _hdot(gr, ai) + _hdot(gi, ar)


def _fft_mid_kernel(g_re_ref, g_im_ref, gt_re_ref, gt_im_ref, a_re_ref, a_im_ref, h_re_ref, h_im_ref, b_re_ref, b_im_ref):
    gr, gi, ar, ai = (_split(t) for t in (g_re_ref[0], g_im_ref[0], a_re_ref[0, 0], a_im_ref[0, 0]))
    xr = _hdot(gr, ar) - _hdot(gi, ai)
    xi = _hdot(gr, ai) + _hdot(gi, ar)
    hr, hi = h_re_ref[0, 0], h_im_ref[0, 0]
    yr = _split(xr * hr - xi * hi)
    yi = _split(xr * hi + xi * hr)
    tr, ti = _split(gt_re_ref[0]), _split(gt_im_ref[0])
    b_re_ref[0, 0] = _hdot(tr, yr) + _hdot(ti, yi)
    b_im_ref[0, 0] = _hdot(tr, yi) - _hdot(ti, yr)


def fft_spectrum(a_re, a_im, g_re, g_im):
    bsz, r, _, c = a_re.shape
    g_spec = pl.BlockSpec((1, r, r), lambda k, b: (k, 0, 0))
    a_spec = pl.BlockSpec((1, 1, r, c), lambda k, b: (b, k, 0, 0))
    out = jax.ShapeDtypeStruct(a_re.shape, F32)
    return pl.pallas_call(
        _fft_spectrum_kernel,
        grid=(r, bsz),
        in_specs=[g_spec, g_spec, a_spec, a_spec],
        out_specs=[a_spec, a_spec],
        out_shape=[out, out],
        compiler_params=pltpu.CompilerParams(dimension_semantics=("parallel", "parallel"), vmem_limit_bytes=VMEM_LIMIT),
        name="fft_spectrum",
    )(g_re, g_im, a_re, a_im)


def fft_filter_mid(a_re, a_im, h_re, h_im, g_re, g_im):
    bsz, r, _, c = a_re.shape
    gt_re, gt_im = jnp.swapaxes(g_re, 1, 2), jnp.swapaxes(g_im, 1, 2)
    g_spec = pl.BlockSpec((1, r, r), lambda k, b: (k, 0, 0))
    a_spec = pl.BlockSpec((1, 1, r, c), lambda k, b: (b, k, 0, 0))
    h_spec = pl.BlockSpec((1, 1, r, c), lambda k, b: (0, k, 0, 0))
    out = jax.ShapeDtypeStruct(a_re.shape, F32)
    return pl.pallas_call(
        _fft_mid_kernel,
        grid=(r, bsz),
        in_specs=[g_spec, g_spec, g_spec, g_spec, a_spec, a_spec, h_spec, h_spec],
        out_specs=[a_spec, a_spec],
        out_shape=[out, out],
        compiler_params=pltpu.CompilerParams(dimension_semantics=("parallel", "parallel"), vmem_limit_bytes=VMEM_LIMIT),
        name="fft_filter_mid",
    )(g_re, g_im, gt_re, gt_im, a_re, a_im, h_re, h_im)


def _ifft_rows_kernel(c_ref, s_ref, b_re_ref, b_im_ref, y_ref):
    y = _hdot(_split(c_ref[...]), _split(b_re_ref[0])) + _hdot(_split(s_ref[...]), _split(b_im_ref[0]))
    y_ref[0] = y * (1.0 / (FFT_R * FFT_R))


def ifft_rows(b_re, b_im, f_re, f_im, rows):
    bsz, r, m = b_re.shape
    f_spec = pl.BlockSpec((rows, r), lambda b, j: (0, 0))
    b_spec = pl.BlockSpec((1, r, FFT_COLS), lambda b, j: (b, 0, j))
    return pl.pallas_call(
        _ifft_rows_kernel,
        grid=(bsz, m // FFT_COLS),
        in_specs=[f_spec, f_spec, b_spec, b_spec],
        out_specs=pl.BlockSpec((1, rows, FFT_COLS), lambda b, j: (b, 0, j)),
        out_shape=jax.ShapeDtypeStruct((bsz, rows, m), F32),
        compiler_params=pltpu.CompilerParams(dimension_semantics=("parallel", "parallel"), vmem_limit_bytes=VMEM_LIMIT),
        name="ifft_rows",
    )(f_re[:rows], f_im[:rows], b_re, b_im)


def long_conv_fft(u, taps):
    bsz, seq_len, c = u.shape
    r = FFT_R
    f_re, f_im, g_re, g_im = _dft_tables()
    t_re, t_im = fft_rows(taps.reshape(1, r, r * c), f_re, f_im)
    h_re, h_im = fft_spectrum(t_re.reshape(1, r, r, c), t_im.reshape(1, r, r, c), g_re, g_im)
    half = seq_len // r
    a_re, a_im = fft_rows(u.reshape(bsz, half, r * c), f_re[:, :half], f_im[:, :half])
    b_re, b_im = fft_filter_mid(a_re.reshape(bsz, r, r, c), a_im.reshape(bsz, r, r, c), h_re, h_im, g_re, g_im)
    y = ifft_rows(b_re.reshape(bsz, r, r * c), b_im.reshape(bsz, r, r * c), f_re, f_im, half)
    return y.reshape(bsz, seq_len, c)


def hyena_mixer(zh, p):
    zs = depthwise_conv(zh, p['hy_conv_w']) + p['hy_conv_b']
    x0, x1, v = jnp.split(zs, 3, axis=-1)
    z = x1 * v
    seq_len = zh.shape[1]
    h_fwd, h_bwd = hyena_filters(seq_len, p)
    if 2 * seq_len == FFT_R * FFT_R:
        taps = jnp.concatenate([h_fwd[:1] + h_bwd[:1] + p['hy_bias'], h_fwd[1:], jnp.zeros_like(h_fwd[:1]), h_bwd[:0:-1]], axis=0)
        y = x0 * long_conv_fft(z, taps)
    else:
        y = x0 * (two_sided_long_conv(z, h_fwd, h_bwd) + p['hy_bias'] * z)
    return rms_norm(y, p['hy_norm_g'])


def in_projection(h, p, full):
    bsz, n, k = h.shape
    hb = h.reshape(bsz * n, k).astype(BF16)
    out = {}
    out['s5'] = matmul(hb, p['w_in_s5'], name="in_s5").reshape(bsz, n, -1)
    out['qkv'] = matmul(hb, p['w_in_qkv'], name="in_qkv").reshape(bsz, n, -1)
    ab = matmul(hb, p['w_in_ab'], name="in_ab").reshape(bsz, n, -1)
    out['a'] = ab[..., :2 * DN_HEADS]
    out['b'] = ab[..., 2 * DN_HEADS:4 * DN_HEADS]
    if full:
        out['g'] = matmul(hb, p['w_in_g'], name="in_g").reshape(bsz, n, -1)
        out['hy'] = matmul(hb, p['w_in_hy'], name="in_hy").reshape(bsz, n, -1)
    return out


def token_mixer(h, hc, p, need_ctx_out):
    z = in_projection(h, p, True)
    zc = in_projection(hc, p, need_ctx_out)
    s5_y, s5_yc = s5_mixer(z['s5'], zc['s5'], p, need_ctx_out)
    dn_y, dn_yc = deltanet_mixer(z, zc, p, need_ctx_out)
    y = mm3(jnp.concatenate([s5_y, dn_y, hyena_mixer(z['hy'], p)], axis=-1).astype(BF16), p['w_out'], "out_proj")
    if not need_ctx_out:
        return y, None
    yc = mm3(jnp.concatenate([s5_yc, dn_yc, hyena_mixer(zc['hy'], p)], axis=-1).astype(BF16), p['w_out'], "out_proj_ctx")
    return y, yc


def _ada_kernel(a_ref, w_ref, o_ref):
    o_ref[...] = jnp.dot(a_ref[...].astype(BF16), w_ref[0].astype(BF16), preferred_element_type=F32)


def ada_modulation(c_rows, w_ada, b_ada, layer):
    rows, k = c_rows.shape
    n = w_ada.shape[2]
    tn = 1024
    a = jnp.zeros((8, k), F32).at[:rows].set(jax.nn.silu(c_rows))
    out = pl.pallas_call(
        _ada_kernel,
        grid=(n // tn,),
        in_specs=[pl.BlockSpec((8, k), lambda j: (0, 0)), pl.BlockSpec((1, k, tn), lambda j: (layer, 0, j))],
        out_specs=pl.BlockSpec((8, tn), lambda j: (0, j)),
        out_shape=jax.ShapeDtypeStruct((8, n), F32),
        compiler_params=pltpu.CompilerParams(dimension_semantics=("parallel",), vmem_limit_bytes=VMEM_LIMIT),
        name="ada",
    )(a, w_ada)
    return out[:rows] + b_ada


def trunk_layer(x, xc, c, c_ctx, p, last):
    bsz, n, dm = x.shape
    lc = xc.shape[1]
    ada = ada_modulation(jnp.concatenate([c, c_ctx[None]], axis=0), p['w_ada'], p['b_ada'], p['layer'])
    sh1, sc1, g1, sh2, sc2, g2 = jnp.split(ada[:bsz, None, :], 6, axis=-1)
    mc = jnp.split(ada[bsz], 6, axis=-1)
    y, yc = token_mixer(modulate(x, sh1, sc1), modulate(xc, mc[0], mc[1]), p, not last)
    x = layer_norm(DEEPNORM_ALPHA * x + g1 * y, p['ln1_g'], p['ln1_b'])
    if last:
        f = hier_moe(modulate(x, sh2, sc2).reshape(bsz * n, dm), p).reshape(bsz, n, dm)
        return layer_norm(DEEPNORM_ALPHA * x + g2 * f, p['ln2_g'], p['ln2_b']), None
    xc = layer_norm(DEEPNORM_ALPHA * xc + mc[2] * yc, p['ln1_g'], p['ln1_b'])
    tokens = jnp.concatenate([modulate(x, sh2, sc2).reshape(bsz * n, dm), modulate(xc, mc[3], mc[4]).reshape(bsz * lc, dm)], axis=0)
    f = hier_moe(tokens, p)
    x = layer_norm(DEEPNORM_ALPHA * x + g2 * f[:bsz * n].reshape(bsz, n, dm), p['ln2_g'], p['ln2_b'])
    xc = layer_norm(DEEPNORM_ALPHA * xc + mc[5] * f[bsz * n:].reshape(bsz, lc, dm), p['ln2_g'], p['ln2_b'])
    return x, xc


def kernel(x, c, ctx, c_ctx, w_ada, b_ada, w_in, s5_lam_re, s5_lam_im, s5_log_step, s5_b_re, s5_b_im, s5_c_re, s5_c_im, s5_d, s5_glu_w, s5_glu_b, s5_norm_g, dn_conv_w, dn_a_log, dn_dt_bias, dn_norm_g, hy_conv_w, hy_conv_b, hy_f_w1, hy_f_b1, hy_f_w2, hy_f_b2, hy_f_w3, hy_f_b3, hy_f_freq, hy_f_w4, hy_bias, hy_norm_g, w_out, ln1_g, ln1_b, ln2_g, ln2_b, moe_w_group, moe_b_group, moe_w_expert, moe_b_expert, moe_w_gate, moe_w_up, moe_w_down):
    xc = ctx
    for l in range(DEPTH):
        w_in_l = w_in[l]
        w_ab = jnp.zeros((D_MODEL, LANE), F32).at[:, :4 * DN_HEADS].set(w_in_l[:, COL_DN_A:STATE_COLS])
        w_router = jnp.zeros((D_MODEL, LANE), F32).at[:, :N_GROUPS].set(moe_w_group[l]).at[:, N_GROUPS:N_GROUPS + N_EXPERTS].set(moe_w_expert[l])
        p = {
            'layer': l, 'w_ada': w_ada, 'b_ada': b_ada[l],
            'w_in_s5': w_in_l[:, COL_S5:COL_DN_QKV].astype(BF16),
            'w_in_qkv': w_in_l[:, COL_DN_QKV:COL_DN_A].astype(BF16),
            'w_in_ab': w_ab.astype(BF16),
            'w_in_g': w_in_l[:, COL_DN_G:COL_HY].astype(BF16),
            'w_in_hy': w_in_l[:, COL_HY:].astype(BF16),
            's5_lam_re': s5_lam_re[l], 's5_lam_im': s5_lam_im[l], 's5_log_step': s5_log_step[l],
            's5_b_re': s5_b_re[l], 's5_b_im': s5_b_im[l], 's5_c_re': s5_c_re[l], 's5_c_im': s5_c_im[l],
            's5_d': s5_d[l], 's5_glu_w': s5_glu_w[l].astype(BF16), 's5_glu_b': s5_glu_b[l], 's5_norm_g': s5_norm_g[l],
            'dn_conv_w': dn_conv_w[l], 'dn_a_log': dn_a_log[l], 'dn_dt_bias': dn_dt_bias[l], 'dn_norm_g': dn_norm_g[l],
            'hy_conv_w': hy_conv_w[l], 'hy_conv_b': hy_conv_b[l],
            'hy_f_w1': hy_f_w1[l], 'hy_f_b1': hy_f_b1[l], 'hy_f_w2': hy_f_w2[l], 'hy_f_b2': hy_f_b2[l],
            'hy_f_w3': hy_f_w3[l], 'hy_f_b3': hy_f_b3[l], 'hy_f_freq': hy_f_freq[l], 'hy_f_w4': hy_f_w4[l],
            'hy_bias': hy_bias[l], 'hy_norm_g': hy_norm_g[l], 'w_out': w_out[l].astype(BF16),
            'ln1_g': ln1_g[l], 'ln1_b': ln1_b[l], 'ln2_g': ln2_g[l], 'ln2_b': ln2_b[l],
            'moe_w_router': w_router, 'moe_b_group': moe_b_group[l], 'moe_b_expert': moe_b_expert[l],
            'moe_w_gate': moe_w_gate, 'moe_w_up': moe_w_up, 'moe_w_down': moe_w_down,
        }
        x, xc = trunk_layer(x, xc, c, c_ctx, p, l == DEPTH - 1)
    return x
```

```python
import functools
import math

import jax
import jax.numpy as jnp
from jax import lax
from jax.experimental import pallas as pl
from jax.experimental.pallas import tpu as pltpu

D_MODEL = 4096
DEPTH = 2
GRID_W = 64

S5_WIDTH = D_MODEL // 4
S5_GROUP = 16
S5_GROUPS = S5_WIDTH // S5_GROUP
S5_STATE = 64

DN_WIDTH = D_MODEL // 2
DN_HEAD_DIM = 128
DN_HEADS = DN_WIDTH // DN_HEAD_DIM
DN_CHUNK = 64
SHORT_CONV = 3

HY_WIDTH = D_MODEL - S5_WIDTH - DN_WIDTH
HY_BANDS = 16
HY_EMB = 1 + 2 * HY_BANDS
HY_HIDDEN = 64
HY_DECAY_SHORT_PCT = 0.3
HY_DECAY_LONG_PCT = 1.5
HY_DECAY_TARGET = 1e-2

MIX_WIDTH = S5_WIDTH + DN_WIDTH + HY_WIDTH

COL_S5 = 0
COL_DN_QKV = COL_S5 + S5_WIDTH
COL_DN_A = COL_DN_QKV + 3 * DN_WIDTH
COL_DN_B = COL_DN_A + 2 * DN_HEADS
STATE_COLS = COL_DN_B + 2 * DN_HEADS
COL_DN_G = STATE_COLS
COL_HY = COL_DN_G + DN_WIDTH
IN_WIDTH = COL_HY + 3 * HY_WIDTH

N_GROUPS = 4
EXPERTS_PER_GROUP = 8
N_EXPERTS = N_GROUPS * EXPERTS_PER_GROUP
TOP_K = 2
D_EXPERT = 512

DEEPNORM_ALPHA = (2 * DEPTH) ** 0.25
LN_EPS = 1e-5
RMS_EPS = 1e-6
F32 = jnp.float32
BF16 = jnp.bfloat16

V7X_VMEM_BYTES = 64 * 1024 * 1024
VMEM_LIMIT = 52 * 1024 * 1024
MOE_VMEM_LIMIT = 56 * 1024 * 1024
LANE = 128

MOE_TILE = 256
DN_HEAD_GROUP = 4
DN_SPLIT_LEVELS = 2
S5_T = 64


def _mm_kernel(a_ref, b_ref, o_ref):
    o_ref[...] = jnp.dot(a_ref[...].astype(BF16), b_ref[...].astype(BF16), preferred_element_type=F32).astype(o_ref.dtype)


def _pick(n, prefs):
    for p in prefs:
        if n % p == 0:
            return p
    return n


def matmul(a, b, out_dtype=F32, name="matmul"):
    m, k = a.shape
    n = b.shape[1]
    tm = _pick(m, (512, 256, 128, 64, 32, 16, 8))
    tn = _pick(n, (1024, 512, 256, 128))
    return pl.pallas_call(
        _mm_kernel,
        grid=(m // tm, n // tn),
        in_specs=[pl.BlockSpec((tm, k), lambda i, j: (i, 0)), pl.BlockSpec((k, tn), lambda i, j: (0, j))],
        out_specs=pl.BlockSpec((tm, tn), lambda i, j: (i, j)),
        out_shape=jax.ShapeDtypeStruct((m, n), out_dtype),
        compiler_params=pltpu.CompilerParams(dimension_semantics=("parallel", "parallel"), vmem_limit_bytes=VMEM_LIMIT),
        name=name,
    )(a, b)


def _mm_f32_kernel(a_ref, b_ref, o_ref):
    o_ref[...] = _hdot(_split(a_ref[...]), _split(b_ref[...]))


def matmul_f32(a, b, name="matmul_f32"):
    m, k = a.shape
    n = b.shape[1]
    tm = _pick(m, (512, 256, 128, 64, 32, 16, 8))
    return pl.pallas_call(
        _mm_f32_kernel,
        grid=(m // tm,),
        in_specs=[pl.BlockSpec((tm, k), lambda i: (i, 0)), pl.BlockSpec((k, n), lambda i: (0, 0))],
        out_specs=pl.BlockSpec((tm, n), lambda i: (i, 0)),
        out_shape=jax.ShapeDtypeStruct((m, n), F32),
        compiler_params=pltpu.CompilerParams(dimension_semantics=("parallel",), vmem_limit_bytes=VMEM_LIMIT),
        name=name,
    )(a, b)


def _moe_kernel(tile_expert_ref, n_used_ref, x_ref, w_ref, wg_ref, wu_ref, wd_ref, o_ref, wg_s, wu_s, wd_s):
    i = pl.program_id(0)
    prev = tile_expert_ref[jnp.maximum(i - 1, 0)]

    @pl.when((i == 0) | (tile_expert_ref[i] != prev))
    def _():
        wg_s[...] = wg_ref[0, 0].astype(BF16)
        wu_s[...] = wu_ref[0, 0].astype(BF16)
        wd_s[...] = wd_ref[0, 0].astype(BF16)

    @pl.when(i < n_used_ref[0])
    def _():
        x = x_ref[...]
        g = jnp.dot(x, wg_s[...], preferred_element_type=F32)
        u = jnp.dot(x, wu_s[...], preferred_element_type=F32)
        h = (g * jax.nn.sigmoid(g)) * u * w_ref[...]
        o_ref[...] = jnp.dot(h.astype(BF16), wd_s[...], preferred_element_type=F32).astype(o_ref.dtype)

    @pl.when(i >= n_used_ref[0])
    def _():
        o_ref[...] = jnp.zeros_like(o_ref)


def moe_experts(tile_expert, n_used, xs, row_w, wg, wu, wd, layer):
    p_rows, dm = xs.shape
    n_tiles = p_rows // MOE_TILE
    single = pl.Buffered(1)
    grid_spec = pltpu.PrefetchScalarGridSpec(
        num_scalar_prefetch=2,
        grid=(n_tiles,),
        in_specs=[
            pl.BlockSpec((MOE_TILE, dm), lambda i, te, nu: (i, 0)),
            pl.BlockSpec((MOE_TILE, 1), lambda i, te, nu: (i, 0)),
            pl.BlockSpec((1, 1, dm, D_EXPERT), lambda i, te, nu: (layer, te[i], 0, 0), pipeline_mode=single),
            pl.BlockSpec((1, 1, dm, D_EXPERT), lambda i, te, nu: (layer, te[i], 0, 0), pipeline_mode=single),
            pl.BlockSpec((1, 1, D_EXPERT, dm), lambda i, te, nu: (layer, te[i], 0, 0), pipeline_mode=single),
        ],
        out_specs=pl.BlockSpec((MOE_TILE, dm), lambda i, te, nu: (i, 0)),
        scratch_shapes=[pltpu.VMEM((dm, D_EXPERT), BF16), pltpu.VMEM((dm, D_EXPERT), BF16), pltpu.VMEM((D_EXPERT, dm), BF16)],
    )
    return pl.pallas_call(
        _moe_kernel,
        grid_spec=grid_spec,
        out_shape=jax.ShapeDtypeStruct((p_rows, dm), BF16),
        compiler_params=pltpu.CompilerParams(dimension_semantics=("arbitrary",), vmem_limit_bytes=MOE_VMEM_LIMIT),
        name="moe_experts",
    )(tile_expert, n_used, xs, row_w, wg, wu, wd)


def hier_moe(t, p):
    n_tok = t.shape[0]
    logits = matmul_f32(t, p['moe_w_router'], name="moe_router")
    group_logits = logits[:, :N_GROUPS] + p['moe_b_group']
    group = jnp.argmax(group_logits, axis=-1)
    group_w = jnp.max(jax.nn.softmax(group_logits, axis=-1), axis=-1, keepdims=True)
    exp_logits = (logits[:, N_GROUPS:N_GROUPS + N_EXPERTS] + p['moe_b_expert']).reshape(n_tok, N_GROUPS, EXPERTS_PER_GROUP)
    exp_logits = jnp.take_along_axis(exp_logits, group[:, None, None], axis=1)[:, 0]
    top_logits, top_idx = lax.top_k(exp_logits, TOP_K)
    top_w = jax.nn.softmax(top_logits, axis=-1) * group_w
    expert_id = (group[:, None] * EXPERTS_PER_GROUP + top_idx).astype(jnp.int32)

    n_rows = n_tok * TOP_K
    flat_e = expert_id.reshape(n_rows)
    flat_w = top_w.reshape(n_rows)
    flat_t = jnp.arange(n_rows, dtype=jnp.int32) // TOP_K
    order = jnp.argsort(flat_e, stable=True)
    sorted_e = flat_e[order]
    counts = jnp.zeros((N_EXPERTS,), jnp.int32).at[flat_e].add(1)
    padded = ((counts + MOE_TILE - 1) // MOE_TILE) * MOE_TILE
    pad_end = jnp.cumsum(padded)
    pad_start = pad_end - padded
    start = jnp.cumsum(counts) - counts
    dest = pad_start[sorted_e] + (jnp.arange(n_rows, dtype=jnp.int32) - start[sorted_e])
    p_rows = n_rows + N_EXPERTS * MOE_TILE
    n_tiles = p_rows // MOE_TILE
    row_token = jnp.zeros((p_rows,), jnp.int32).at[dest].set(flat_t[order])
    row_w = jnp.zeros((p_rows,), F32).at[dest].set(flat_w[order])
    pos = jnp.zeros((n_rows,), jnp.int32).at[order].set(dest).reshape(n_tok, TOP_K)
    n_used = (pad_end[-1] // MOE_TILE).astype(jnp.int32)
    tile_idx = jnp.minimum(jnp.arange(n_tiles, dtype=jnp.int32), n_used - 1)
    tile_expert = jnp.minimum(jnp.searchsorted(pad_end, tile_idx * MOE_TILE, side='right'), N_EXPERTS - 1).astype(jnp.int32)

    xs = t.astype(BF16)[row_token]
    ys = moe_experts(tile_expert, n_used.reshape(1), xs, row_w[:, None], p['moe_w_gate'], p['moe_w_up'], p['moe_w_down'], p['layer'])
    return ys[pos[:, 0]].astype(F32) + ys[pos[:, 1]].astype(F32)


def _flip(t, rev):
    return jnp.flip(t, axis=1) if rev else t


def layer_norm(x, g, b):
    mu = jnp.mean(x, axis=-1, keepdims=True)
    var = jnp.mean(jnp.square(x - mu), axis=-1, keepdims=True)
    return (x - mu) * lax.rsqrt(var + LN_EPS) * g + b


def rms_norm(x, g):
    return x * lax.rsqrt(jnp.mean(jnp.square(x), axis=-1, keepdims=True) + RMS_EPS) * g


def l2_normalize(t):
    return t * lax.rsqrt(jnp.sum(t * t, axis=-1, keepdims=True) + 1e-6)


def depthwise_conv(u, w):
    k = w.shape[0]
    return lax.conv_general_dilated(u, w[:, None, :], window_strides=(1,), padding=[(k // 2, k // 2)], dimension_numbers=('NWC', 'WIO', 'NWC'), feature_group_count=u.shape[-1])


def modulate(t, shift, scale):
    return t * (1 + scale) + shift


def mm3(t, w, name):
    bsz, n, k = t.shape
    return matmul(t.reshape(bsz * n, k), w, name=name).reshape(bsz, n, w.shape[1])


def s5_discretize(lam_re, lam_im, log_step, b_re, b_im):
    step = jnp.exp(log_step)[:, None]
    mag = jnp.exp(lam_re * step)
    a_re = mag * jnp.cos(lam_im * step)
    a_im = mag * jnp.sin(lam_im * step)
    den = lam_re * lam_re + lam_im * lam_im
    f_re = ((a_re - 1.0) * lam_re + a_im * lam_im) / den
    f_im = (a_im * lam_re - (a_re - 1.0) * lam_im) / den
    bb_re = f_re[..., None] * b_re - f_im[..., None] * b_im
    bb_im = f_re[..., None] * b_im + f_im[..., None] * b_re
    return a_re, a_im, bb_re, bb_im


def _cmul(x, y):
    return x[0] * y[0] - x[1] * y[1], x[0] * y[1] + x[1] * y[0]


def s5_tables(p):
    hp = lax.Precision.HIGHEST
    t_len, grp, st = S5_T, S5_GROUPS, S5_STATE
    taps, wst, rd, pq = [], [], [], []
    for d in range(2):
        a_re, a_im, bb_re, bb_im = s5_discretize(p['s5_lam_re'][d], p['s5_lam_im'][d], p['s5_log_step'][d], p['s5_b_re'][d], p['s5_b_im'][d])
        c_re, c_im = p['s5_c_re'][d], p['s5_c_im'][d]
        pr, pi = lax.associative_scan(_cmul, (jnp.broadcast_to(a_re, (t_len, grp, st)), jnp.broadcast_to(a_im, (t_len, grp, st))), axis=0)
        pw_re = jnp.concatenate([jnp.ones((1, grp, st), F32), pr], axis=0)
        pw_im = jnp.concatenate([jnp.zeros((1, grp, st), F32), pi], axis=0)
        ca_re = c_re[None] * pw_re[:, :, None, :] - c_im[None] * pw_im[:, :, None, :]
        ca_im = c_re[None] * pw_im[:, :, None, :] + c_im[None] * pw_re[:, :, None, :]
        taps.append(jnp.einsum('tgip,gpj->tgij', ca_re[:t_len], bb_re, precision=hp) - jnp.einsum('tgip,gpj->tgij', ca_im[:t_len], bb_im, precision=hp))
        e_st = jnp.arange(t_len - 1, -1, -1) if d == 0 else jnp.arange(t_len)
        w_re = pw_re[e_st][..., None] * bb_re[None] - pw_im[e_st][..., None] * bb_im[None]
        w_im = pw_re[e_st][..., None] * bb_im[None] + pw_im[e_st][..., None] * bb_re[None]
        to_rows = lambda w: jnp.transpose(w, (1, 0, 3, 2)).reshape(grp, t_len * S5_GROUP, st)
        wst += [to_rows(w_re), to_rows(w_im)]
        e_rd = jnp.arange(1, t_len + 1) if d == 0 else jnp.arange(t_len, 0, -1)
        to_cols = lambda r: jnp.transpose(r, (1, 3, 0, 2)).reshape(grp, st, t_len * S5_GROUP)
        rd += [to_cols(ca_re[e_rd]), to_cols(-ca_im[e_rd])]
        lv_re, lv_im = pw_re[t_len], pw_im[t_len]
        for _ in range(8):
            pq += [jnp.concatenate([lv_re, lv_re], axis=-1), jnp.concatenate([-lv_im, lv_im], axis=-1)]
            lv_re, lv_im = _cmul((lv_re, lv_im), (lv_re, lv_im))
    kf, kb = taps
    blocks = jnp.concatenate([kb[1:][::-1], (kf[0] + kb[0])[None], kf[1:], jnp.zeros_like(kf[:1])], axis=0)
    ext = jnp.transpose(blocks, (1, 3, 0, 2)).reshape(grp, S5_GROUP, 2 * t_len * S5_GROUP)
    return ext, jnp.concatenate(wst, axis=-1).astype(BF16), jnp.concatenate(rd, axis=1).astype(BF16), jnp.stack(pq, axis=1)


def _s5_kernel(u_ref, ext_ref, wst_ref, rd_ref, pq_ref, h0_ref, y_ref, hfin_ref, m_ref, *, n_chunks, bsz):
    cw = S5_T * S5_GROUP
    st2 = 2 * S5_STATE
    ext = ext_ref[0]
    for s in range(S5_T):
        off = S5_GROUP * (S5_T - 1 - s)
        shifted = ext if off == 0 else pltpu.roll(ext, shift=2 * cw - off, axis=1)
        m_ref[S5_GROUP * s:S5_GROUP * (s + 1), :] = shifted[:, :cw].astype(BF16)
    u = u_ref[0]
    rows = u.shape[0]
    y = jnp.dot(u, m_ref[...], preferred_element_type=F32)
    s_all = jnp.dot(u, wst_ref[0], preferred_element_type=F32)
    sf, sb = s_all[:, :st2], s_all[:, st2:]
    row = lax.broadcasted_iota(jnp.int32, (rows, st2), 0)
    c_idx = row % n_chunks

    def cmul_rows(idx, x):
        return pq_ref[0, idx:idx + 1, :] * x + pq_ref[0, idx + 1:idx + 2, :] * pltpu.roll(x, shift=S5_STATE, axis=1)

    h0 = h0_ref[0]
    h0f = jnp.zeros((rows, st2), F32)
    h0b = jnp.zeros((rows, st2), F32)
    for b in range(bsz):
        in_b = (row >= b * n_chunks) & (row < (b + 1) * n_chunks)
        h0f = jnp.where(in_b, h0[b:b + 1, :st2], h0f)
        h0b = jnp.where(in_b, h0[b:b + 1, st2:], h0b)
    hf = jnp.where(c_idx == 0, h0f, pltpu.roll(sf, shift=1, axis=0))
    hb = jnp.where(c_idx == n_chunks - 1, h0b, pltpu.roll(sb, shift=rows - 1, axis=0))
    level, sh = 0, 1
    while sh < n_chunks:
        dn = jnp.where(c_idx >= sh, pltpu.roll(hf, shift=sh, axis=0), 0.0)
        up = jnp.where(c_idx < n_chunks - sh, pltpu.roll(hb, shift=rows - sh, axis=0), 0.0)
        hf = hf + cmul_rows(2 * level, dn)
        hb = hb + cmul_rows(16 + 2 * level, up)
        level, sh = level + 1, sh * 2
    h_in = jnp.concatenate([hf, hb], axis=1)
    y_ref[0] = y + jnp.dot(h_in.astype(BF16), rd_ref[0], preferred_element_type=F32)
    hfin_ref[0] = jnp.concatenate([cmul_rows(0, hf) + sf, cmul_rows(16, hb) + sb], axis=1)


def s5_scan(u, tables, h0, n_chunks, bsz):
    ext, wst, rd, pq = tables
    grp, rows, cw = u.shape
    st4 = 4 * S5_STATE
    blk = lambda *shape: pl.BlockSpec((1,) + shape, lambda i: (i, 0, 0))
    return pl.pallas_call(
        functools.partial(_s5_kernel, n_chunks=n_chunks, bsz=bsz),
        grid=(grp,),
        in_specs=[blk(rows, cw), blk(S5_GROUP, 2 * cw), blk(cw, st4), blk(st4, cw), blk(32, 2 * S5_STATE), blk(bsz, st4)],
        out_specs=[blk(rows, cw), blk(rows, st4)],
        out_shape=[jax.ShapeDtypeStruct((grp, rows, cw), F32), jax.ShapeDtypeStruct((grp, rows, st4), F32)],
        scratch_shapes=[pltpu.VMEM((cw, cw), BF16)],
        compiler_params=pltpu.CompilerParams(dimension_semantics=("parallel",), vmem_limit_bytes=VMEM_LIMIT),
        name="s5_scan",
    )(u, ext, wst, rd, pq, h0)


def s5_sequence(u, tables, h0):
    bsz, n, _ = u.shape
    n_chunks = n // S5_T
    rows = bsz * n_chunks
    rows_pad = -(-rows // 16) * 16
    ug = u.astype(BF16).reshape(bsz, n_chunks, S5_T, S5_GROUPS, S5_GROUP)
    ug = jnp.transpose(ug, (3, 0, 1, 2, 4)).reshape(S5_GROUPS, rows, S5_T * S5_GROUP)
    if rows_pad != rows:
        ug = jnp.pad(ug, ((0, 0), (0, rows_pad - rows), (0, 0)))
    y, hfin = s5_scan(ug, tables, h0, n_chunks, bsz)
    y = y[:, :rows].reshape(S5_GROUPS, bsz, n_chunks, S5_T, S5_GROUP)
    y = jnp.transpose(y, (1, 2, 3, 0, 4)).reshape(bsz, n, S5_WIDTH)
    hfin = hfin[:, :rows].reshape(S5_GROUPS, bsz, n_chunks, 4 * S5_STATE)
    st2 = 2 * S5_STATE
    h_end = jnp.concatenate([hfin[:, :, n_chunks - 1, :st2], hfin[:, :, 0, st2:]], axis=-1)
    return y, h_end


def s5_glu(y, p):
    y = jax.nn.gelu(y)
    y = y * jax.nn.sigmoid(mm3(y, p['s5_glu_w'], "s5_glu") + p['s5_glu_b'])
    return rms_norm(y, p['s5_norm_g'])


def s5_mixer(u, uc, p, need_ctx_out):
    bsz = u.shape[0]
    tables = s5_tables(p)
    yc, h_ctx = s5_sequence(uc, tables, jnp.zeros((S5_GROUPS, bsz, 4 * S5_STATE), F32))
    y, _ = s5_sequence(u, tables, h_ctx)
    out = s5_glu(u * p['s5_d'] + y, p)
    out_c = s5_glu(uc * p['s5_d'] + yc, p) if need_ctx_out else None
    return out, out_c


def dn_features(z_qkv, conv_w):
    bsz, seq_len, _ = z_qkv.shape
    q, k, v = jnp.split(jax.nn.silu(depthwise_conv(z_qkv, conv_w)), 3, axis=-1)
    shp = (bsz, seq_len, DN_HEADS, DN_HEAD_DIM)
    return l2_normalize(q.reshape(shp)) * DN_HEAD_DIM ** -0.5, l2_normalize(k.reshape(shp)), v.reshape(shp)


def dn_gates(z_a, z_b, a_log, dt_bias):
    bsz, seq_len, _ = z_a.shape
    shp = (bsz, seq_len, 2, DN_HEADS)
    g = -jnp.exp(a_log) * jax.nn.softplus(z_a.reshape(shp) + dt_bias)
    beta = jax.nn.sigmoid(z_b.reshape(shp))
    return g, beta


def _dn_kernel(q_ref, k_ref, v_ref, g_ref, gt_ref, b_ref, s0_ref, o_ref, sfin_ref, s_ref, *, n_chunks):
    cs, hd, hg = DN_CHUNK, DN_HEAD_DIM, DN_HEAD_GROUP
    rows = cs * hg
    n_groups = DN_HEADS // hg
    groups = range(n_groups)
    step = pl.program_id(1)

    @pl.when(step == 0)
    def _():
        s_ref[...] = s0_ref[0]

    r_i = lax.broadcasted_iota(jnp.int32, (rows, rows), 0)
    c_i = lax.broadcasted_iota(jnp.int32, (rows, rows), 1)
    same_head = (r_i // cs) == (c_i // cs)
    incl = same_head & ((r_i % cs) >= (c_i % cs))
    strict = same_head & ((r_i % cs) > (c_i % cs))
    upto = same_head & ((r_i % cs) <= (c_i % cs))
    r_c = lax.broadcasted_iota(jnp.int32, (cs, cs), 0)
    c_c = lax.broadcasted_iota(jnp.int32, (cs, cs), 1)
    hp = lax.Precision.HIGHEST
    beta = b_ref[0, 0]
    gc = jnp.dot((r_c >= c_c).astype(F32), g_ref[0, 0], preferred_element_type=F32, precision=hp)
    gc_row = jnp.dot(gt_ref[0, 0], upto.astype(F32), preferred_element_type=F32, precision=hp)
    g_tot = gc[cs - 1:cs, :]
    e_g = jnp.exp(gc)
    e_tail = jnp.exp(g_tot - gc)
    e_tot = jnp.exp(g_tot)

    def head_cols(h):
        return slice(h * hd, (h + 1) * hd)

    def stack_heads(grp, fn):
        return jnp.concatenate([fn(grp * hg + j) for j in range(hg)], axis=0)

    def col(x, h, width):
        return jnp.broadcast_to(x[:, h:h + 1], (cs, width))

    def mm(a, b):
        return jnp.dot(a.astype(BF16), b.astype(BF16), preferred_element_type=F32)

    def mm_split_rhs(a_bf16, b):
        hi = b.astype(BF16)
        lo = (b - hi.astype(F32)).astype(BF16)
        t = jnp.dot(a_bf16, jnp.concatenate([hi, lo], axis=1), preferred_element_type=F32)
        return t[:, :b.shape[1]] + t[:, b.shape[1]:]

    kst = [stack_heads(gr, lambda h: k_ref[0, :, head_cols(h)]) for gr in groups]
    qst = [stack_heads(gr, lambda h: q_ref[0, :, head_cols(h)]) for gr in groups]
    b_col = [stack_heads(gr, lambda h: col(beta, h, hd)) for gr in groups]
    eg_col = [stack_heads(gr, lambda h: col(e_g, h, hd)) for gr in groups]
    kb = [kst[gr] * b_col[gr] for gr in groups]
    decay = []
    for gr in groups:
        diff = stack_heads(gr, lambda h: col(gc, h, rows)) - gc_row[gr:gr + 1, :]
        decay.append(jnp.where(incl, jnp.exp(jnp.where(incl, diff, 0.0)), 0.0))
    kq = [lax.dot_general(jnp.concatenate([kb[gr], qst[gr]], axis=0).astype(BF16), kst[gr].astype(BF16),
                          (((1,), (1,)), ((), ())), preferred_element_type=F32) for gr in groups]
    n = [jnp.where(strict, kq[gr][:rows] * decay[gr], 0.0).astype(BF16) for gr in groups]
    qk = [(kq[gr][rows:] * decay[gr]).astype(BF16) for gr in groups]
    x = [jnp.concatenate([stack_heads(gr, lambda h: v_ref[0, :, head_cols(h)]) * b_col[gr], kb[gr] * eg_col[gr]], axis=1) for gr in groups]
    x = [x[gr] - mm_split_rhs(n[gr], x[gr]) for gr in groups]
    pw = n
    for level in range(5):
        pw = [jnp.dot(pw[gr], pw[gr], preferred_element_type=F32).astype(BF16) for gr in groups]
        if level < DN_SPLIT_LEVELS:
            x = [x[gr] + mm_split_rhs(pw[gr], x[gr]) for gr in groups]
        else:
            x = [x[gr] + jnp.dot(pw[gr], x[gr].astype(BF16), preferred_element_type=F32) for gr in groups]
    qd = [qst[gr] * eg_col[gr] for gr in groups]
    for gr in groups:
        v_new, o_state = [], []
        for j in range(hg):
            h, rs = gr * hg + j, slice(j * cs, (j + 1) * cs)
            ws = mm(jnp.concatenate([x[gr][rs, hd:], qd[gr][rs]], axis=0), s_ref[h])
            v_new.append(x[gr][rs, :hd] - ws[:cs])
            o_state.append(ws[cs:])
        v_all = jnp.concatenate(v_new, axis=0).astype(BF16)
        o_all = jnp.concatenate(o_state, axis=0) + jnp.dot(qk[gr], v_all, preferred_element_type=F32)
        for j in range(hg):
            h, rs = gr * hg + j, slice(j * cs, (j + 1) * cs)
            o_ref[0, :, head_cols(h)] = o_all[rs]
            kt = (kst[gr][rs] * col(e_tail, h, hd)).astype(BF16)
            s_ref[h] = s_ref[h] * e_tot[:, h:h + 1] + lax.dot_general(kt, v_all[rs], (((0,), (0,)), ((), ())), preferred_element_type=F32)

    @pl.when(step == n_chunks - 1)
    def _():
        sfin_ref[0] = s_ref[...]


def gated_delta_rule(q, k, v, g, beta, s0):
    bsz, seq_len, heads, hd = q.shape
    width = heads * hd
    n_chunks = seq_len // DN_CHUNK
    q, k, v = (t.reshape(bsz, seq_len, width) for t in (q, k, v))
    g4 = g.reshape(bsz, n_chunks, DN_CHUNK, heads)
    gt4 = jnp.swapaxes(g4, 2, 3).reshape(bsz, n_chunks, heads // DN_HEAD_GROUP, DN_HEAD_GROUP * DN_CHUNK)
    b4 = beta.reshape(bsz, n_chunks, DN_CHUNK, heads)
    tok_spec = pl.BlockSpec((1, DN_CHUNK, width), lambda b, i: (b, i, 0))
    gate_spec = pl.BlockSpec((1, 1, DN_CHUNK, heads), lambda b, i: (b, i, 0, 0))
    gate_t_spec = pl.BlockSpec((1, 1, heads // DN_HEAD_GROUP, DN_HEAD_GROUP * DN_CHUNK), lambda b, i: (b, i, 0, 0))
    state_spec = pl.BlockSpec((1, heads, hd, hd), lambda b, i: (b, 0, 0, 0))
    o, s_fin = pl.pallas_call(
        functools.partial(_dn_kernel, n_chunks=n_chunks),
        grid=(bsz, n_chunks),
        in_specs=[tok_spec, tok_spec, tok_spec, gate_spec, gate_t_spec, gate_spec, state_spec],
        out_specs=[tok_spec, state_spec],
        out_shape=[jax.ShapeDtypeStruct((bsz, seq_len, width), F32), jax.ShapeDtypeStruct(s0.shape, F32)],
        scratch_shapes=[pltpu.VMEM((heads, hd, hd), F32)],
        compiler_params=pltpu.CompilerParams(dimension_semantics=("parallel", "arbitrary"), vmem_limit_bytes=VMEM_LIMIT),
        name="delta_rule",
    )(q, k, v, g4, gt4, b4, s0)
    return o.reshape(bsz, seq_len, heads, hd), s_fin


def dn_output(o, z_gate, p):
    bsz, seq_len = o.shape[:2]
    gate = jax.nn.silu(z_gate).reshape(bsz, seq_len, DN_HEADS, DN_HEAD_DIM)
    return (rms_norm(o, p['dn_norm_g']) * gate).reshape(bsz, seq_len, DN_WIDTH)


def deltanet_mixer(z, zc, p, need_ctx_out):
    q, k, v = dn_features(z['qkv'], p['dn_conv_w'])
    g, beta = dn_gates(z['a'], z['b'], p['dn_a_log'], p['dn_dt_bias'])
    qc, kc, vc = dn_features(zc['qkv'], p['dn_conv_w'])
    gc, betac = dn_gates(zc['a'], zc['b'], p['dn_a_log'], p['dn_dt_bias'])
    s0 = jnp.zeros((q.shape[0], DN_HEADS, DN_HEAD_DIM, DN_HEAD_DIM), F32)
    o, oc = 0.0, 0.0
    for d in range(2):
        rev = d == 1
        oc_d, s_ctx = gated_delta_rule(*(_flip(t, rev) for t in (qc, kc, vc, gc[:, :, d], betac[:, :, d])), s0)
        o_d, _ = gated_delta_rule(*(_flip(t, rev) for t in (q, k, v, g[:, :, d], beta[:, :, d])), s_ctx)
        o = o + _flip(o_d, rev)
        if need_ctx_out:
            oc = oc + _flip(oc_d, rev)
    y = dn_output(o, z['g'], p)
    yc = dn_output(oc, zc['g'], p) if need_ctx_out else None
    return y, yc


def hyena_filters(seq_len, p):
    hp = lax.Precision.HIGHEST
    t = jnp.linspace(0.0, 1.0, seq_len, dtype=F32)[:, None]
    w = (2.0 * math.pi / seq_len) * jnp.arange(seq_len, dtype=F32)[:, None]
    f = jnp.linspace(1e-4, HY_BANDS - 1, HY_BANDS, dtype=F32)[None, :]
    feats = jnp.concatenate([t, jnp.cos(w * f), -jnp.sin(w * f)], axis=-1)
    freq = p['hy_f_freq']
    h = jnp.sin(freq * (jnp.dot(feats, p['hy_f_w1'], precision=hp) + p['hy_f_b1']))
    h = jnp.sin(freq * (jnp.dot(h, p['hy_f_w2'], precision=hp) + p['hy_f_b2']))
    h = jnp.sin(freq * (jnp.dot(h, p['hy_f_w3'], precision=hp) + p['hy_f_b3']))
    h = jnp.dot(h, p['hy_f_w4'], precision=hp).reshape(seq_len, 2, HY_WIDTH)
    rates = jnp.linspace(math.log(HY_DECAY_TARGET) / HY_DECAY_LONG_PCT, math.log(HY_DECAY_TARGET) / HY_DECAY_SHORT_PCT, HY_WIDTH, dtype=F32)
    h = h * jnp.exp(-t * jnp.abs(rates))[:, None, :]
    return h[:, 0], h[:, 1]


def two_sided_long_conv(u, h_fwd, h_bwd):
    seq_len = u.shape[1]
    taps = jnp.concatenate([h_fwd[:1] + h_bwd[:1], h_fwd[1:], jnp.zeros_like(h_fwd[:1]), h_bwd[:0:-1]], axis=0)
    u_f = jnp.fft.rfft(u, n=2 * seq_len, axis=1)
    t_f = jnp.fft.rfft(taps, axis=0)
    return jnp.fft.irfft(u_f * t_f[None], n=2 * seq_len, axis=1)[:, :seq_len]


FFT_R = 128
FFT_COLS = 8192


def _dft_tables():
    n, big = FFT_R, FFT_R * FFT_R
    idx = jnp.arange(n, dtype=jnp.int32)
    ang = (2.0 * math.pi / n) * ((idx[:, None] * idx[None, :]) % n).astype(F32)
    m = (idx[None, None, :] * (n * idx[None, :, None] + idx[:, None, None])) % big
    g_ang = (2.0 * math.pi / big) * m.astype(F32)
    return jnp.cos(ang), -jnp.sin(ang), jnp.cos(g_ang), -jnp.sin(g_ang)


def _split(x):
    hi = x.astype(BF16)
    return hi, (x - hi.astype(F32)).astype(BF16)


def _hdot(a, b):
    (a_hi, a_lo), (b_hi, b_lo) = a, b
    m = a_hi.shape[0]
    t = jnp.dot(jnp.concatenate([a_hi, a_lo], axis=0), b_hi, preferred_element_type=F32)
    return t[:m] + t[m:] + jnp.dot(a_hi, b_lo, preferred_element_type=F32)


def _fft_rows_kernel(f_re_ref, f_im_ref, x_ref, a_re_ref, a_im_ref):
    x = _split(x_ref[0])
    a_re_ref[0] = _hdot(_split(f_re_ref[...]), x)
    a_im_ref[0] = _hdot(_split(f_im_ref[...]), x)


def fft_rows(x, f_re, f_im):
    bsz, k, m = x.shape
    out = jax.ShapeDtypeStruct((bsz, FFT_R, m), F32)
    f_spec = pl.BlockSpec((FFT_R, k), lambda b, j: (0, 0))
    o_spec = pl.BlockSpec((1, FFT_R, FFT_COLS), lambda b, j: (b, 0, j))
    return pl.pallas_call(
        _fft_rows_kernel,
        grid=(bsz, m // FFT_COLS),
        in_specs=[f_spec, f_spec, pl.BlockSpec((1, k, FFT_COLS), lambda b, j: (b, 0, j))],
        out_specs=[o_spec, o_spec],
        out_shape=[out, out],
        compiler_params=pltpu.CompilerParams(dimension_semantics=("parallel", "parallel"), vmem_limit_bytes=VMEM_LIMIT),
        name="fft_rows",
    )(f_re, f_im, x)


def _fft_spectrum_kernel(g_re_ref, g_im_ref, a_re_ref, a_im_ref, x_re_ref, x_im_ref):
    gr, gi, ar, ai = (_split(t) for t in (g_re_ref[0], g_im_ref[0], a_re_ref[0, 0], a_im_ref[0, 0]))
    x_re_ref[0, 0] = _hdot(gr, ar) - _hdot(gi, ai)
    x_im_ref[0, 0] = _hdot(gr, ai) + _hdot(gi, ar)


def _fft_mid_kernel(g_re_ref, g_im_ref, gt_re_ref, gt_im_ref, a_re_ref, a_im_ref, h_re_ref, h_im_ref, b_re_ref, b_im_ref):
    gr, gi, tr, ti = (_split(t[0]) for t in (g_re_ref, g_im_ref, gt_re_ref, gt_im_ref))
    hr, hi = h_re_ref[0, 0], h_im_ref[0, 0]
    for b in range(a_re_ref.shape[0]):
        ar, ai = _split(a_re_ref[b, 0]), _split(a_im_ref[b, 0])
        xr = _hdot(gr, ar) - _hdot(gi, ai)
        xi = _hdot(gr, ai) + _hdot(gi, ar)
        yr = _split(xr * hr - xi * hi)
        yi = _split(xr * hi + xi * hr)
        b_re_ref[b, 0] = _hdot(tr, yr) + _hdot(ti, yi)
        b_im_ref[b, 0] = _hdot(tr, yi) - _hdot(ti, yr)


def fft_spectrum(a_re, a_im, g_re, g_im):
    bsz, r, _, c = a_re.shape
    g_spec = pl.BlockSpec((1, r, r), lambda k, b: (k, 0, 0))
    a_spec = pl.BlockSpec((1, 1, r, c), lambda k, b: (b, k, 0, 0))
    out = jax.ShapeDtypeStruct(a_re.shape, F32)
    return pl.pallas_call(
        _fft_spectrum_kernel,
        grid=(r, bsz),
        in_specs=[g_spec, g_spec, a_spec, a_spec],
        out_specs=[a_spec, a_spec],
        out_shape=[out, out],
        compiler_params=pltpu.CompilerParams(dimension_semantics=("parallel", "parallel"), vmem_limit_bytes=VMEM_LIMIT),
        name="fft_spectrum",
    )(g_re, g_im, a_re, a_im)


def fft_filter_mid(a_re, a_im, h_re, h_im, g_re, g_im):
    bsz, r, _, c = a_re.shape
    gt_re, gt_im = jnp.swapaxes(g_re, 1, 2), jnp.swapaxes(g_im, 1, 2)
    g_spec = pl.BlockSpec((1, r, r), lambda k: (k, 0, 0))
    a_spec = pl.BlockSpec((bsz, 1, r, c), lambda k: (0, k, 0, 0))
    h_spec = pl.BlockSpec((1, 1, r, c), lambda k: (0, k, 0, 0))
    out = jax.ShapeDtypeStruct(a_re.shape, F32)
    return pl.pallas_call(
        _fft_mid_kernel,
        grid=(r,),
        in_specs=[g_spec, g_spec, g_spec, g_spec, a_spec, a_spec, h_spec, h_spec],
        out_specs=[a_spec, a_spec],
        out_shape=[out, out],
        compiler_params=pltpu.CompilerParams(dimension_semantics=("parallel",), vmem_limit_bytes=VMEM_LIMIT),
        name="fft_filter_mid",
    )(g_re, g_im, gt_re, gt_im, a_re, a_im, h_re, h_im)


def _ifft_rows_kernel(c_ref, s_ref, b_re_ref, b_im_ref, y_ref):
    y = _hdot(_split(c_ref[...]), _split(b_re_ref[0])) + _hdot(_split(s_ref[...]), _split(b_im_ref[0]))
    y_ref[0] = y * (1.0 / (FFT_R * FFT_R))


def ifft_rows(b_re, b_im, f_re, f_im, rows):
    bsz, r, m = b_re.shape
    f_spec = pl.BlockSpec((rows, r), lambda b, j: (0, 0))
    b_spec = pl.BlockSpec((1, r, FFT_COLS), lambda b, j: (b, 0, j))
    return pl.pallas_call(
        _ifft_rows_kernel,
        grid=(bsz, m // FFT_COLS),
        in_specs=[f_spec, f_spec, b_spec, b_spec],
        out_specs=pl.BlockSpec((1, rows, FFT_COLS), lambda b, j: (b, 0, j)),
        out_shape=jax.ShapeDtypeStruct((bsz, rows, m), F32),
        compiler_params=pltpu.CompilerParams(dimension_semantics=("parallel", "parallel"), vmem_limit_bytes=VMEM_LIMIT),
        name="ifft_rows",
    )(f_re[:rows], f_im[:rows], b_re, b_im)


def long_conv_fft(u, taps):
    bsz, seq_len, c = u.shape
    r = FFT_R
    f_re, f_im, g_re, g_im = _dft_tables()
    t_re, t_im = fft_rows(taps.reshape(1, r, r * c), f_re, f_im)
    h_re, h_im = fft_spectrum(t_re.reshape(1, r, r, c), t_im.reshape(1, r, r, c), g_re, g_im)
    half = seq_len // r
    a_re, a_im = fft_rows(u.reshape(bsz, half, r * c), f_re[:, :half], f_im[:, :half])
    b_re, b_im = fft_filter_mid(a_re.reshape(bsz, r, r, c), a_im.reshape(bsz, r, r, c), h_re, h_im, g_re, g_im)
    y = ifft_rows(b_re.reshape(bsz, r, r * c), b_im.reshape(bsz, r, r * c), f_re, f_im, half)
    return y.reshape(bsz, seq_len, c)


def hyena_mixer(zh, p):
    zs = depthwise_conv(zh, p['hy_conv_w']) + p['hy_conv_b']
    x0, x1, v = jnp.split(zs, 3, axis=-1)
    z = x1 * v
    seq_len = zh.shape[1]
    h_fwd, h_bwd = hyena_filters(seq_len, p)
    if 2 * seq_len == FFT_R * FFT_R:
        taps = jnp.concatenate([h_fwd[:1] + h_bwd[:1] + p['hy_bias'], h_fwd[1:], jnp.zeros_like(h_fwd[:1]), h_bwd[:0:-1]], axis=0)
        y = x0 * long_conv_fft(z, taps)
    else:
        y = x0 * (two_sided_long_conv(z, h_fwd, h_bwd) + p['hy_bias'] * z)
    return rms_norm(y, p['hy_norm_g'])


def in_projection(h, p, full):
    bsz, n, k = h.shape
    hb = h.reshape(bsz * n, k).astype(BF16)
    out = {}
    out['s5'] = matmul(hb, p['w_in_s5'], name="in_s5").reshape(bsz, n, -1)
    out['qkv'] = matmul(hb, p['w_in_qkv'], name="in_qkv").reshape(bsz, n, -1)
    ab = matmul(hb, p['w_in_ab'], name="in_ab").reshape(bsz, n, -1)
    out['a'] = ab[..., :2 * DN_HEADS]
    out['b'] = ab[..., 2 * DN_HEADS:4 * DN_HEADS]
    if full:
        out['g'] = matmul(hb, p['w_in_g'], name="in_g").reshape(bsz, n, -1)
        out['hy'] = matmul(hb, p['w_in_hy'], name="in_hy").reshape(bsz, n, -1)
    return out


def token_mixer(h, hc, p, need_ctx_out):
    z = in_projection(h, p, True)
    zc = in_projection(hc, p, need_ctx_out)
    s5_y, s5_yc = s5_mixer(z['s5'], zc['s5'], p, need_ctx_out)
    dn_y, dn_yc = deltanet_mixer(z, zc, p, need_ctx_out)
    y = mm3(jnp.concatenate([s5_y, dn_y, hyena_mixer(z['hy'], p)], axis=-1).astype(BF16), p['w_out'], "out_proj")
    if not need_ctx_out:
        return y, None
    yc = mm3(jnp.concatenate([s5_yc, dn_yc, hyena_mixer(zc['hy'], p)], axis=-1).astype(BF16), p['w_out'], "out_proj_ctx")
    return y, yc


def _ada_kernel(a_ref, w_ref, o_ref):
    o_ref[...] = jnp.dot(a_ref[...].astype(BF16), w_ref[0].astype(BF16), preferred_element_type=F32)


def ada_modulation(c_rows, w_ada, b_ada, layer):
    rows, k = c_rows.shape
    n = w_ada.shape[2]
    tn = 1024
    a = jnp.zeros((8, k), F32).at[:rows].set(jax.nn.silu(c_rows))
    out = pl.pallas_call(
        _ada_kernel,
        grid=(n // tn,),
        in_specs=[pl.BlockSpec((8, k), lambda j: (0, 0)), pl.BlockSpec((1, k, tn), lambda j: (layer, 0, j))],
        out_specs=pl.BlockSpec((8, tn), lambda j: (0, j)),
        out_shape=jax.ShapeDtypeStruct((8, n), F32),
        compiler_params=pltpu.CompilerParams(dimension_semantics=("parallel",), vmem_limit_bytes=VMEM_LIMIT),
        name="ada",
    )(a, w_ada)
    return out[:rows] + b_ada


def trunk_layer(x, xc, c, c_ctx, p, last):
    bsz, n, dm = x.shape
    lc = xc.shape[1]
    ada = ada_modulation(jnp.concatenate([c, c_ctx[None]], axis=0), p['w_ada'], p['b_ada'], p['layer'])
    sh1, sc1, g1, sh2, sc2, g2 = jnp.split(ada[:bsz, None, :], 6, axis=-1)
    mc = jnp.split(ada[bsz], 6, axis=-1)
    y, yc = token_mixer(modulate(x, sh1, sc1), modulate(xc, mc[0], mc[1]), p, not last)
    x = layer_norm(DEEPNORM_ALPHA * x + g1 * y, p['ln1_g'], p['ln1_b'])
    if last:
        f = hier_moe(modulate(x, sh2, sc2).reshape(bsz * n, dm), p).reshape(bsz, n, dm)
        return layer_norm(DEEPNORM_ALPHA * x + g2 * f, p['ln2_g'], p['ln2_b']), None
    xc = layer_norm(DEEPNORM_ALPHA * xc + mc[2] * yc, p['ln1_g'], p['ln1_b'])
    tokens = jnp.concatenate([modulate(x, sh2, sc2).reshape(bsz * n, dm), modulate(xc, mc[3], mc[4]).reshape(bsz * lc, dm)], axis=0)
    f = hier_moe(tokens, p)
    x = layer_norm(DEEPNORM_ALPHA * x + g2 * f[:bsz * n].reshape(bsz, n, dm), p['ln2_g'], p['ln2_b'])
    xc = layer_norm(DEEPNORM_ALPHA * xc + mc[5] * f[bsz * n:].reshape(bsz, lc, dm), p['ln2_g'], p['ln2_b'])
    return x, xc


def kernel(x, c, ctx, c_ctx, w_ada, b_ada, w_in, s5_lam_re, s5_lam_im, s5_log_step, s5_b_re, s5_b_im, s5_c_re, s5_c_im, s5_d, s5_glu_w, s5_glu_b, s5_norm_g, dn_conv_w, dn_a_log, dn_dt_bias, dn_norm_g, hy_conv_w, hy_conv_b, hy_f_w1, hy_f_b1, hy_f_w2, hy_f_b2, hy_f_w3, hy_f_b3, hy_f_freq, hy_f_w4, hy_bias, hy_norm_g, w_out, ln1_g, ln1_b, ln2_g, ln2_b, moe_w_group, moe_b_group, moe_w_expert, moe_b_expert, moe_w_gate, moe_w_up, moe_w_down):
    xc = ctx
    for l in range(DEPTH):
        w_in_l = w_in[l]
        w_ab = jnp.zeros((D_MODEL, LANE), F32).at[:, :4 * DN_HEADS].set(w_in_l[:, COL_DN_A:STATE_COLS])
        w_router = jnp.zeros((D_MODEL, LANE), F32).at[:, :N_GROUPS].set(moe_w_group[l]).at[:, N_GROUPS:N_GROUPS + N_EXPERTS].set(moe_w_expert[l])
        p = {
            'layer': l, 'w_ada': w_ada, 'b_ada': b_ada[l],
            'w_in_s5': w_in_l[:, COL_S5:COL_DN_QKV].astype(BF16),
            'w_in_qkv': w_in_l[:, COL_DN_QKV:COL_DN_A].astype(BF16),
            'w_in_ab': w_ab.astype(BF16),
            'w_in_g': w_in_l[:, COL_DN_G:COL_HY].astype(BF16),
            'w_in_hy': w_in_l[:, COL_HY:].astype(BF16),
            's5_lam_re': s5_lam_re[l], 's5_lam_im': s5_lam_im[l], 's5_log_step': s5_log_step[l],
            's5_b_re': s5_b_re[l], 's5_b_im': s5_b_im[l], 's5_c_re': s5_c_re[l], 's5_c_im': s5_c_im[l],
            's5_d': s5_d[l], 's5_glu_w': s5_glu_w[l].astype(BF16), 's5_glu_b': s5_glu_b[l], 's5_norm_g': s5_norm_g[l],
            'dn_conv_w': dn_conv_w[l], 'dn_a_log': dn_a_log[l], 'dn_dt_bias': dn_dt_bias[l], 'dn_norm_g': dn_norm_g[l],
            'hy_conv_w': hy_conv_w[l], 'hy_conv_b': hy_conv_b[l],
            'hy_f_w1': hy_f_w1[l], 'hy_f_b1': hy_f_b1[l], 'hy_f_w2': hy_f_w2[l], 'hy_f_b2': hy_f_b2[l],
            'hy_f_w3': hy_f_w3[l], 'hy_f_b3': hy_f_b3[l], 'hy_f_freq': hy_f_freq[l], 'hy_f_w4': hy_f_w4[l],
            'hy_bias': hy_bias[l], 'hy_norm_g': hy_norm_g[l], 'w_out': w_out[l].astype(BF16),
            'ln1_g': ln1_g[l], 'ln1_b': ln1_b[l], 'ln2_g': ln2_g[l], 'ln2_b': ln2_b[l],
            'moe_w_router': w_router, 'moe_b_group': moe_b_group[l], 'moe_b_expert': moe_b_expert[l],
            'moe_w_gate': moe_w_gate, 'moe_w_up': moe_w_up, 'moe_w_down': moe_w_down,
        }
        x, xc = trunk_layer(x, xc, c, c_ctx, p, l == DEPTH - 1)
    return x
```
